```python
import math
import jax, jax.numpy as jnp
from jax import lax
import numpy as np

D_MODEL = 1024
BATCH = 8
SEQ = 2048
DEPTH = 4
DEC_BATCH = 128
DEC_SEQ = 1
PAST_LEN = 16384
PAGE_SIZE = 128

POOL_WIDTH = D_MODEL // 2
POOL_WINDOWS = (2, 4, 8, 16)
N_POOL_GROUPS = len(POOL_WINDOWS)
POOL_GROUP = POOL_WIDTH // N_POOL_GROUPS
POOL_STATE = max(POOL_WINDOWS) - 1
SSM_WIDTH = D_MODEL // 2
SSM_GROUP = 16
N_SSM_GROUPS = SSM_WIDTH // SSM_GROUP
SSM_STATE = 64
DT_MIN = 1e-3
DT_MAX = 1e-1
IN_WIDTH = POOL_WIDTH + SSM_WIDTH + 2 * D_MODEL
D_FF = 11 * D_MODEL // 4
N_EXPERTS = 8
TOP_K = 2
D_FF_EXPERT = 7 * D_MODEL // 2
N_DENSE = (DEPTH + 1) // 2
N_MOE = DEPTH // 2
ALPHA = (2.0 * DEPTH) ** 0.25
BETA = (8.0 * DEPTH) ** -0.25
LN_EPS = 1e-5

kernel_name = "hybrid_pool_s5_deepnorm_moe_step"


def layer_norm(x, g, b):
    xf = x.astype(jnp.float32)
    mu = jnp.mean(xf, axis=-1, keepdims=True)
    var = jnp.mean(jnp.square(xf - mu), axis=-1, keepdims=True)
    y = (xf - mu) * lax.rsqrt(var + LN_EPS) * g.astype(jnp.float32) + b.astype(jnp.float32)
    return y.astype(x.dtype)


def pool_mixer(u, prev, start_pos, w_pool, pool_scale):
    L = u.shape[1]
    u_ext = jnp.concatenate([prev.astype(u.dtype), u], axis=1)
    cs = jnp.cumsum(u_ext.astype(jnp.float32), axis=1)
    cs = jnp.concatenate([jnp.zeros_like(cs[:, :1]), cs], axis=1)
    hi = cs[:, POOL_STATE + 1:]
    uf = u.astype(jnp.float32)
    pos = start_pos + jnp.arange(L)
    groups = []
    for gi, w in enumerate(POOL_WINDOWS):
        sl = slice(gi * POOL_GROUP, (gi + 1) * POOL_GROUP)
        lo = cs[:, POOL_STATE + 1 - w: POOL_STATE + 1 - w + L, sl]
        count = jnp.minimum(w, pos + 1).astype(jnp.float32)[None, :, None]
        groups.append((hi[..., sl] - lo) / count - uf[..., sl])
    pooled = jnp.stack(groups, axis=2).astype(u.dtype)
    mixed = jnp.einsum('blgc,gcd->blgd', pooled, w_pool)
    out = mixed.reshape(u.shape) * pool_scale
    return out, u_ext[:, -POOL_STATE:]


def ssm_mixer(u, h0_re, h0_im, lambda_re, lambda_im, log_dt, b_re, b_im, c_re, c_im, d_skip, w_glu, b_glu):
    f32 = jnp.float32
    bsz, L, _ = u.shape
    uf = u.astype(f32).reshape(bsz, L, N_SSM_GROUPS, SSM_GROUP)
    lam = lax.complex(lambda_re.astype(f32), lambda_im.astype(f32))
    dt = jnp.exp(log_dt.astype(f32))[:, None]
    lam_bar = jnp.exp(lam * dt)
    b_mat = lax.complex(b_re.astype(f32), b_im.astype(f32))
    b_bar = ((lam_bar - 1.0) / lam)[..., None] * b_mat
    bu = jnp.einsum('blgc,gpc->blgp', uf.astype(jnp.complex64), b_bar)
    h0 = lax.complex(h0_re.astype(f32), h0_im.astype(f32))
    bu = bu.at[:, 0].add(lam_bar * h0)
    a = jnp.broadcast_to(lam_bar, bu.shape)

    def combine(left, right):
        a_l, b_l = left
        a_r, b_r = right
        return a_l * a_r, a_r * b_l + b_r

    _, h = lax.associative_scan(combine, (a, bu), axis=1)
    c_mat = lax.complex(c_re.astype(f32), c_im.astype(f32))
    y = jnp.real(jnp.einsum('blgp,gcp->blgc', h, c_mat)) + d_skip.astype(f32) * uf
    y = jax.nn.gelu(y.reshape(bsz, L, SSM_WIDTH)).astype(u.dtype)
    out = y * jax.nn.sigmoid(y @ w_glu + b_glu)
    h_last = h[:, -1]
    return out, jnp.real(h_last), jnp.imag(h_last)


def token_mix(x, pool_prev, h0_re, h0_im, start_pos, p):
    z = x @ p['w_in'] + p['b_in']
    u_a, u_b, z_ga, z_gb = jnp.split(
        z, [POOL_WIDTH, POOL_WIDTH + SSM_WIDTH, POOL_WIDTH + SSM_WIDTH + D_MODEL], axis=-1)
    out_a, pool_new = pool_mixer(u_a, pool_prev, start_pos, p['w_pool'], p['pool_scale'])
    out_b, h_re, h_im = ssm_mixer(u_b, h0_re, h0_im, p['lambda_re'], p['lambda_im'], p['log_dt'],
                                  p['b_re'], p['b_im'], p['c_re'], p['c_im'], p['d_skip'],
                                  p['w_glu'], p['b_glu'])
    merged = (jax.nn.sigmoid(z_ga) * (out_a @ p['w_proj_a'])
              + jax.nn.sigmoid(z_gb) * (out_b.astype(x.dtype) @ p['w_proj_b']))
    return merged @ p['w_out'], pool_new, h_re, h_im


def swiglu(x, wg, wu, wd):
    return (jax.nn.silu(x @ wg) * (x @ wu)) @ wd


def moe_swiglu(x, w_router, b_router, wg, wu, wd):
    f32 = jnp.float32
    logits = x.astype(f32) @ w_router.astype(f32) + b_router.astype(f32)
    top_val, top_idx = lax.top_k(logits, TOP_K)
    top_w = jax.nn.softmax(top_val, axis=-1)
    gates = jnp.einsum('blk,blke->ble', top_w, jax.nn.one_hot(top_idx, N_EXPERTS, dtype=f32))
    out = jnp.zeros(x.shape, f32)
    for e in range(N_EXPERTS):
        out = out + gates[..., e:e + 1] * swiglu(x, wg[e], wu[e], wd[e]).astype(f32)
    return out.astype(x.dtype)


def setup_inputs(seed: int = 0) -> dict:
    key = jax.random.key(seed)
    ks = iter(jax.random.split(key, 40))

    def nrm(shape, scale=1.0):
        return jax.random.normal(next(ks), shape, jnp.float32) * scale

    G, P, c = N_SSM_GROUPS, SSM_STATE, SSM_GROUP
    inp = {}
    inp['x_prompt'] = nrm((BATCH, SEQ, D_MODEL))
    inp['x_sample'] = nrm((DEC_BATCH, DEC_SEQ, D_MODEL))
    inp['state_pool'] = nrm((DEPTH, DEC_BATCH, POOL_STATE, POOL_WIDTH))
    inp['state_ssm_re'] = nrm((DEPTH, DEC_BATCH, G, P), 0.1)
    inp['state_ssm_im'] = nrm((DEPTH, DEC_BATCH, G, P), 0.1)
    inp['ln1_g'] = 1.0 + nrm((DEPTH, D_MODEL), 0.02)
    inp['ln1_b'] = nrm((DEPTH, D_MODEL), 0.02)
    inp['w_in'] = nrm((DEPTH, D_MODEL, IN_WIDTH), D_MODEL ** -0.5)
    inp['b_in'] = nrm((DEPTH, IN_WIDTH), 0.02)
    inp['w_pool'] = nrm((DEPTH, N_POOL_GROUPS, POOL_GROUP, POOL_GROUP), POOL_GROUP ** -0.5)
    inp['pool_scale'] = 1.0 + nrm((DEPTH, POOL_WIDTH), 0.02)
    inp['lambda_re'] = -0.5 + nrm((DEPTH, G, P), 0.01)
    inp['lambda_im'] = math.pi * jnp.arange(P, dtype=jnp.float32) + nrm((DEPTH, G, P), 0.01)
    inp['log_dt'] = jax.random.uniform(next(ks), (DEPTH, G), jnp.float32,
                                       math.log(DT_MIN), math.log(DT_MAX))
    inp['b_re'] = nrm((DEPTH, G, P, c), (2.0 * c) ** -0.5)
    inp['b_im'] = nrm((DEPTH, G, P, c), (2.0 * c) ** -0.5)
    inp['c_re'] = nrm((DEPTH, G, c, P), P ** -0.5)
    inp['c_im'] = nrm((DEPTH, G, c, P), P ** -0.5)
    inp['d_skip'] = nrm((DEPTH, G, c))
    inp['w_glu'] = nrm((DEPTH, SSM_WIDTH, SSM_WIDTH), SSM_WIDTH ** -0.5)
    inp['b_glu'] = nrm((DEPTH, SSM_WIDTH), 0.02)
    inp['w_proj_a'] = nrm((DEPTH, POOL_WIDTH, D_MODEL), POOL_WIDTH ** -0.5)
    inp['w_proj_b'] = nrm((DEPTH, SSM_WIDTH, D_MODEL), SSM_WIDTH ** -0.5)
    inp['w_out'] = nrm((DEPTH, D_MODEL, D_MODEL), BETA * D_MODEL ** -0.5)
    inp['ln2_g'] = 1.0 + nrm((DEPTH, D_MODEL), 0.02)
    inp['ln2_b'] = nrm((DEPTH, D_MODEL), 0.02)
    inp['w_ffn_gate'] = nrm((N_DENSE, D_MODEL, D_FF), D_MODEL ** -0.5)
    inp['w_ffn_up'] = nrm((N_DENSE, D_MODEL, D_FF), D_MODEL ** -0.5)
    inp['w_ffn_down'] = nrm((N_DENSE, D_FF, D_MODEL), BETA * D_FF ** -0.5)
    inp['w_router'] = nrm((N_MOE, D_MODEL, N_EXPERTS), D_MODEL ** -0.5)
    inp['b_router'] = nrm((N_MOE, N_EXPERTS), 0.01)
    inp['w_moe_gate'] = nrm((N_MOE, N_EXPERTS, D_MODEL, D_FF_EXPERT), D_MODEL ** -0.5)
    inp['w_moe_up'] = nrm((N_MOE, N_EXPERTS, D_MODEL, D_FF_EXPERT), D_MODEL ** -0.5)
    inp['w_moe_down'] = nrm((N_MOE, N_EXPERTS, D_FF_EXPERT, D_MODEL), BETA * D_FF_EXPERT ** -0.5)
    return inp


def reference(x_prompt, x_sample, state_pool, state_ssm_re, state_ssm_im,
              ln1_g, ln1_b, w_in, b_in, w_pool, pool_scale, lambda_re, lambda_im, log_dt,
              b_re, b_im, c_re, c_im, d_skip, w_glu, b_glu, w_proj_a, w_proj_b, w_out,
              ln2_g, ln2_b, w_ffn_gate, w_ffn_up, w_ffn_down,
              w_router, b_router, w_moe_gate, w_moe_up, w_moe_down):
    y_p, y_s = x_prompt, x_sample
    bp = x_prompt.shape[0]
    zero_pool = jnp.zeros((bp, POOL_STATE, POOL_WIDTH), x_prompt.dtype)
    zero_h = jnp.zeros((bp, N_SSM_GROUPS, SSM_STATE), jnp.float32)
    pool_p, re_p, im_p, pool_s, re_s, im_s = [], [], [], [], [], []
    for l in range(DEPTH):
        p = dict(w_in=w_in[l], b_in=b_in[l], w_pool=w_pool[l], pool_scale=pool_scale[l],
                 lambda_re=lambda_re[l], lambda_im=lambda_im[l], log_dt=log_dt[l],
                 b_re=b_re[l], b_im=b_im[l], c_re=c_re[l], c_im=c_im[l], d_skip=d_skip[l],
                 w_glu=w_glu[l], b_glu=b_glu[l], w_proj_a=w_proj_a[l], w_proj_b=w_proj_b[l],
                 w_out=w_out[l])
        m_p, np_pool, np_re, np_im = token_mix(y_p, zero_pool, zero_h, zero_h, 0, p)
        m_s, ns_pool, ns_re, ns_im = token_mix(y_s, state_pool[l], state_ssm_re[l],
                                               state_ssm_im[l], PAST_LEN, p)
        pool_p.append(np_pool); re_p.append(np_re); im_p.append(np_im)
        pool_s.append(ns_pool); re_s.append(ns_re); im_s.append(ns_im)
        y_p = layer_norm(ALPHA * y_p + m_p.astype(y_p.dtype), ln1_g[l], ln1_b[l])
        y_s = layer_norm(ALPHA * y_s + m_s.astype(y_s.dtype), ln1_g[l], ln1_b[l])
        j = l // 2
        if l % 2 == 0:
            f_p = swiglu(y_p, w_ffn_gate[j], w_ffn_up[j], w_ffn_down[j])
            f_s = swiglu(y_s, w_ffn_gate[j], w_ffn_up[j], w_ffn_down[j])
        else:
            f_p = moe_swiglu(y_p, w_router[j], b_router[j], w_moe_gate[j], w_moe_up[j], w_moe_down[j])
            f_s = moe_swiglu(y_s, w_router[j], b_router[j], w_moe_gate[j], w_moe_up[j], w_moe_down[j])
        y_p = layer_norm(ALPHA * y_p + f_p.astype(y_p.dtype), ln2_g[l], ln2_b[l])
        y_s = layer_norm(ALPHA * y_s + f_s.astype(y_s.dtype), ln2_g[l], ln2_b[l])
    return (y_p, y_s, jnp.stack(pool_p), jnp.stack(re_p), jnp.stack(im_p),
            jnp.stack(pool_s), jnp.stack(re_s), jnp.stack(im_s))
```

```python
import functools
import math

import jax
import jax.numpy as jnp
from jax import lax
from jax.experimental import pallas as pl
from jax.experimental.pallas import tpu as pltpu

F32 = jnp.float32
BF16 = jnp.bfloat16

POOL_WINDOWS = (2, 4, 8, 16)
POOL_STATE = max(POOL_WINDOWS) - 1
POOL_GROUP = 128
SSM_GROUP = 16
SSM_STATE = 64
N_EXPERTS = 8
TOP_K = 2
LN_EPS = 1e-5
PAST_LEN = 16384

LANES = 128
ROW_TILE = 512
MIX_STEPS = 64
SCAN_COLS = 512
MOE_ROWS = 512
MOE_SUPER = 2048
MOE_FF_CHUNK = 512
VMEM_LIMIT = 56 * 1024 * 1024


def _dot(a, b):
    return jnp.dot(a, b, preferred_element_type=F32)


def _layer_norm(r, g, b):
    mu = jnp.mean(r, axis=-1, keepdims=True)
    d = r - mu
    var = jnp.mean(d * d, axis=-1, keepdims=True)
    return d * lax.rsqrt(var + LN_EPS) * g + b


def _params(*sem):
    return pltpu.CompilerParams(dimension_semantics=sem, vmem_limit_bytes=VMEM_LIMIT)


def _ssm_param_kernel(lre_ref, lim_ref, ldt_ref, bre_ref, bim_ref, are_ref, aim_ref, bbre_ref, bbim_ref):
    lre, lim = lre_ref[...], lim_ref[...]
    dt = jnp.exp(ldt_ref[...])
    mag = jnp.exp(lre * dt)
    a_re = mag * jnp.cos(lim * dt)
    a_im = mag * jnp.sin(lim * dt)
    den = lre * lre + lim * lim
    n_re, n_im = a_re - 1.0, a_im
    k_re = (n_re * lre + n_im * lim) / den
    k_im = (n_im * lre - n_re * lim) / den
    b_re, b_im = bre_ref[...], bim_ref[...]
    are_ref[...] = a_re
    aim_ref[...] = a_im
    bbre_ref[...] = k_re * b_re - k_im * b_im
    bbim_ref[...] = k_re * b_im + k_im * b_re


def _ssm_params(lambda_re, lambda_im, log_dt, b_re, b_im):
    g, p = lambda_re.shape
    c = b_re.shape[-1]
    rep = lambda a: jnp.repeat(a, c, axis=0)
    bt = lambda a: jnp.transpose(a, (0, 2, 1)).reshape(g * c, p)
    shape = jax.ShapeDtypeStruct((g * c, p), F32)
    return pl.pallas_call(_ssm_param_kernel, out_shape=(shape,) * 4)(
        rep(lambda_re), rep(lambda_im), rep(jnp.broadcast_to(log_dt[:, None], (g, p))),
        bt(b_re), bt(b_im))


def _block_diag_in(bt, g, c, p):
    m = g // 8
    a = bt.reshape(m, 8, c, p)
    eye = jnp.eye(8, dtype=bt.dtype)
    return jnp.einsum('mgcp,gh->mgchp', a, eye).reshape(m, 8 * c, 8 * p)


def _block_diag_out(cm, g, c, p):
    m = g // 8
    a = jnp.transpose(cm, (0, 2, 1)).reshape(m, 8, p, c)
    eye = jnp.eye(8, dtype=cm.dtype)
    return jnp.einsum('mgpc,gh->mgphc', a, eye).reshape(m, 8 * p, 8 * c)


def _mix_kernel(*refs, n_steps, zero_tail, **kw):
    if not zero_tail:
        _mix_body(*refs, **kw)
        return
    step = pl.program_id(0)
    y_ref = refs[-6]

    @pl.when(step < n_steps)
    def _():
        _mix_body(*refs, **kw)

    @pl.when(step >= n_steps)
    def _():
        y_ref[...] = jnp.zeros(y_ref.shape, F32)


def _mix_body(x_ref, prev_ref, h0_ref, win_ref, bin_ref, wpool_ref, pscale_ref, lam_ref,
              bre_ref, bim_ref, cre_ref, cim_ref, dskip_ref, wglu_ref, bglu_ref,
              wpa_ref, wpb_ref, wout_ref, lng_ref, lnb_ref, *rest,
              nb, tt, start_pos, alpha, aliased):
    if aliased:
        rest = rest[1:]
    y_ref, pool_ref, hlast_ref, ext_ref, bu_ref, hs_ref = rest
    rows = nb * tt
    halo = POOL_STATE * nb
    d_model = x_ref.shape[1]
    pw = pscale_ref.shape[1]
    nst = lam_ref.shape[1] // 2
    step = pl.program_id(0)

    @pl.when(step == 0)
    def _():
        ext_ref[0:halo, :] = prev_ref[...]
        hs_ref[...] = h0_ref[...]

    x = x_ref[...]
    xb = x.astype(BF16)

    u_a = _dot(xb, win_ref[:, 0:pw]) + bin_ref[:, 0:pw]
    ext_ref[halo:halo + rows, :] = u_a
    t_idx = lax.shift_right_logical(lax.broadcasted_iota(jnp.int32, (rows, 1), 0), int(math.log2(nb)))
    pos1 = t_idx + (step * tt + start_pos + 1)
    mixed = []
    for gi, w in enumerate(POOL_WINDOWS):
        sl = slice(gi * POOL_GROUP, (gi + 1) * POOL_GROUP)
        cur = u_a[:, sl]
        acc = cur
        for k in range(1, w):
            acc = acc + ext_ref[halo - k * nb:halo - k * nb + rows, sl]
        count = jnp.minimum(pos1, w).astype(F32)
        pooled = acc / count - cur
        mixed.append(_dot(pooled.astype(BF16), wpool_ref[gi]))
    out_a = jnp.concatenate(mixed, axis=-1) * pscale_ref[...]
    new_halo = ext_ref[rows:rows + halo, :]
    ext_ref[0:halo, :] = new_halo
    pool_ref[...] = new_halo
    proj_a = _dot(out_a.astype(BF16), wpa_ref[...])

    u_b = _dot(xb, win_ref[:, pw:2 * pw]) + bin_ref[:, pw:2 * pw]
    ub16 = u_b.astype(BF16)
    n_m = bre_ref.shape[0]
    kin, kst = bre_ref.shape[1], bre_ref.shape[2]
    for m in range(n_m):
        um = ub16[:, m * kin:(m + 1) * kin]
        bu_ref[:, m * kst:(m + 1) * kst] = _dot(um, bre_ref[m])
        bu_ref[:, nst + m * kst:nst + (m + 1) * kst] = _dot(um, bim_ref[m])
    if tt == 1:
        a_re, a_im = lam_ref[:, 0:nst], lam_ref[:, nst:]
        h_re, h_im = hs_ref[:, 0:nst], hs_ref[:, nst:]
        n_re = a_re * h_re - a_im * h_im + bu_ref[:, 0:nst]
        n_im = a_re * h_im + a_im * h_re + bu_ref[:, nst:]
        bu_ref[:, 0:nst] = n_re
        bu_ref[:, nst:] = n_im
        hs_ref[:, 0:nst] = n_re
        hs_ref[:, nst:] = n_im
    else:
        for q in range(nst // SCAN_COLS):
            c_re = slice(q * SCAN_COLS, (q + 1) * SCAN_COLS)
            c_im = slice(nst + q * SCAN_COLS, nst + (q + 1) * SCAN_COLS)
            a_re = jnp.broadcast_to(lam_ref[:, c_re], (nb, SCAN_COLS))
            a_im = jnp.broadcast_to(lam_ref[:, c_im], (nb, SCAN_COLS))

            def body(t, carry, c_re=c_re, c_im=c_im, a_re=a_re, a_im=a_im):
                h_re, h_im = carry
                r = pl.ds(pl.multiple_of(t * nb, nb), nb)
                n_re = a_re * h_re - a_im * h_im + bu_ref[r, c_re]
                n_im = a_re * h_im + a_im * h_re + bu_ref[r, c_im]
                bu_ref[r, c_re] = n_re
                bu_ref[r, c_im] = n_im
                return n_re, n_im

            h_re, h_im = lax.fori_loop(0, tt, body, (hs_ref[:, c_re], hs_ref[:, c_im]), unroll=2)
            hs_ref[:, c_re] = h_re
            hs_ref[:, c_im] = h_im
    hlast_ref[...] = hs_ref[...]
    ys = []
    for m in range(n_m):
        h_re = bu_ref[:, m * kst:(m + 1) * kst].astype(BF16)
        h_im = bu_ref[:, nst + m * kst:nst + (m + 1) * kst].astype(BF16)
        ys.append(_dot(h_re, cre_ref[m]) - _dot(h_im, cim_ref[m]))
    y = jnp.concatenate(ys, axis=-1) + dskip_ref[...] * u_b
    y = jax.nn.gelu(y)
    out_b = y * jax.nn.sigmoid(_dot(y.astype(BF16), wglu_ref[...]) + bglu_ref[...])
    proj_b = _dot(out_b.astype(BF16), wpb_ref[...])

    z_ga = _dot(xb, win_ref[:, 2 * pw:2 * pw + d_model]) + bin_ref[:, 2 * pw:2 * pw + d_model]
    merged = jax.nn.sigmoid(z_ga) * proj_a
    z_gb = _dot(xb, win_ref[:, 2 * pw + d_model:]) + bin_ref[:, 2 * pw + d_model:]
    merged = merged + jax.nn.sigmoid(z_gb) * proj_b
    mix = _dot(merged.astype(BF16), wout_ref[...])
    out = _layer_norm(alpha * x + mix, lng_ref[...], lnb_ref[...])
    if y_ref.shape[0] == rows:
        y_ref[...] = out
    else:
        y_ref[0:rows, :] = out
        y_ref[rows:, :] = jnp.zeros((y_ref.shape[0] - rows, d_model), F32)


def _token_mix(x_all, row0, n_steps, nb, tt, start_pos, alpha, prev, h0, wts, total_rows, y_buf=None):
    rows = nb * tt
    d_model = x_all.shape[1]
    nst2 = h0.shape[1]
    pw = prev.shape[1]
    halo = POOL_STATE * nb
    aliased = y_buf is not None
    const = lambda a: pl.BlockSpec(a.shape, lambda i, n=a.ndim: (0,) * n)
    in_specs = [pl.BlockSpec((rows, d_model), lambda i: (row0 // rows + i, 0)), const(prev), const(h0)]
    in_specs += [const(w) for w in wts]
    args = [x_all, prev, h0, *wts]
    if aliased:
        in_specs.append(pl.BlockSpec(memory_space=pl.ANY))
        args.append(y_buf)
        y_spec = pl.BlockSpec((ROW_TILE, d_model), lambda i: (row0 // ROW_TILE, 0))
    else:
        y_spec = pl.BlockSpec((rows, d_model), lambda i: (row0 // rows + i, 0))
    out_shape = (jax.ShapeDtypeStruct((total_rows, d_model), F32),
                 jax.ShapeDtypeStruct((halo, pw), F32),
                 jax.ShapeDtypeStruct((nb, nst2), F32))
    out_specs = (y_spec,
                 pl.BlockSpec((halo, pw), lambda i: (0, 0)),
                 pl.BlockSpec((nb, nst2), lambda i: (0, 0)))
    zero_tail = not aliased
    assert aliased or row0 + (n_steps + 1) * rows == total_rows
    kern = functools.partial(_mix_kernel, n_steps=n_steps, zero_tail=zero_tail,
                             nb=nb, tt=tt, start_pos=start_pos, alpha=alpha, aliased=aliased)
    return pl.pallas_call(
        kern, grid=(n_steps + int(zero_tail),), in_specs=in_specs, out_specs=out_specs, out_shape=out_shape,
        scratch_shapes=[pltpu.VMEM((halo + rows, pw), F32), pltpu.VMEM((rows, nst2), F32),
                        pltpu.VMEM((nb, nst2), F32)],
        input_output_aliases={len(args) - 1: 0} if aliased else {},
        compiler_params=_params("arbitrary"))(*args)


def _ffn_kernel(x_ref, wg_ref, wu_ref, wd_ref, g_ref, b_ref, o_ref, xb_ref, acc_ref, *, alpha):
    c = pl.program_id(1)

    @pl.when(c == 0)
    def _():
        xb_ref[...] = x_ref[...].astype(BF16)

    xb = xb_ref[...]
    h = jax.nn.silu(_dot(xb, wg_ref[...])) * _dot(xb, wu_ref[...])
    part = _dot(h.astype(BF16), wd_ref[...])

    @pl.when(c == 0)
    def _():
        acc_ref[...] = part

    @pl.when(c > 0)
    def _():
        acc_ref[...] += part

    @pl.when(c == pl.num_programs(1) - 1)
    def _():
        o_ref[...] = _layer_norm(alpha * x_ref[...] + acc_ref[...], g_ref[...], b_ref[...])


def _dense_ffn(x_all, wg, wu, wd, ln_g, ln_b, alpha):
    n, d_model = x_all.shape
    d_ff = wg.shape[1]
    n_chunks = 2
    fc = d_ff // n_chunks
    assert fc * n_chunks == d_ff and fc % LANES == 0
    return pl.pallas_call(
        functools.partial(_ffn_kernel, alpha=alpha),
        grid=(n // ROW_TILE, n_chunks),
        in_specs=[pl.BlockSpec((ROW_TILE, d_model), lambda i, c: (i, 0)),
                  pl.BlockSpec((d_model, fc), lambda i, c: (0, c)),
                  pl.BlockSpec((d_model, fc), lambda i, c: (0, c)),
                  pl.BlockSpec((fc, d_model), lambda i, c: (c, 0)),
                  pl.BlockSpec((1, d_model), lambda i, c: (0, 0)),
                  pl.BlockSpec((1, d_model), lambda i, c: (0, 0))],
        out_specs=pl.BlockSpec((ROW_TILE, d_model), lambda i, c: (i, 0)),
        out_shape=jax.ShapeDtypeStruct((n, d_model), F32),
        scratch_shapes=[pltpu.VMEM((ROW_TILE, d_model), BF16), pltpu.VMEM((ROW_TILE, d_model), F32)],
        compiler_params=_params("parallel", "arbitrary"))(x_all, wg, wu, wd, ln_g, ln_b)


def _split_bf16(a):
    hi = a.astype(BF16)
    lo = (a - hi.astype(F32)).astype(BF16)
    return hi, lo


def _router_kernel(x_ref, w_ref, b_ref, idx_ref, gate_ref):
    x_hi, x_lo = _split_bf16(x_ref[...])
    w_hi, w_lo = _split_bf16(w_ref[...])
    logits = _dot(x_hi, w_hi) + (_dot(x_lo, w_hi) + _dot(x_hi, w_lo)) + b_ref[...]
    lane = lax.broadcasted_iota(jnp.int32, logits.shape, 1)
    logits = jnp.where(lane < N_EXPERTS, logits, -jnp.inf)
    m1 = jnp.max(logits, axis=-1, keepdims=True)
    i1 = jnp.min(jnp.where(logits == m1, lane, LANES), axis=-1, keepdims=True)
    rest = jnp.where(lane == i1, -jnp.inf, logits)
    m2 = jnp.max(rest, axis=-1, keepdims=True)
    i2 = jnp.min(jnp.where(rest == m2, lane, LANES), axis=-1, keepdims=True)
    e2 = jnp.exp(m2 - m1)
    den = 1.0 + e2
    idx_ref[...] = jnp.where(lane == 0, i1, i2)
    gate_ref[...] = jnp.where(lane == 0, 1.0 / den, e2 / den)


def _router(x_all, w_router, b_router):
    n, d_model = x_all.shape
    w = jnp.pad(w_router, ((0, 0), (0, LANES - N_EXPERTS)))
    b = jnp.pad(b_router, (0, LANES - N_EXPERTS))[None, :]
    return pl.pallas_call(
        _router_kernel, grid=(n // ROW_TILE,),
        in_specs=[pl.BlockSpec((ROW_TILE, d_model), lambda i: (i, 0)),
                  pl.BlockSpec((d_model, LANES), lambda i: (0, 0)),
                  pl.BlockSpec((1, LANES), lambda i: (0, 0))],
        out_specs=(pl.BlockSpec((ROW_TILE, LANES), lambda i: (i, 0)),
                   pl.BlockSpec((ROW_TILE, LANES), lambda i: (i, 0))),
        out_shape=(jax.ShapeDtypeStruct((n, LANES), jnp.int32), jax.ShapeDtypeStruct((n, LANES), F32)),
        compiler_params=_params("parallel"))(x_all, w, b)


def _row_copy(src_hbm, row, dst_ref, k, sem):
    return pltpu.make_async_copy(src_hbm.at[pl.ds(row, 1)], dst_ref.at[pl.ds(k, 1)], sem)


def _gather_kernel(src_ref, x_hbm, o_ref, sem):
    base = pl.program_id(0) * MOE_ROWS

    def start(k, _):
        _row_copy(x_hbm, src_ref[base + k], o_ref, k, sem).start()
        return 0

    def wait(k, _):
        _row_copy(x_hbm, 0, o_ref, k, sem).wait()
        return 0

    lax.fori_loop(0, MOE_ROWS, start, 0)
    lax.fori_loop(0, MOE_ROWS, wait, 0)


def _gather_rows(x_all, src, n_slots):
    d_model = x_all.shape[1]
    return pl.pallas_call(
        _gather_kernel,
        grid_spec=pltpu.PrefetchScalarGridSpec(
            num_scalar_prefetch=1, grid=(n_slots // MOE_ROWS,),
            in_specs=[pl.BlockSpec(memory_space=pl.ANY)],
            out_specs=pl.BlockSpec((MOE_ROWS, d_model), lambda i, s: (i, 0)),
            scratch_shapes=[pltpu.SemaphoreType.DMA(())]),
        out_shape=jax.ShapeDtypeStruct((n_slots, d_model), F32),
        compiler_params=_params("arbitrary"))(src, x_all)


def _moe_kernel(blk_e_ref, nvalid_ref, xs_ref, wg_ref, wu_ref, wd_ref, o_ref, wgb_ref, wub_ref, wdb_ref):
    c, j = pl.program_id(1), pl.program_id(2)
    b = pl.program_id(0) * pl.num_programs(2) + j
    changed = jnp.logical_or(j == 0, blk_e_ref[b] != blk_e_ref[jnp.maximum(b - 1, 0)])

    @pl.when(changed)
    def _():
        wgb_ref[...] = wg_ref[...].astype(BF16)
        wub_ref[...] = wu_ref[...].astype(BF16)
        wdb_ref[...] = wd_ref[...].astype(BF16)

    rows = pl.ds(pl.multiple_of(j * MOE_ROWS, MOE_ROWS), MOE_ROWS)
    valid = b < nvalid_ref[0]

    @pl.when(valid)
    def _():
        xb = xs_ref[rows, :].astype(BF16)
        h = jax.nn.silu(_dot(xb, wgb_ref[...])) * _dot(xb, wub_ref[...])
        part = _dot(h.astype(BF16), wdb_ref[...])

        @pl.when(c == 0)
        def _():
            o_ref[rows, :] = part

        @pl.when(c > 0)
        def _():
            o_ref[rows, :] += part

    @pl.when(jnp.logical_and(jnp.logical_not(valid), c == 0))
    def _():
        o_ref[rows, :] = jnp.zeros((MOE_ROWS, o_ref.shape[1]), F32)


def _grouped_swiglu(xs, blk_e, nvalid, wg, wu, wd):
    n_slots, d_model = xs.shape
    d_ff = wg.shape[2]
    fc = MOE_FF_CHUNK
    jb = MOE_SUPER // MOE_ROWS
    assert d_ff % fc == 0 and n_slots % MOE_SUPER == 0
    w_in_map = lambda s, c, j, be, nv: (be[s * jb + j], 0, c)
    w_out_map = lambda s, c, j, be, nv: (be[s * jb + j], c, 0)
    return pl.pallas_call(
        _moe_kernel,
        grid_spec=pltpu.PrefetchScalarGridSpec(
            num_scalar_prefetch=2, grid=(n_slots // MOE_SUPER, d_ff // fc, jb),
            in_specs=[pl.BlockSpec((MOE_SUPER, d_model), lambda s, c, j, be, nv: (s, 0)),
                      pl.BlockSpec((None, d_model, fc), w_in_map),
                      pl.BlockSpec((None, d_model, fc), w_in_map),
                      pl.BlockSpec((None, fc, d_model), w_out_map)],
            out_specs=pl.BlockSpec((MOE_SUPER, d_model), lambda s, c, j, be, nv: (s, 0)),
            scratch_shapes=[pltpu.VMEM((d_model, fc), BF16), pltpu.VMEM((d_model, fc), BF16),
                            pltpu.VMEM((fc, d_model), BF16)]),
        out_shape=jax.ShapeDtypeStruct((n_slots, d_model), F32),
        compiler_params=_params("arbitrary", "arbitrary", "arbitrary"))(blk_e, nvalid, xs, wg, wu, wd)


def _combine_kernel(slot_ref, x_ref, gate_ref, g_ref, b_ref, ys_hbm, o_ref, buf_ref, sem, *, alpha, n_rows):
    base = pl.program_id(0) * ROW_TILE

    def start(k, _):
        for choice in range(TOP_K):
            _row_copy(ys_hbm, slot_ref[choice * n_rows + base + k], buf_ref.at[choice], k, sem).start()
        return 0

    def wait(k, _):
        for choice in range(TOP_K):
            _row_copy(ys_hbm, 0, buf_ref.at[choice], k, sem).wait()
        return 0

    lax.fori_loop(0, ROW_TILE, start, 0)
    lax.fori_loop(0, ROW_TILE, wait, 0)
    gates = gate_ref[...]
    f = gates[:, 0:1] * buf_ref[0] + gates[:, 1:2] * buf_ref[1]
    o_ref[...] = _layer_norm(alpha * x_ref[...] + f, g_ref[...], b_ref[...])


def _combine(x_all, gates, slots, ys, ln_g, ln_b, alpha):
    n, d_model = x_all.shape
    return pl.pallas_call(
        functools.partial(_combine_kernel, alpha=alpha, n_rows=n),
        grid_spec=pltpu.PrefetchScalarGridSpec(
            num_scalar_prefetch=1, grid=(n // ROW_TILE,),
            in_specs=[pl.BlockSpec((ROW_TILE, d_model), lambda i, s: (i, 0)),
                      pl.BlockSpec((ROW_TILE, LANES), lambda i, s: (i, 0)),
                      pl.BlockSpec((1, d_model), lambda i, s: (0, 0)),
                      pl.BlockSpec((1, d_model), lambda i, s: (0, 0)),
                      pl.BlockSpec(memory_space=pl.ANY)],
            out_specs=pl.BlockSpec((ROW_TILE, d_model), lambda i, s: (i, 0)),
            scratch_shapes=[pltpu.VMEM((TOP_K, ROW_TILE, d_model), F32), pltpu.SemaphoreType.DMA(())]),
        out_shape=jax.ShapeDtypeStruct((n, d_model), F32),
        compiler_params=_params("arbitrary"))(slots, x_all, gates, ln_g, ln_b, ys)


def _routing_tables(idx, n_tokens, n_slots):
    choice = idx[:n_tokens, :TOP_K].reshape(-1)
    onehot = (choice[:, None] == jnp.arange(N_EXPERTS, dtype=jnp.int32)[None, :]).astype(jnp.int32)
    csum = jnp.cumsum(onehot, axis=0)
    rank = jnp.sum(onehot * (csum - onehot), axis=1)
    count = csum[-1]
    blocks = (count + MOE_ROWS - 1) // MOE_ROWS
    blk_end = jnp.cumsum(blocks)
    base = (blk_end - blocks) * MOE_ROWS
    slot = jnp.sum(onehot * base[None, :], axis=1) + rank
    token = jnp.arange(TOP_K * n_tokens, dtype=jnp.int32) // TOP_K
    src = jnp.zeros((n_slots,), jnp.int32).at[slot].set(token)
    nvalid = blk_end[-1]
    blk = jnp.minimum(jnp.arange(n_slots // MOE_ROWS, dtype=jnp.int32), nvalid - 1)
    blk_e = jnp.minimum(jnp.searchsorted(blk_end, blk, side='right'), N_EXPERTS - 1).astype(jnp.int32)
    return slot.reshape(n_tokens, TOP_K), src, blk_e, nvalid.reshape(1).astype(jnp.int32)


def _moe_ffn(x_all, n_tokens, w_router, b_router, wg, wu, wd, ln_g, ln_b, alpha):
    n = x_all.shape[0]
    idx, gates = _router(x_all, w_router, b_router)
    n_slots = TOP_K * n_tokens + N_EXPERTS * MOE_ROWS
    n_slots = -(-n_slots // MOE_SUPER) * MOE_SUPER
    slot, src, blk_e, nvalid = _routing_tables(idx, n_tokens, n_slots)
    xs = _gather_rows(x_all, src, n_slots)
    ys = _grouped_swiglu(xs, blk_e, nvalid, wg, wu, wd)
    slots = jnp.zeros((TOP_K, n), jnp.int32).at[:, :n_tokens].set(slot.T).reshape(-1)
    return _combine(x_all, gates, slots, ys, ln_g, ln_b, alpha)


def kernel(x_prompt, x_sample, state_pool, state_ssm_re, state_ssm_im, ln1_g, ln1_b, w_in, b_in, w_pool, pool_scale, lambda_re, lambda_im, log_dt, b_re, b_im, c_re, c_im, d_skip, w_glu, b_glu, w_proj_a, w_proj_b, w_out, ln2_g, ln2_b, w_ffn_gate, w_ffn_up, w_ffn_down, w_router, b_router, w_moe_gate, w_moe_up, w_moe_down):
    batch, seq, d_model = x_prompt.shape
    dec_batch = x_sample.shape[0]
    depth = w_in.shape[0]
    n_groups, n_state = lambda_re.shape[1], lambda_re.shape[2]
    chan = b_re.shape[-1]
    nst = n_groups * n_state
    pw = state_pool.shape[-1]
    alpha = (2.0 * depth) ** 0.25
    n_prompt = batch * seq
    n_tokens = n_prompt + dec_batch
    total_rows = -(-n_tokens // ROW_TILE) * ROW_TILE
    assert x_sample.shape[1] == 1 and n_prompt % ROW_TILE == 0 and seq % MIX_STEPS == 0
    assert batch * MIX_STEPS == ROW_TILE and dec_batch <= ROW_TILE

    y_all = jnp.concatenate([
        jnp.transpose(x_prompt, (1, 0, 2)).reshape(n_prompt, d_model),
        x_sample.reshape(dec_batch, d_model),
        jnp.zeros((total_rows - n_tokens, d_model), F32)], axis=0)
    zero_prev = jnp.zeros((POOL_STATE * batch, pw), F32)
    zero_h = jnp.zeros((batch, 2 * nst), F32)
    row = lambda a: a[None, :]
    outs = [[] for _ in range(6)]
    for l in range(depth):
        a_re, a_im, bb_re, bb_im = _ssm_params(lambda_re[l], lambda_im[l], log_dt[l], b_re[l], b_im[l])
        lam = jnp.concatenate([a_re[::chan].reshape(1, nst), a_im[::chan].reshape(1, nst)], axis=1)
        wts = (w_in[l].astype(BF16), row(b_in[l]), w_pool[l].astype(BF16), row(pool_scale[l]), lam,
               _block_diag_in(bb_re, n_groups, chan, n_state).astype(BF16),
               _block_diag_in(bb_im, n_groups, chan, n_state).astype(BF16),
               _block_diag_out(c_re[l], n_groups, chan, n_state).astype(BF16),
               _block_diag_out(c_im[l], n_groups, chan, n_state).astype(BF16),
               d_skip[l].reshape(1, -1), w_glu[l].astype(BF16), row(b_glu[l]),
               w_proj_a[l].astype(BF16), w_proj_b[l].astype(BF16), w_out[l].astype(BF16),
               row(ln1_g[l]), row(ln1_b[l]))
        prev_s = jnp.transpose(state_pool[l], (1, 0, 2)).reshape(POOL_STATE * dec_batch, pw)
        h0_s = jnp.concatenate([state_ssm_re[l].reshape(dec_batch, nst),
                                state_ssm_im[l].reshape(dec_batch, nst)], axis=1)
        y_mix, pool_p, h_p = _token_mix(y_all, 0, seq // MIX_STEPS, batch, MIX_STEPS, 0, alpha,
                                        zero_prev, zero_h, wts, total_rows)
        y_mix, pool_s, h_s = _token_mix(y_all, n_prompt, 1, dec_batch, 1, PAST_LEN, alpha,
                                        prev_s, h0_s, wts, total_rows, y_buf=y_mix)
        outs[0].append(jnp.transpose(pool_p.reshape(POOL_STATE, batch, pw), (1, 0, 2)))
        outs[1].append(h_p[:, :nst].reshape(batch, n_groups, n_state))
        outs[2].append(h_p[:, nst:].reshape(batch, n_groups, n_state))
        outs[3].append(jnp.transpose(pool_s.reshape(POOL_STATE, dec_batch, pw), (1, 0, 2)))
        outs[4].append(h_s[:, :nst].reshape(dec_batch, n_groups, n_state))
        outs[5].append(h_s[:, nst:].reshape(dec_batch, n_groups, n_state))
        j = l // 2
        if l % 2 == 0:
            y_all = _dense_ffn(y_mix, w_ffn_gate[j].astype(BF16), w_ffn_up[j].astype(BF16),
                               w_ffn_down[j].astype(BF16), row(ln2_g[l]), row(ln2_b[l]), alpha)
        else:
            y_all = _moe_ffn(y_mix, n_tokens, w_router[j], b_router[j], w_moe_gate[j], w_moe_up[j],
                             w_moe_down[j], row(ln2_g[l]), row(ln2_b[l]), alpha)
    y_p = jnp.transpose(y_all[:n_prompt].reshape(seq, batch, d_model), (1, 0, 2))
    y_s = y_all[n_prompt:n_tokens].reshape(dec_batch, 1, d_model)
    return (y_p, y_s, *[jnp.stack(o) for o in outs])
```

```python
import functools
import math

import jax
import jax.numpy as jnp
from jax import lax
from jax.experimental import pallas as pl
from jax.experimental.pallas import tpu as pltpu

F32 = jnp.float32
BF16 = jnp.bfloat16

POOL_WINDOWS = (2, 4, 8, 16)
POOL_STATE = max(POOL_WINDOWS) - 1
POOL_GROUP = 128
SSM_GROUP = 16
SSM_STATE = 64
N_EXPERTS = 8
TOP_K = 2
LN_EPS = 1e-5
PAST_LEN = 16384

LANES = 128
ROW_TILE = 512
MIX_STEPS = 64
SCAN_COLS = 512
MOE_ROWS = 512
MOE_SUPER = 2048
MOE_FF_CHUNK = 512
GROUP = 16
STAGE_ROWS = TOP_K * ROW_TILE + N_EXPERTS * 2 * GROUP
VMEM_LIMIT = 56 * 1024 * 1024


def _dot(a, b):
    return jnp.dot(a, b, preferred_element_type=F32)


def _layer_norm(r, g, b):
    mu = jnp.mean(r, axis=-1, keepdims=True)
    d = r - mu
    var = jnp.mean(d * d, axis=-1, keepdims=True)
    return d * lax.rsqrt(var + LN_EPS) * g + b


def _params(*sem):
    return pltpu.CompilerParams(dimension_semantics=sem, vmem_limit_bytes=VMEM_LIMIT)


def _ssm_param_kernel(lre_ref, lim_ref, ldt_ref, bre_ref, bim_ref, are_ref, aim_ref, bbre_ref, bbim_ref):
    lre, lim = lre_ref[...], lim_ref[...]
    dt = jnp.exp(ldt_ref[...])
    mag = jnp.exp(lre * dt)
    a_re = mag * jnp.cos(lim * dt)
    a_im = mag * jnp.sin(lim * dt)
    den = lre * lre + lim * lim
    n_re, n_im = a_re - 1.0, a_im
    k_re = (n_re * lre + n_im * lim) / den
    k_im = (n_im * lre - n_re * lim) / den
    b_re, b_im = bre_ref[...], bim_ref[...]
    are_ref[...] = a_re
    aim_ref[...] = a_im
    bbre_ref[...] = k_re * b_re - k_im * b_im
    bbim_ref[...] = k_re * b_im + k_im * b_re


def _ssm_params(lambda_re, lambda_im, log_dt, b_re, b_im):
    g, p = lambda_re.shape
    c = b_re.shape[-1]
    rep = lambda a: jnp.repeat(a, c, axis=0)
    bt = lambda a: jnp.transpose(a, (0, 2, 1)).reshape(g * c, p)
    shape = jax.ShapeDtypeStruct((g * c, p), F32)
    return pl.pallas_call(_ssm_param_kernel, out_shape=(shape,) * 4, name="s5_discretise")(
        rep(lambda_re), rep(lambda_im), rep(jnp.broadcast_to(log_dt[:, None], (g, p))),
        bt(b_re), bt(b_im))


def _block_diag_in(bt, g, c, p):
    m = g // 8
    a = bt.reshape(m, 8, c, p)
    eye = jnp.eye(8, dtype=bt.dtype)
    return jnp.einsum('mgcp,gh->mgchp', a, eye).reshape(m, 8 * c, 8 * p)


def _block_diag_out(cm, g, c, p):
    m = g // 8
    a = jnp.transpose(cm, (0, 2, 1)).reshape(m, 8, p, c)
    eye = jnp.eye(8, dtype=cm.dtype)
    return jnp.einsum('mgpc,gh->mgphc', a, eye).reshape(m, 8 * p, 8 * c)


def _mix_kernel(*refs, n_steps, zero_tail, **kw):
    if not zero_tail:
        _mix_body(*refs, **kw)
        return
    step = pl.program_id(0)
    y_ref = refs[-6]

    @pl.when(step < n_steps)
    def _():
        _mix_body(*refs, **kw)

    @pl.when(step >= n_steps)
    def _():
        y_ref[...] = jnp.zeros(y_ref.shape, F32)


def _mix_body(x_ref, prev_ref, h0_ref, win_ref, bin_ref, wpool_ref, pscale_ref, lam_ref,
              bre_ref, bim_ref, cre_ref, cim_ref, dskip_ref, wglu_ref, bglu_ref,
              wpa_ref, wpb_ref, wout_ref, lng_ref, lnb_ref, *rest,
              nb, tt, start_pos, alpha, aliased):
    if aliased:
        rest = rest[1:]
    y_ref, pool_ref, hlast_ref, ext_ref, bu_ref, hs_ref = rest
    rows = nb * tt
    halo = POOL_STATE * nb
    d_model = x_ref.shape[1]
    pw = pscale_ref.shape[1]
    nst = lam_ref.shape[1] // 2
    step = pl.program_id(0)

    @pl.when(step == 0)
    def _():
        ext_ref[0:halo, :] = prev_ref[...]
        hs_ref[...] = h0_ref[...]

    x = x_ref[...]
    xb = x.astype(BF16)

    u_a = _dot(xb, win_ref[:, 0:pw]) + bin_ref[:, 0:pw]
    ext_ref[halo:halo + rows, :] = u_a
    t_idx = lax.shift_right_logical(lax.broadcasted_iota(jnp.int32, (rows, 1), 0), int(math.log2(nb)))
    pos1 = t_idx + (step * tt + start_pos + 1)
    mixed = []
    for gi, w in enumerate(POOL_WINDOWS):
        sl = slice(gi * POOL_GROUP, (gi + 1) * POOL_GROUP)
        cur = u_a[:, sl]
        acc = cur
        for k in range(1, w):
            acc = acc + ext_ref[halo - k * nb:halo - k * nb + rows, sl]
        count = jnp.minimum(pos1, w).astype(F32)
        pooled = acc / count - cur
        mixed.append(_dot(pooled.astype(BF16), wpool_ref[gi]))
    out_a = jnp.concatenate(mixed, axis=-1) * pscale_ref[...]
    new_halo = ext_ref[rows:rows + halo, :]
    ext_ref[0:halo, :] = new_halo
    pool_ref[...] = new_halo
    proj_a = _dot(out_a.astype(BF16), wpa_ref[...])

    u_b = _dot(xb, win_ref[:, pw:2 * pw]) + bin_ref[:, pw:2 * pw]
    ub16 = u_b.astype(BF16)
    n_m = bre_ref.shape[0]
    kin, kst = bre_ref.shape[1], bre_ref.shape[2]
    for m in range(n_m):
        um = ub16[:, m * kin:(m + 1) * kin]
        bu_ref[:, m * kst:(m + 1) * kst] = _dot(um, bre_ref[m])
        bu_ref[:, nst + m * kst:nst + (m + 1) * kst] = _dot(um, bim_ref[m])
    if tt == 1:
        a_re, a_im = lam_ref[:, 0:nst], lam_ref[:, nst:]
        h_re, h_im = hs_ref[:, 0:nst], hs_ref[:, nst:]
        n_re = a_re * h_re - a_im * h_im + bu_ref[:, 0:nst]
        n_im = a_re * h_im + a_im * h_re + bu_ref[:, nst:]
        bu_ref[:, 0:nst] = n_re
        bu_ref[:, nst:] = n_im
        hs_ref[:, 0:nst] = n_re
        hs_ref[:, nst:] = n_im
    else:
        for q in range(nst // SCAN_COLS):
            c_re = slice(q * SCAN_COLS, (q + 1) * SCAN_COLS)
            c_im = slice(nst + q * SCAN_COLS, nst + (q + 1) * SCAN_COLS)
            a_re = jnp.broadcast_to(lam_ref[:, c_re], (nb, SCAN_COLS))
            a_im = jnp.broadcast_to(lam_ref[:, c_im], (nb, SCAN_COLS))

            def body(t, carry, c_re=c_re, c_im=c_im, a_re=a_re, a_im=a_im):
                h_re, h_im = carry
                r = pl.ds(pl.multiple_of(t * nb, nb), nb)
                n_re = a_re * h_re - a_im * h_im + bu_ref[r, c_re]
                n_im = a_re * h_im + a_im * h_re + bu_ref[r, c_im]
                bu_ref[r, c_re] = n_re
                bu_ref[r, c_im] = n_im
                return n_re, n_im

            h_re, h_im = lax.fori_loop(0, tt, body, (hs_ref[:, c_re], hs_ref[:, c_im]), unroll=2)
            hs_ref[:, c_re] = h_re
            hs_ref[:, c_im] = h_im
    hlast_ref[...] = hs_ref[...]
    ys = []
    for m in range(n_m):
        h_re = bu_ref[:, m * kst:(m + 1) * kst].astype(BF16)
        h_im = bu_ref[:, nst + m * kst:nst + (m + 1) * kst].astype(BF16)
        ys.append(_dot(h_re, cre_ref[m]) - _dot(h_im, cim_ref[m]))
    y = jnp.concatenate(ys, axis=-1) + dskip_ref[...] * u_b
    y = jax.nn.gelu(y)
    out_b = y * jax.nn.sigmoid(_dot(y.astype(BF16), wglu_ref[...]) + bglu_ref[...])
    proj_b = _dot(out_b.astype(BF16), wpb_ref[...])

    z_ga = _dot(xb, win_ref[:, 2 * pw:2 * pw + d_model]) + bin_ref[:, 2 * pw:2 * pw + d_model]
    merged = jax.nn.sigmoid(z_ga) * proj_a
    z_gb = _dot(xb, win_ref[:, 2 * pw + d_model:]) + bin_ref[:, 2 * pw + d_model:]
    merged = merged + jax.nn.sigmoid(z_gb) * proj_b
    mix = _dot(merged.astype(BF16), wout_ref[...])
    out = _layer_norm(alpha * x + mix, lng_ref[...], lnb_ref[...])
    if y_ref.shape[0] == rows:
        y_ref[...] = out
    else:
        y_ref[0:rows, :] = out
        y_ref[rows:, :] = jnp.zeros((y_ref.shape[0] - rows, d_model), F32)


def _token_mix(x_all, row0, n_steps, nb, tt, start_pos, alpha, prev, h0, wts, total_rows, y_buf=None):
    rows = nb * tt
    d_model = x_all.shape[1]
    nst2 = h0.shape[1]
    pw = prev.shape[1]
    halo = POOL_STATE * nb
    aliased = y_buf is not None
    const = lambda a: pl.BlockSpec(a.shape, lambda i, n=a.ndim: (0,) * n)
    in_specs = [pl.BlockSpec((rows, d_model), lambda i: (row0 // rows + i, 0)), const(prev), const(h0)]
    in_specs += [const(w) for w in wts]
    args = [x_all, prev, h0, *wts]
    if aliased:
        in_specs.append(pl.BlockSpec(memory_space=pl.ANY))
        args.append(y_buf)
        y_spec = pl.BlockSpec((ROW_TILE, d_model), lambda i: (row0 // ROW_TILE, 0))
    else:
        y_spec = pl.BlockSpec((rows, d_model), lambda i: (row0 // rows + i, 0))
    out_shape = (jax.ShapeDtypeStruct((total_rows, d_model), F32),
                 jax.ShapeDtypeStruct((halo, pw), F32),
                 jax.ShapeDtypeStruct((nb, nst2), F32))
    out_specs = (y_spec,
                 pl.BlockSpec((halo, pw), lambda i: (0, 0)),
                 pl.BlockSpec((nb, nst2), lambda i: (0, 0)))
    zero_tail = not aliased
    assert aliased or row0 + (n_steps + 1) * rows == total_rows
    kern = functools.partial(_mix_kernel, n_steps=n_steps, zero_tail=zero_tail,
                             nb=nb, tt=tt, start_pos=start_pos, alpha=alpha, aliased=aliased)
    return pl.pallas_call(
        kern, grid=(n_steps + int(zero_tail),), in_specs=in_specs, out_specs=out_specs, out_shape=out_shape,
        scratch_shapes=[pltpu.VMEM((halo + rows, pw), F32), pltpu.VMEM((rows, nst2), F32),
                        pltpu.VMEM((nb, nst2), F32)],
        input_output_aliases={len(args) - 1: 0} if aliased else {},
        name="token_mix_decode" if aliased else "token_mix_prompt",
        compiler_params=_params("arbitrary"))(*args)


def _ffn_kernel(x_ref, wg_ref, wu_ref, wd_ref, g_ref, b_ref, o_ref, xb_ref, acc_ref, *, alpha):
    c = pl.program_id(1)

    @pl.when(c == 0)
    def _():
        xb_ref[...] = x_ref[...].astype(BF16)

    xb = xb_ref[...]
    h = jax.nn.silu(_dot(xb, wg_ref[...])) * _dot(xb, wu_ref[...])
    part = _dot(h.astype(BF16), wd_ref[...])

    @pl.when(c == 0)
    def _():
        acc_ref[...] = part

    @pl.when(c > 0)
    def _():
        acc_ref[...] += part

    @pl.when(c == pl.num_programs(1) - 1)
    def _():
        o_ref[...] = _layer_norm(alpha * x_ref[...] + acc_ref[...], g_ref[...], b_ref[...])


def _dense_ffn(x_all, wg, wu, wd, ln_g, ln_b, alpha):
    n, d_model = x_all.shape
    d_ff = wg.shape[1]
    n_chunks = 2
    fc = d_ff // n_chunks
    assert fc * n_chunks == d_ff and fc % LANES == 0
    return pl.pallas_call(
        functools.partial(_ffn_kernel, alpha=alpha),
        grid=(n // ROW_TILE, n_chunks),
        in_specs=[pl.BlockSpec((ROW_TILE, d_model), lambda i, c: (i, 0)),
                  pl.BlockSpec((d_model, fc), lambda i, c: (0, c)),
                  pl.BlockSpec((d_model, fc), lambda i, c: (0, c)),
                  pl.BlockSpec((fc, d_model), lambda i, c: (c, 0)),
                  pl.BlockSpec((1, d_model), lambda i, c: (0, 0)),
                  pl.BlockSpec((1, d_model), lambda i, c: (0, 0))],
        out_specs=pl.BlockSpec((ROW_TILE, d_model), lambda i, c: (i, 0)),
        out_shape=jax.ShapeDtypeStruct((n, d_model), F32),
        scratch_shapes=[pltpu.VMEM((ROW_TILE, d_model), BF16), pltpu.VMEM((ROW_TILE, d_model), F32)],
        name="dense_swiglu_ln",
        compiler_params=_params("parallel", "arbitrary"))(x_all, wg, wu, wd, ln_g, ln_b)


def _split_bf16(a):
    hi = a.astype(BF16)
    lo = (a - hi.astype(F32)).astype(BF16)
    return hi, lo


def _router_kernel(x_ref, w_ref, b_ref, idx_ref, gate_ref):
    x_hi, x_lo = _split_bf16(x_ref[...])
    w_hi, w_lo = _split_bf16(w_ref[...])
    logits = _dot(x_hi, w_hi) + (_dot(x_lo, w_hi) + _dot(x_hi, w_lo)) + b_ref[...]
    lane = lax.broadcasted_iota(jnp.int32, logits.shape, 1)
    logits = jnp.where(lane < N_EXPERTS, logits, -jnp.inf)
    m1 = jnp.max(logits, axis=-1, keepdims=True)
    i1 = jnp.min(jnp.where(logits == m1, lane, LANES), axis=-1, keepdims=True)
    rest = jnp.where(lane == i1, -jnp.inf, logits)
    m2 = jnp.max(rest, axis=-1, keepdims=True)
    i2 = jnp.min(jnp.where(rest == m2, lane, LANES), axis=-1, keepdims=True)
    e2 = jnp.exp(m2 - m1)
    den = 1.0 + e2
    idx_ref[...] = jnp.where(lane == 0, i1, i2)
    gate_ref[...] = jnp.where(lane == 0, 1.0 / den, e2 / den)


def _router(x_all, w_router, b_router):
    n, d_model = x_all.shape
    w = jnp.pad(w_router, ((0, 0), (0, LANES - N_EXPERTS)))
    b = jnp.pad(b_router, (0, LANES - N_EXPERTS))[None, :]
    return pl.pallas_call(
        _router_kernel, grid=(n // ROW_TILE,),
        in_specs=[pl.BlockSpec((ROW_TILE, d_model), lambda i: (i, 0)),
                  pl.BlockSpec((d_model, LANES), lambda i: (0, 0)),
                  pl.BlockSpec((1, LANES), lambda i: (0, 0))],
        out_specs=(pl.BlockSpec((ROW_TILE, LANES), lambda i: (i, 0)),
                   pl.BlockSpec((ROW_TILE, LANES), lambda i: (i, 0))),
        out_shape=(jax.ShapeDtypeStruct((n, LANES), jnp.int32), jax.ShapeDtypeStruct((n, LANES), F32)),
        name="moe_router",
        compiler_params=_params("parallel"))(x_all, w, b)


def _group_copy(src_ref, src_row, dst_ref, dst_row, sem):
    return pltpu.make_async_copy(src_ref.at[pl.ds(pl.multiple_of(src_row, GROUP), GROUP)],
                                 dst_ref.at[pl.ds(pl.multiple_of(dst_row, GROUP), GROUP)], sem)


def _wait_groups(count, src_ref, dst_ref, sem):
    def wait(_, carry):
        _group_copy(src_ref, 0, dst_ref, 0, sem).wait()
        return carry
    lax.fori_loop(0, count, wait, 0)


def _regroup_kernel(w0_ref, off_ref, ngrp_ref, part_ref, pend_ref, x_ref, pos_ref, xs_hbm,
                    stage_ref, carry_ref, zero_ref, sem):
    step = pl.program_id(0)

    @pl.when(step == 0)
    def _():
        carry_ref[...] = jnp.zeros(carry_ref.shape, BF16)
        zero_ref[...] = jnp.zeros(zero_ref.shape, BF16)

    xb = x_ref[...].astype(BF16)
    stage_row = lax.broadcasted_iota(jnp.int32, (STAGE_ROWS, ROW_TILE), 0)
    pos = pos_ref[...]
    hit = jnp.logical_or(stage_row == pos[0:1, :], stage_row == pos[1:2, :])
    stage_ref[...] = _dot(jnp.where(hit, 1.0, 0.0).astype(BF16), xb).astype(BF16)

    started = 0
    for e in range(N_EXPERTS):
        k = step * N_EXPERTS + e
        off, w0, ngrp = off_ref[k], w0_ref[k], ngrp_ref[k]
        head = pl.ds(pl.multiple_of(off, GROUP), GROUP)
        stage_ref[head, :] = stage_ref[head, :] + carry_ref[e]

        def send(q, carry, off=off, w0=w0):
            _group_copy(stage_ref, off + q * GROUP, xs_hbm, w0 + q * GROUP, sem).start()
            return carry

        lax.fori_loop(0, ngrp, send, 0)
        tail = stage_ref[pl.ds(pl.multiple_of(off + ngrp * GROUP, GROUP), GROUP), :]
        carry_ref[e] = jnp.where(part_ref[k] != 0, tail, jnp.zeros_like(tail))
        started = started + ngrp
    _wait_groups(started, stage_ref, xs_hbm, sem)

    @pl.when(step == pl.num_programs(0) - 1)
    def _():
        flushed = 0
        for e in range(N_EXPERTS):
            k = step * N_EXPERTS + e
            row = w0_ref[k] + ngrp_ref[k] * GROUP
            is_open = part_ref[k]

            @pl.when(is_open != 0)
            def _(row=row, e=e):
                pltpu.make_async_copy(carry_ref.at[e], xs_hbm.at[pl.ds(pl.multiple_of(row, GROUP), GROUP)],
                                      sem).start()

            row = row + is_open * GROUP
            n_zero = (pend_ref[e] - row) // GROUP

            def fill(q, carry, row=row):
                _group_copy(zero_ref, 0, xs_hbm, row + q * GROUP, sem).start()
                return carry

            lax.fori_loop(0, n_zero, fill, 0)
            flushed = flushed + is_open + n_zero
        _wait_groups(flushed, zero_ref, xs_hbm, sem)


def _regroup(x_all, pos_row, tables, n_slots):
    n, d_model = x_all.shape
    return pl.pallas_call(
        _regroup_kernel,
        grid_spec=pltpu.PrefetchScalarGridSpec(
            num_scalar_prefetch=5, grid=(n // ROW_TILE,),
            in_specs=[pl.BlockSpec((ROW_TILE, d_model), lambda i, *_: (i, 0)),
                      pl.BlockSpec((TOP_K, ROW_TILE), lambda i, *_: (0, i))],
            out_specs=pl.BlockSpec(memory_space=pl.ANY),
            scratch_shapes=[pltpu.VMEM((STAGE_ROWS, d_model), BF16),
                            pltpu.VMEM((N_EXPERTS, GROUP, d_model), BF16),
                            pltpu.VMEM((GROUP, d_model), BF16),
                            pltpu.SemaphoreType.DMA(())]),
        out_shape=jax.ShapeDtypeStruct((n_slots, d_model), BF16),
        name="moe_regroup",
        compiler_params=_params("arbitrary"))(
            tables["w0"], tables["off"], tables["ngrp"], tables["part"], tables["pend"], x_all, pos_row)


def _moe_kernel(blk_e_ref, nvalid_ref, xs_ref, wg_ref, wu_ref, wd_ref, o_ref,
                wgb_ref, wub_ref, wdb_ref, acc_ref):
    c, j = pl.program_id(1), pl.program_id(2)
    last_c = pl.num_programs(1) - 1
    b = pl.program_id(0) * pl.num_programs(2) + j
    changed = jnp.logical_or(j == 0, blk_e_ref[b] != blk_e_ref[jnp.maximum(b - 1, 0)])

    @pl.when(changed)
    def _():
        wgb_ref[...] = wg_ref[...].astype(BF16)
        wub_ref[...] = wu_ref[...].astype(BF16)
        wdb_ref[...] = wd_ref[...].astype(BF16)

    rows = pl.ds(pl.multiple_of(j * MOE_ROWS, MOE_ROWS), MOE_ROWS)
    valid = b < nvalid_ref[0]

    @pl.when(valid)
    def _():
        xb = xs_ref[rows, :]
        h = jax.nn.silu(_dot(xb, wgb_ref[...])) * _dot(xb, wub_ref[...])
        part = _dot(h.astype(BF16), wdb_ref[...])

        @pl.when(c == 0)
        def _():
            acc_ref[rows, :] = part

        @pl.when(jnp.logical_and(c > 0, c < last_c))
        def _():
            acc_ref[rows, :] += part

        @pl.when(c == last_c)
        def _():
            o_ref[rows, :] = (acc_ref[rows, :] + part).astype(BF16)

    @pl.when(jnp.logical_and(jnp.logical_not(valid), c == last_c))
    def _():
        o_ref[rows, :] = jnp.zeros((MOE_ROWS, o_ref.shape[1]), BF16)


def _grouped_swiglu(xs, blk_e, nvalid, layer, wg, wu, wd):
    n_slots, d_model = xs.shape
    d_ff = wg.shape[3]
    fc = MOE_FF_CHUNK
    jb = MOE_SUPER // MOE_ROWS
    assert d_ff % fc == 0 and d_ff // fc >= 2 and n_slots % MOE_SUPER == 0
    w_in_map = lambda s, c, j, be, nv: (layer, be[s * jb + j], 0, c)
    w_out_map = lambda s, c, j, be, nv: (layer, be[s * jb + j], c, 0)
    return pl.pallas_call(
        _moe_kernel,
        grid_spec=pltpu.PrefetchScalarGridSpec(
            num_scalar_prefetch=2, grid=(n_slots // MOE_SUPER, d_ff // fc, jb),
            in_specs=[pl.BlockSpec((MOE_SUPER, d_model), lambda s, c, j, be, nv: (s, 0)),
                      pl.BlockSpec((None, None, d_model, fc), w_in_map),
                      pl.BlockSpec((None, None, d_model, fc), w_in_map),
                      pl.BlockSpec((None, None, fc, d_model), w_out_map)],
            out_specs=pl.BlockSpec((MOE_SUPER, d_model), lambda s, c, j, be, nv: (s, 0)),
            scratch_shapes=[pltpu.VMEM((d_model, fc), BF16), pltpu.VMEM((d_model, fc), BF16),
                            pltpu.VMEM((fc, d_model), BF16), pltpu.VMEM((MOE_SUPER, d_model), F32)]),
        out_shape=jax.ShapeDtypeStruct((n_slots, d_model), BF16),
        name="moe_grouped_swiglu",
        compiler_params=_params("arbitrary", "arbitrary", "arbitrary"))(blk_e, nvalid, xs, wg, wu, wd)


def _combine_kernel(w0_ref, off_ref, nwin_ref, x_ref, pos_ref, gate_ref, g_ref, b_ref, ys_hbm, o_ref,
                    stage_ref, sem, *, alpha):
    step = pl.program_id(0)
    stage_ref[...] = jnp.zeros(stage_ref.shape, BF16)
    started = 0
    for e in range(N_EXPERTS):
        k = step * N_EXPERTS + e
        off, w0, nwin = off_ref[k], w0_ref[k], nwin_ref[k]

        def fetch(q, carry, off=off, w0=w0):
            _group_copy(ys_hbm, w0 + q * GROUP, stage_ref, off + q * GROUP, sem).start()
            return carry

        lax.fori_loop(0, nwin, fetch, 0)
        started = started + nwin
    _wait_groups(started, ys_hbm, stage_ref, sem)

    stage_col = lax.broadcasted_iota(jnp.int32, (ROW_TILE, STAGE_ROWS), 1)
    pos = pos_ref[...]
    gates = gate_ref[...]
    ys = stage_ref[...]
    f = None
    for choice in range(TOP_K):
        pick = jnp.where(stage_col == pos[:, choice:choice + 1], 1.0, 0.0).astype(BF16)
        term = gates[:, choice:choice + 1] * _dot(pick, ys)
        f = term if f is None else f + term
    o_ref[...] = _layer_norm(alpha * x_ref[...] + f, g_ref[...], b_ref[...])


def _combine(x_all, gates, pos_col, tables, ys, ln_g, ln_b, alpha):
    n, d_model = x_all.shape
    return pl.pallas_call(
        functools.partial(_combine_kernel, alpha=alpha),
        grid_spec=pltpu.PrefetchScalarGridSpec(
            num_scalar_prefetch=3, grid=(n // ROW_TILE,),
            in_specs=[pl.BlockSpec((ROW_TILE, d_model), lambda i, *_: (i, 0)),
                      pl.BlockSpec((ROW_TILE, TOP_K), lambda i, *_: (i, 0)),
                      pl.BlockSpec((ROW_TILE, LANES), lambda i, *_: (i, 0)),
                      pl.BlockSpec((1, d_model), lambda i, *_: (0, 0)),
                      pl.BlockSpec((1, d_model), lambda i, *_: (0, 0)),
                      pl.BlockSpec(memory_space=pl.ANY)],
            out_specs=pl.BlockSpec((ROW_TILE, d_model), lambda i, *_: (i, 0)),
            scratch_shapes=[pltpu.VMEM((STAGE_ROWS, d_model), BF16), pltpu.SemaphoreType.DMA(())]),
        out_shape=jax.ShapeDtypeStruct((n, d_model), F32),
        name="moe_combine_ln",
        compiler_params=_params("arbitrary"))(
            tables["w0"], tables["off"], tables["nwin"], x_all, pos_col, gates, ln_g, ln_b, ys)


def _routing_tables(idx, n_tokens, n_slots):
    n = idx.shape[0]
    n_tiles = n // ROW_TILE
    experts = jnp.arange(N_EXPERTS, dtype=jnp.int32)
    live = jnp.arange(n, dtype=jnp.int32) < n_tokens
    onehot = jnp.logical_and(idx[:, :TOP_K, None] == experts, live[:, None, None])
    onehot = onehot.astype(jnp.int32).reshape(n * TOP_K, N_EXPERTS)
    csum = jnp.cumsum(onehot, axis=0)
    rank = jnp.sum(onehot * (csum - onehot), axis=1)
    count = csum[-1]
    blocks = (count + MOE_ROWS - 1) // MOE_ROWS
    blk_end = jnp.cumsum(blocks)
    base = (blk_end - blocks) * MOE_ROWS
    slot = jnp.sum(onehot * base[None, :], axis=1) + rank
    before = jnp.concatenate([jnp.zeros((1, N_EXPERTS), jnp.int32), csum])[::ROW_TILE * TOP_K]
    start = base[None, :] + before[:-1]
    end = base[None, :] + before[1:]
    w0 = start // GROUP * GROUP
    ngrp = end // GROUP - start // GROUP
    part = (end % GROUP != 0).astype(jnp.int32)
    nwin = ngrp + part
    off = (jnp.cumsum(nwin, axis=1) - nwin) * GROUP
    shift = jnp.repeat(off - w0, ROW_TILE * TOP_K, axis=0)
    pos = jnp.where(jnp.sum(onehot, axis=1) > 0, slot + jnp.sum(onehot * shift, axis=1), -1)
    pos_col = pos.reshape(n, TOP_K)
    nvalid = blk_end[-1]
    blk = jnp.minimum(jnp.arange(n_slots // MOE_ROWS, dtype=jnp.int32), nvalid - 1)
    blk_e = jnp.minimum(jnp.sum((blk[:, None] >= blk_end[None, :]).astype(jnp.int32), axis=1), N_EXPERTS - 1)
    pend = (blk_end * MOE_ROWS).at[-1].set(n_slots)
    flat = lambda a: a.reshape(-1).astype(jnp.int32)
    tables = dict(w0=flat(w0), off=flat(off), ngrp=flat(ngrp), part=flat(part), nwin=flat(nwin),
                  pend=flat(pend))
    return pos_col, tables, blk_e.astype(jnp.int32), nvalid.reshape(1).astype(jnp.int32)


def _moe_ffn(x_all, n_tokens, layer, w_router, b_router, wg, wu, wd, ln_g, ln_b, alpha):
    idx, gates = _router(x_all, w_router, b_router)
    n_slots = TOP_K * n_tokens + N_EXPERTS * MOE_ROWS
    n_slots = -(-n_slots // MOE_SUPER) * MOE_SUPER
    pos_col, tables, blk_e, nvalid = _routing_tables(idx, n_tokens, n_slots)
    xs = _regroup(x_all, pos_col.T, tables, n_slots)
    ys = _grouped_swiglu(xs, blk_e, nvalid, layer, wg, wu, wd)
    return _combine(x_all, gates, pos_col, tables, ys, ln_g, ln_b, alpha)


def kernel(x_prompt, x_sample, state_pool, state_ssm_re, state_ssm_im, ln1_g, ln1_b, w_in, b_in, w_pool, pool_scale, lambda_re, lambda_im, log_dt, b_re, b_im, c_re, c_im, d_skip, w_glu, b_glu, w_proj_a, w_proj_b, w_out, ln2_g, ln2_b, w_ffn_gate, w_ffn_up, w_ffn_down, w_router, b_router, w_moe_gate, w_moe_up, w_moe_down):
    batch, seq, d_model = x_prompt.shape
    dec_batch = x_sample.shape[0]
    depth = w_in.shape[0]
    n_groups, n_state = lambda_re.shape[1], lambda_re.shape[2]
    chan = b_re.shape[-1]
    nst = n_groups * n_state
    pw = state_pool.shape[-1]
    alpha = (2.0 * depth) ** 0.25
    n_prompt = batch * seq
    n_tokens = n_prompt + dec_batch
    total_rows = -(-n_tokens // ROW_TILE) * ROW_TILE
    assert x_sample.shape[1] == 1 and n_prompt % ROW_TILE == 0 and seq % MIX_STEPS == 0
    assert batch * MIX_STEPS == ROW_TILE and dec_batch <= ROW_TILE

    y_all = jnp.concatenate([
        jnp.transpose(x_prompt, (1, 0, 2)).reshape(n_prompt, d_model),
        x_sample.reshape(dec_batch, d_model),
        jnp.zeros((total_rows - n_tokens, d_model), F32)], axis=0)
    zero_prev = jnp.zeros((POOL_STATE * batch, pw), F32)
    zero_h = jnp.zeros((batch, 2 * nst), F32)
    row = lambda a: a[None, :]
    outs = [[] for _ in range(6)]
    for l in range(depth):
        a_re, a_im, bb_re, bb_im = _ssm_params(lambda_re[l], lambda_im[l], log_dt[l], b_re[l], b_im[l])
        lam = jnp.concatenate([a_re[::chan].reshape(1, nst), a_im[::chan].reshape(1, nst)], axis=1)
        wts = (w_in[l].astype(BF16), row(b_in[l]), w_pool[l].astype(BF16), row(pool_scale[l]), lam,
               _block_diag_in(bb_re, n_groups, chan, n_state).astype(BF16),
               _block_diag_in(bb_im, n_groups, chan, n_state).astype(BF16),
               _block_diag_out(c_re[l], n_groups, chan, n_state).astype(BF16),
               _block_diag_out(c_im[l], n_groups, chan, n_state).astype(BF16),
               d_skip[l].reshape(1, -1), w_glu[l].astype(BF16), row(b_glu[l]),
               w_proj_a[l].astype(BF16), w_proj_b[l].astype(BF16), w_out[l].astype(BF16),
               row(ln1_g[l]), row(ln1_b[l]))
        prev_s = jnp.transpose(state_pool[l], (1, 0, 2)).reshape(POOL_STATE * dec_batch, pw)
        h0_s = jnp.concatenate([state_ssm_re[l].reshape(dec_batch, nst),
                                state_ssm_im[l].reshape(dec_batch, nst)], axis=1)
        y_mix, pool_p, h_p = _token_mix(y_all, 0, seq // MIX_STEPS, batch, MIX_STEPS, 0, alpha,
                                        zero_prev, zero_h, wts, total_rows)
        y_mix, pool_s, h_s = _token_mix(y_all, n_prompt, 1, dec_batch, 1, PAST_LEN, alpha,
                                        prev_s, h0_s, wts, total_rows, y_buf=y_mix)
        outs[0].append(jnp.transpose(pool_p.reshape(POOL_STATE, batch, pw), (1, 0, 2)))
        outs[1].append(h_p[:, :nst].reshape(batch, n_groups, n_state))
        outs[2].append(h_p[:, nst:].reshape(batch, n_groups, n_state))
        outs[3].append(jnp.transpose(pool_s.reshape(POOL_STATE, dec_batch, pw), (1, 0, 2)))
        outs[4].append(h_s[:, :nst].reshape(dec_batch, n_groups, n_state))
        outs[5].append(h_s[:, nst:].reshape(dec_batch, n_groups, n_state))
        j = l // 2
        if l % 2 == 0:
            y_all = _dense_ffn(y_mix, w_ffn_gate[j].astype(BF16), w_ffn_up[j].astype(BF16),
                               w_ffn_down[j].astype(BF16), row(ln2_g[l]), row(ln2_b[l]), alpha)
        else:
            y_all = _moe_ffn(y_mix, n_tokens, j, w_router[j], b_router[j], w_moe_gate, w_moe_up,
                             w_moe_down, row(ln2_g[l]), row(ln2_b[l]), alpha)
    y_p = jnp.transpose(y_all[:n_prompt].reshape(seq, batch, d_model), (1, 0, 2))
    y_s = y_all[n_prompt:n_tokens].reshape(dec_batch, 1, d_model)
    return (y_p, y_s, *[jnp.stack(o) for o in outs])
```

```python
import functools
import math

import jax
import jax.numpy as jnp
from jax import lax
from jax.experimental import pallas as pl
from jax.experimental.pallas import tpu as pltpu

F32 = jnp.float32
BF16 = jnp.bfloat16

POOL_WINDOWS = (2, 4, 8, 16)
POOL_STATE = max(POOL_WINDOWS) - 1
POOL_GROUP = 128
SSM_GROUP = 16
SSM_STATE = 64
N_EXPERTS = 8
TOP_K = 2
LN_EPS = 1e-5
PAST_LEN = 16384

LANES = 128
ROW_TILE = 512
MIX_STEPS = 64
SCAN_COLS = 512
MOE_ROWS = 512
MOE_SUPER = 2048
MOE_FF_CHUNK = 896
GROUP = 16
STAGE_ROWS = TOP_K * ROW_TILE + N_EXPERTS * 2 * GROUP
VMEM_LIMIT = 56 * 1024 * 1024


def _dot(a, b):
    return jnp.dot(a, b, preferred_element_type=F32)


def _layer_norm(r, g, b):
    mu = jnp.mean(r, axis=-1, keepdims=True)
    d = r - mu
    var = jnp.mean(d * d, axis=-1, keepdims=True)
    return d * lax.rsqrt(var + LN_EPS) * g + b


def _interleave(wg, wu):
    k, f = wg.shape
    pieces = jnp.stack([wg.reshape(k, f // LANES, LANES), wu.reshape(k, f // LANES, LANES)], axis=2)
    return pieces.reshape(k, 2 * f)


def _swiglu(xb, wgu_ref, wd_ref):
    pieces = []
    for p in range(wd_ref.shape[0] // LANES):
        gate_up = _dot(xb, wgu_ref[:, 2 * p * LANES:2 * (p + 1) * LANES])
        pieces.append((jax.nn.silu(gate_up[:, :LANES]) * gate_up[:, LANES:]).astype(BF16))
    return _dot(jnp.concatenate(pieces, axis=-1), wd_ref[...])


def _params(*sem):
    return pltpu.CompilerParams(dimension_semantics=sem, vmem_limit_bytes=VMEM_LIMIT)


def _ssm_param_kernel(lre_ref, lim_ref, ldt_ref, bre_ref, bim_ref, are_ref, aim_ref, bbre_ref, bbim_ref):
    lre, lim = lre_ref[...], lim_ref[...]
    dt = jnp.exp(ldt_ref[...])
    mag = jnp.exp(lre * dt)
    a_re = mag * jnp.cos(lim * dt)
    a_im = mag * jnp.sin(lim * dt)
    den = lre * lre + lim * lim
    n_re, n_im = a_re - 1.0, a_im
    k_re = (n_re * lre + n_im * lim) / den
    k_im = (n_im * lre - n_re * lim) / den
    b_re, b_im = bre_ref[...], bim_ref[...]
    are_ref[...] = a_re
    aim_ref[...] = a_im
    bbre_ref[...] = k_re * b_re - k_im * b_im
    bbim_ref[...] = k_re * b_im + k_im * b_re


def _ssm_params(lambda_re, lambda_im, log_dt, b_re, b_im):
    g, p = lambda_re.shape
    c = b_re.shape[-1]
    rep = lambda a: jnp.repeat(a, c, axis=0)
    bt = lambda a: jnp.transpose(a, (0, 2, 1)).reshape(g * c, p)
    shape = jax.ShapeDtypeStruct((g * c, p), F32)
    return pl.pallas_call(_ssm_param_kernel, out_shape=(shape,) * 4, name="s5_discretise")(
        rep(lambda_re), rep(lambda_im), rep(jnp.broadcast_to(log_dt[:, None], (g, p))),
        bt(b_re), bt(b_im))


def _block_diag_in(bt, g, c, p):
    m = g // 8
    a = bt.reshape(m, 8, c, p)
    eye = jnp.eye(8, dtype=bt.dtype)
    return jnp.einsum('mgcp,gh->mgchp', a, eye).reshape(m, 8 * c, 8 * p)


def _block_diag_out(cm, g, c, p):
    m = g // 8
    a = jnp.transpose(cm, (0, 2, 1)).reshape(m, 8, p, c)
    eye = jnp.eye(8, dtype=cm.dtype)
    return jnp.einsum('mgpc,gh->mgphc', a, eye).reshape(m, 8 * p, 8 * c)


def _mix_kernel(*refs, n_steps, zero_tail, **kw):
    if not zero_tail:
        _mix_body(*refs, **kw)
        return
    step = pl.program_id(0)
    y_ref = refs[-6]

    @pl.when(step < n_steps)
    def _():
        _mix_body(*refs, **kw)

    @pl.when(step >= n_steps)
    def _():
        y_ref[...] = jnp.zeros(y_ref.shape, F32)


def _mix_body(x_ref, prev_ref, h0_ref, win_ref, bin_ref, wpool_ref, pscale_ref, lam_ref,
              bre_ref, bim_ref, cre_ref, cim_ref, dskip_ref, wglu_ref, bglu_ref,
              wpa_ref, wpb_ref, wout_ref, lng_ref, lnb_ref, *rest,
              nb, tt, start_pos, alpha, aliased):
    if aliased:
        rest = rest[1:]
    y_ref, pool_ref, hlast_ref, ext_ref, bu_ref, hs_ref = rest
    rows = nb * tt
    halo = POOL_STATE * nb
    d_model = x_ref.shape[1]
    pw = pscale_ref.shape[1]
    nst = lam_ref.shape[1] // 2
    step = pl.program_id(0)

    @pl.when(step == 0)
    def _():
        ext_ref[0:halo, :] = prev_ref[...]
        hs_ref[...] = h0_ref[...]

    x = x_ref[...]
    xb = x.astype(BF16)

    u_a = _dot(xb, win_ref[:, 0:pw]) + bin_ref[:, 0:pw]
    ext_ref[halo:halo + rows, :] = u_a
    t_idx = lax.shift_right_logical(lax.broadcasted_iota(jnp.int32, (rows, 1), 0), int(math.log2(nb)))
    pos1 = t_idx + (step * tt + start_pos + 1)
    mixed = []
    for gi, w in enumerate(POOL_WINDOWS):
        sl = slice(gi * POOL_GROUP, (gi + 1) * POOL_GROUP)
        cur = u_a[:, sl]
        acc = cur
        for k in range(1, w):
            acc = acc + ext_ref[halo - k * nb:halo - k * nb + rows, sl]
        count = jnp.minimum(pos1, w).astype(F32)
        pooled = acc / count - cur
        mixed.append(_dot(pooled.astype(BF16), wpool_ref[gi]))
    out_a = jnp.concatenate(mixed, axis=-1) * pscale_ref[...]
    new_halo = ext_ref[rows:rows + halo, :]
    ext_ref[0:halo, :] = new_halo
    pool_ref[...] = new_halo
    proj_a = _dot(out_a.astype(BF16), wpa_ref[...])

    u_b = _dot(xb, win_ref[:, pw:2 * pw]) + bin_ref[:, pw:2 * pw]
    ub16 = u_b.astype(BF16)
    n_m = bre_ref.shape[0]
    kin, kst = bre_ref.shape[1], bre_ref.shape[2]
    for m in range(n_m):
        um = ub16[:, m * kin:(m + 1) * kin]
        bu_ref[:, m * kst:(m + 1) * kst] = _dot(um, bre_ref[m])
        bu_ref[:, nst + m * kst:nst + (m + 1) * kst] = _dot(um, bim_ref[m])
    if tt == 1:
        a_re, a_im = lam_ref[:, 0:nst], lam_ref[:, nst:]
        h_re, h_im = hs_ref[:, 0:nst], hs_ref[:, nst:]
        n_re = a_re * h_re - a_im * h_im + bu_ref[:, 0:nst]
        n_im = a_re * h_im + a_im * h_re + bu_ref[:, nst:]
        bu_ref[:, 0:nst] = n_re
        bu_ref[:, nst:] = n_im
        hs_ref[:, 0:nst] = n_re
        hs_ref[:, nst:] = n_im
    else:
        for q in range(nst // SCAN_COLS):
            c_re = slice(q * SCAN_COLS, (q + 1) * SCAN_COLS)
            c_im = slice(nst + q * SCAN_COLS, nst + (q + 1) * SCAN_COLS)
            a_re = jnp.broadcast_to(lam_ref[:, c_re], (nb, SCAN_COLS))
            a_im = jnp.broadcast_to(lam_ref[:, c_im], (nb, SCAN_COLS))

            def body(t, carry, c_re=c_re, c_im=c_im, a_re=a_re, a_im=a_im):
                h_re, h_im = carry
                r = pl.ds(pl.multiple_of(t * nb, nb), nb)
                n_re = a_re * h_re - a_im * h_im + bu_ref[r, c_re]
                n_im = a_re * h_im + a_im * h_re + bu_ref[r, c_im]
                bu_ref[r, c_re] = n_re
                bu_ref[r, c_im] = n_im
                return n_re, n_im

            h_re, h_im = lax.fori_loop(0, tt, body, (hs_ref[:, c_re], hs_ref[:, c_im]), unroll=2)
            hs_ref[:, c_re] = h_re
            hs_ref[:, c_im] = h_im
    hlast_ref[...] = hs_ref[...]
    ys = []
    for m in range(n_m):
        h_re = bu_ref[:, m * kst:(m + 1) * kst].astype(BF16)
        h_im = bu_ref[:, nst + m * kst:nst + (m + 1) * kst].astype(BF16)
        ys.append(_dot(h_re, cre_ref[m]) - _dot(h_im, cim_ref[m]))
    y = jnp.concatenate(ys, axis=-1) + dskip_ref[...] * u_b
    y = jax.nn.gelu(y)
    out_b = y * jax.nn.sigmoid(_dot(y.astype(BF16), wglu_ref[...]) + bglu_ref[...])
    proj_b = _dot(out_b.astype(BF16), wpb_ref[...])

    z_ga = _dot(xb, win_ref[:, 2 * pw:2 * pw + d_model]) + bin_ref[:, 2 * pw:2 * pw + d_model]
    merged = jax.nn.sigmoid(z_ga) * proj_a
    z_gb = _dot(xb, win_ref[:, 2 * pw + d_model:]) + bin_ref[:, 2 * pw + d_model:]
    merged = merged + jax.nn.sigmoid(z_gb) * proj_b
    mix = _dot(merged.astype(BF16), wout_ref[...])
    out = _layer_norm(alpha * x + mix, lng_ref[...], lnb_ref[...])
    if y_ref.shape[0] == rows:
        y_ref[...] = out
    else:
        y_ref[0:rows, :] = out
        y_ref[rows:, :] = jnp.zeros((y_ref.shape[0] - rows, d_model), F32)


def _token_mix(x_all, row0, n_steps, nb, tt, start_pos, alpha, prev, h0, wts, total_rows, y_buf=None):
    rows = nb * tt
    d_model = x_all.shape[1]
    nst2 = h0.shape[1]
    pw = prev.shape[1]
    halo = POOL_STATE * nb
    aliased = y_buf is not None
    const = lambda a: pl.BlockSpec(a.shape, lambda i, n=a.ndim: (0,) * n)
    in_specs = [pl.BlockSpec((rows, d_model), lambda i: (row0 // rows + i, 0)), const(prev), const(h0)]
    in_specs += [const(w) for w in wts]
    args = [x_all, prev, h0, *wts]
    if aliased:
        in_specs.append(pl.BlockSpec(memory_space=pl.ANY))
        args.append(y_buf)
        y_spec = pl.BlockSpec((ROW_TILE, d_model), lambda i: (row0 // ROW_TILE, 0))
    else:
        y_spec = pl.BlockSpec((rows, d_model), lambda i: (row0 // rows + i, 0))
    out_shape = (jax.ShapeDtypeStruct((total_rows, d_model), F32),
                 jax.ShapeDtypeStruct((halo, pw), F32),
                 jax.ShapeDtypeStruct((nb, nst2), F32))
    out_specs = (y_spec,
                 pl.BlockSpec((halo, pw), lambda i: (0, 0)),
                 pl.BlockSpec((nb, nst2), lambda i: (0, 0)))
    zero_tail = not aliased
    assert aliased or row0 + (n_steps + 1) * rows == total_rows
    kern = functools.partial(_mix_kernel, n_steps=n_steps, zero_tail=zero_tail,
                             nb=nb, tt=tt, start_pos=start_pos, alpha=alpha, aliased=aliased)
    return pl.pallas_call(
        kern, grid=(n_steps + int(zero_tail),), in_specs=in_specs, out_specs=out_specs, out_shape=out_shape,
        scratch_shapes=[pltpu.VMEM((halo + rows, pw), F32), pltpu.VMEM((rows, nst2), F32),
                        pltpu.VMEM((nb, nst2), F32)],
        input_output_aliases={len(args) - 1: 0} if aliased else {},
        name="token_mix_decode" if aliased else "token_mix_prompt",
        compiler_params=_params("arbitrary"))(*args)


def _ffn_kernel(x_ref, wgu_ref, wd_ref, g_ref, b_ref, o_ref, *, alpha):
    x = x_ref[...]
    ffn = _swiglu(x.astype(BF16), wgu_ref, wd_ref)
    o_ref[...] = _layer_norm(alpha * x + ffn, g_ref[...], b_ref[...])


def _dense_ffn(x_all, wgu, wd, ln_g, ln_b, alpha):
    n, d_model = x_all.shape
    resident = lambda a: pl.BlockSpec(a.shape, lambda i: (0, 0), pipeline_mode=pl.Buffered(1))
    return pl.pallas_call(
        functools.partial(_ffn_kernel, alpha=alpha),
        grid=(n // ROW_TILE,),
        in_specs=[pl.BlockSpec((ROW_TILE, d_model), lambda i: (i, 0)),
                  resident(wgu), resident(wd), resident(ln_g), resident(ln_b)],
        out_specs=pl.BlockSpec((ROW_TILE, d_model), lambda i: (i, 0)),
        out_shape=jax.ShapeDtypeStruct((n, d_model), F32),
        name="dense_swiglu_ln",
        compiler_params=_params("parallel"))(x_all, wgu, wd, ln_g, ln_b)


def _split_bf16(a):
    hi = a.astype(BF16)
    lo = (a - hi.astype(F32)).astype(BF16)
    return hi, lo


DEAD_RANK = -(1 << 20)


def _router_kernel(x_ref, w_ref, b_ref, idx_ref, gate_ref, cnt_ref, *, n_tokens):
    x_hi, x_lo = _split_bf16(x_ref[...])
    w_hi, w_lo = _split_bf16(w_ref[...])
    logits = _dot(x_hi, w_hi) + (_dot(x_lo, w_hi) + _dot(x_hi, w_lo)) + b_ref[...]
    lane = lax.broadcasted_iota(jnp.int32, logits.shape, 1)
    logits = jnp.where(lane < N_EXPERTS, logits, -jnp.inf)
    m1 = jnp.max(logits, axis=-1, keepdims=True)
    i1 = jnp.min(jnp.where(logits == m1, lane, LANES), axis=-1, keepdims=True)
    rest = jnp.where(lane == i1, -jnp.inf, logits)
    m2 = jnp.max(rest, axis=-1, keepdims=True)
    i2 = jnp.min(jnp.where(rest == m2, lane, LANES), axis=-1, keepdims=True)
    e2 = jnp.exp(m2 - m1)
    den = 1.0 + e2
    gate_ref[...] = jnp.where(lane == 0, 1.0 / den, e2 / den)

    row = lax.broadcasted_iota(jnp.int32, (ROW_TILE, 1), 0) + pl.program_id(0) * ROW_TILE
    live = row < n_tokens
    pick1 = jnp.where(jnp.logical_and(lane == i1, live), 1.0, 0.0)
    pick2 = jnp.where(jnp.logical_and(lane == i2, live), 1.0, 0.0)
    picks = pick1 + pick2
    r = lax.broadcasted_iota(jnp.int32, (ROW_TILE, ROW_TILE), 0)
    c = lax.broadcasted_iota(jnp.int32, (ROW_TILE, ROW_TILE), 1)
    earlier = _dot(jnp.where(c < r, 1.0, 0.0).astype(BF16), picks.astype(BF16))
    rank1 = jnp.sum(earlier * pick1, axis=-1, keepdims=True).astype(jnp.int32)
    rank2 = jnp.sum(earlier * pick2, axis=-1, keepdims=True).astype(jnp.int32)
    rank1 = jnp.where(live, rank1, DEAD_RANK)
    rank2 = jnp.where(live, rank2, DEAD_RANK)
    idx_ref[...] = jnp.where(lane == 0, i1, jnp.where(lane == 1, i2, jnp.where(lane == 2, rank1, rank2)))
    counts = jnp.sum(picks, axis=0, keepdims=True).astype(jnp.int32)
    cnt_ref[...] = jnp.broadcast_to(counts, cnt_ref.shape)


def _router(x_all, n_tokens, w_router, b_router):
    n, d_model = x_all.shape
    w = jnp.pad(w_router, ((0, 0), (0, LANES - N_EXPERTS)))
    b = jnp.pad(b_router, (0, LANES - N_EXPERTS))[None, :]
    n_tiles = n // ROW_TILE
    return pl.pallas_call(
        functools.partial(_router_kernel, n_tokens=n_tokens), grid=(n_tiles,),
        in_specs=[pl.BlockSpec((ROW_TILE, d_model), lambda i: (i, 0)),
                  pl.BlockSpec((d_model, LANES), lambda i: (0, 0)),
                  pl.BlockSpec((1, LANES), lambda i: (0, 0))],
        out_specs=(pl.BlockSpec((ROW_TILE, LANES), lambda i: (i, 0)),
                   pl.BlockSpec((ROW_TILE, LANES), lambda i: (i, 0)),
                   pl.BlockSpec((8, LANES), lambda i: (i, 0))),
        out_shape=(jax.ShapeDtypeStruct((n, LANES), jnp.int32), jax.ShapeDtypeStruct((n, LANES), F32),
                   jax.ShapeDtypeStruct((n_tiles * 8, LANES), jnp.int32)),
        name="moe_router",
        compiler_params=_params("parallel"))(x_all, w, b)


def _group_copy(src_ref, src_row, dst_ref, dst_row, sem):
    return pltpu.make_async_copy(src_ref.at[pl.ds(pl.multiple_of(src_row, GROUP), GROUP)],
                                 dst_ref.at[pl.ds(pl.multiple_of(dst_row, GROUP), GROUP)], sem)


def _wait_groups(count, src_ref, dst_ref, sem):
    def wait(_, carry):
        _group_copy(src_ref, 0, dst_ref, 0, sem).wait()
        return carry
    lax.fori_loop(0, count, wait, 0)


def _stage_rows(expert, rank, first_ref, step):
    pos = rank
    for e in range(N_EXPERTS):
        pos = pos + jnp.where(expert == e, first_ref[step * N_EXPERTS + e], 0)
    return pos


def _regroup_kernel(w0_ref, off_ref, ngrp_ref, part_ref, pend_ref, first_ref, x_ref, idx_ref, xs_hbm,
                    stage_ref, carry_ref, zero_ref, sem):
    step = pl.program_id(0)

    @pl.when(step == 0)
    def _():
        carry_ref[...] = jnp.zeros(carry_ref.shape, BF16)
        zero_ref[...] = jnp.zeros(zero_ref.shape, BF16)

    xb = x_ref[...].astype(BF16)
    stage_row = lax.broadcasted_iota(jnp.int32, (STAGE_ROWS, ROW_TILE), 0)
    idx = idx_ref[...]
    hit = None
    for choice in range(TOP_K):
        pos = _stage_rows(idx[choice:choice + 1, :], idx[TOP_K + choice:TOP_K + choice + 1, :], first_ref, step)
        hit = stage_row == pos if hit is None else jnp.logical_or(hit, stage_row == pos)
    stage_ref[...] = _dot(jnp.where(hit, 1.0, 0.0).astype(BF16), xb).astype(BF16)

    started = 0
    for e in range(N_EXPERTS):
        k = step * N_EXPERTS + e
        off, w0, ngrp = off_ref[k], w0_ref[k], ngrp_ref[k]
        head = pl.ds(pl.multiple_of(off, GROUP), GROUP)
        stage_ref[head, :] = stage_ref[head, :] + carry_ref[e]

        def send(q, carry, off=off, w0=w0):
            _group_copy(stage_ref, off + q * GROUP, xs_hbm, w0 + q * GROUP, sem).start()
            return carry

        lax.fori_loop(0, ngrp, send, 0)
        tail = stage_ref[pl.ds(pl.multiple_of(off + ngrp * GROUP, GROUP), GROUP), :]
        carry_ref[e] = jnp.where(part_ref[k] != 0, tail, jnp.zeros_like(tail))
        started = started + ngrp
    _wait_groups(started, stage_ref, xs_hbm, sem)

    @pl.when(step == pl.num_programs(0) - 1)
    def _():
        flushed = 0
        for e in range(N_EXPERTS):
            k = step * N_EXPERTS + e
            row = w0_ref[k] + ngrp_ref[k] * GROUP
            is_open = part_ref[k]

            @pl.when(is_open != 0)
            def _(row=row, e=e):
                pltpu.make_async_copy(carry_ref.at[e], xs_hbm.at[pl.ds(pl.multiple_of(row, GROUP), GROUP)],
                                      sem).start()

            row = row + is_open * GROUP
            n_zero = (pend_ref[e] - row) // GROUP

            def fill(q, carry, row=row):
                _group_copy(zero_ref, 0, xs_hbm, row + q * GROUP, sem).start()
                return carry

            lax.fori_loop(0, n_zero, fill, 0)
            flushed = flushed + is_open + n_zero
        _wait_groups(flushed, zero_ref, xs_hbm, sem)


def _regroup(x_all, idx_rows, tables, n_slots):
    n, d_model = x_all.shape
    return pl.pallas_call(
        _regroup_kernel,
        grid_spec=pltpu.PrefetchScalarGridSpec(
            num_scalar_prefetch=6, grid=(n // ROW_TILE,),
            in_specs=[pl.BlockSpec((ROW_TILE, d_model), lambda i, *_: (i, 0)),
                      pl.BlockSpec((2 * TOP_K, ROW_TILE), lambda i, *_: (0, i))],
            out_specs=pl.BlockSpec(memory_space=pl.ANY),
            scratch_shapes=[pltpu.VMEM((STAGE_ROWS, d_model), BF16),
                            pltpu.VMEM((N_EXPERTS, GROUP, d_model), BF16),
                            pltpu.VMEM((GROUP, d_model), BF16),
                            pltpu.SemaphoreType.DMA(())]),
        out_shape=jax.ShapeDtypeStruct((n_slots, d_model), BF16),
        name="moe_regroup",
        compiler_params=_params("arbitrary"))(
            tables["w0"], tables["off"], tables["ngrp"], tables["part"], tables["pend"], tables["first"],
            x_all, idx_rows)


def _moe_kernel(blk_e_ref, nvalid_ref, xs_ref, wg_ref, wu_ref, wd_ref, o_ref, wgub_ref, wdb_ref, acc_ref):
    s, c, j = pl.program_id(0), pl.program_id(1), pl.program_id(2)
    b = s * pl.num_programs(2) + j
    changed = jnp.logical_or(j == 0, blk_e_ref[b] != blk_e_ref[jnp.maximum(b - 1, 0)])

    @pl.when(jnp.logical_and(s == 0, jnp.logical_and(c == 0, j == 0)))
    def _():
        acc_ref[...] = jnp.zeros(acc_ref.shape, F32)

    @pl.when(changed)
    def _():
        for p in range(wd_ref.shape[0] // LANES):
            cols = slice(p * LANES, (p + 1) * LANES)
            wgub_ref[:, 2 * p * LANES:(2 * p + 1) * LANES] = wg_ref[:, cols].astype(BF16)
            wgub_ref[:, (2 * p + 1) * LANES:(2 * p + 2) * LANES] = wu_ref[:, cols].astype(BF16)
        wdb_ref[...] = wd_ref[...].astype(BF16)

    rows = pl.ds(pl.multiple_of(j * MOE_ROWS, MOE_ROWS), MOE_ROWS)
    valid = b < nvalid_ref[0]

    @pl.when(valid)
    def _():
        part = _swiglu(xs_ref[rows, :], wgub_ref, wdb_ref)
        total = jnp.where(c == 0, part, acc_ref[rows, :] + part)
        acc_ref[rows, :] = total
        o_ref[rows, :] = total.astype(BF16)

    @pl.when(jnp.logical_not(valid))
    def _():
        o_ref[rows, :] = jnp.zeros((MOE_ROWS, o_ref.shape[1]), BF16)


def _grouped_swiglu(xs, blk_e, nvalid, layer, wg, wu, wd):
    n_slots, d_model = xs.shape
    d_ff = wg.shape[3]
    fc = MOE_FF_CHUNK
    jb = MOE_SUPER // MOE_ROWS
    assert d_ff % fc == 0 and d_ff // fc >= 2 and n_slots % MOE_SUPER == 0
    w_in_map = lambda s, c, j, be, nv: (layer, be[s * jb + j], 0, c)
    w_out_map = lambda s, c, j, be, nv: (layer, be[s * jb + j], c, 0)
    return pl.pallas_call(
        _moe_kernel,
        grid_spec=pltpu.PrefetchScalarGridSpec(
            num_scalar_prefetch=2, grid=(n_slots // MOE_SUPER, d_ff // fc, jb),
            in_specs=[pl.BlockSpec((MOE_SUPER, d_model), lambda s, c, j, be, nv: (s, 0)),
                      pl.BlockSpec((None, None, d_model, fc), w_in_map),
                      pl.BlockSpec((None, None, d_model, fc), w_in_map),
                      pl.BlockSpec((None, None, fc, d_model), w_out_map)],
            out_specs=pl.BlockSpec((MOE_SUPER, d_model), lambda s, c, j, be, nv: (s, 0)),
            scratch_shapes=[pltpu.VMEM((d_model, 2 * fc), BF16), pltpu.VMEM((fc, d_model), BF16),
                            pltpu.VMEM((MOE_SUPER, d_model), F32)]),
        out_shape=jax.ShapeDtypeStruct((n_slots, d_model), BF16),
        name="moe_grouped_swiglu",
        compiler_params=_params("arbitrary", "arbitrary", "arbitrary"))(blk_e, nvalid, xs, wg, wu, wd)


def _combine_kernel(w0_ref, off_ref, nwin_ref, first_ref, x_ref, idx_ref, gate_ref, g_ref, b_ref, ys_hbm, o_ref,
                    stage_ref, sem, *, alpha):
    step = pl.program_id(0)

    def fetch_tile(tile, buf):
        stage_ref[buf] = jnp.zeros(stage_ref.shape[1:], BF16)
        for e in range(N_EXPERTS):
            k = tile * N_EXPERTS + e
            off, w0 = off_ref[k], w0_ref[k]

            def fetch(q, carry, off=off, w0=w0):
                _group_copy(ys_hbm, w0 + q * GROUP, stage_ref.at[buf], off + q * GROUP, sem.at[buf]).start()
                return carry

            lax.fori_loop(0, nwin_ref[k], fetch, 0)

    @pl.when(step == 0)
    def _():
        fetch_tile(step, 0)

    buf = lax.rem(step, 2)

    @pl.when(step + 1 < pl.num_programs(0))
    def _():
        fetch_tile(step + 1, 1 - buf)

    n_groups = 0
    for e in range(N_EXPERTS):
        n_groups = n_groups + nwin_ref[step * N_EXPERTS + e]
    _wait_groups(n_groups, ys_hbm, stage_ref.at[buf], sem.at[buf])

    stage_col = lax.broadcasted_iota(jnp.int32, (ROW_TILE, STAGE_ROWS), 1)
    idx = idx_ref[...]
    gates = gate_ref[...]
    ys = stage_ref[buf]
    f = None
    for choice in range(TOP_K):
        pos = _stage_rows(idx[:, choice:choice + 1], idx[:, TOP_K + choice:TOP_K + choice + 1], first_ref, step)
        pick = jnp.where(stage_col == pos, 1.0, 0.0).astype(BF16)
        term = gates[:, choice:choice + 1] * _dot(pick, ys)
        f = term if f is None else f + term
    o_ref[...] = _layer_norm(alpha * x_ref[...] + f, g_ref[...], b_ref[...])


def _combine(x_all, idx, gates, tables, ys, ln_g, ln_b, alpha):
    n, d_model = x_all.shape
    return pl.pallas_call(
        functools.partial(_combine_kernel, alpha=alpha),
        grid_spec=pltpu.PrefetchScalarGridSpec(
            num_scalar_prefetch=4, grid=(n // ROW_TILE,),
            in_specs=[pl.BlockSpec((ROW_TILE, d_model), lambda i, *_: (i, 0)),
                      pl.BlockSpec((ROW_TILE, LANES), lambda i, *_: (i, 0)),
                      pl.BlockSpec((ROW_TILE, LANES), lambda i, *_: (i, 0)),
                      pl.BlockSpec((1, d_model), lambda i, *_: (0, 0)),
                      pl.BlockSpec((1, d_model), lambda i, *_: (0, 0)),
                      pl.BlockSpec(memory_space=pl.ANY)],
            out_specs=pl.BlockSpec((ROW_TILE, d_model), lambda i, *_: (i, 0)),
            scratch_shapes=[pltpu.VMEM((2, STAGE_ROWS, d_model), BF16), pltpu.SemaphoreType.DMA((2,))]),
        out_shape=jax.ShapeDtypeStruct((n, d_model), F32),
        name="moe_combine_ln",
        compiler_params=_params("arbitrary"))(
            tables["w0"], tables["off"], tables["nwin"], tables["first"], x_all, idx, gates, ln_g, ln_b, ys)


def _routing_tables(counts, n_slots):
    count = jnp.sum(counts, axis=0)
    blocks = (count + MOE_ROWS - 1) // MOE_ROWS
    blk_end = jnp.cumsum(blocks)
    base = (blk_end - blocks) * MOE_ROWS
    start = base[None, :] + jnp.cumsum(counts, axis=0) - counts
    end = start + counts
    w0 = start // GROUP * GROUP
    ngrp = end // GROUP - start // GROUP
    part = (end % GROUP != 0).astype(jnp.int32)
    nwin = ngrp + part
    off = (jnp.cumsum(nwin, axis=1) - nwin) * GROUP
    nvalid = blk_end[-1]
    blk = jnp.minimum(jnp.arange(n_slots // MOE_ROWS, dtype=jnp.int32), nvalid - 1)
    blk_e = jnp.minimum(jnp.sum((blk[:, None] >= blk_end[None, :]).astype(jnp.int32), axis=1), N_EXPERTS - 1)
    pend = (blk_end * MOE_ROWS).at[-1].set(n_slots)
    flat = lambda a: a.reshape(-1).astype(jnp.int32)
    tables = dict(w0=flat(w0), off=flat(off), first=flat(start - w0 + off), ngrp=flat(ngrp), part=flat(part),
                  nwin=flat(nwin), pend=flat(pend))
    return tables, blk_e.astype(jnp.int32), nvalid.reshape(1).astype(jnp.int32)


def _moe_ffn(x_all, n_tokens, layer, w_router, b_router, wg, wu, wd, ln_g, ln_b, alpha):
    idx, gates, cnt = _router(x_all, n_tokens, w_router, b_router)
    n_slots = TOP_K * n_tokens + N_EXPERTS * MOE_ROWS
    n_slots = -(-n_slots // MOE_SUPER) * MOE_SUPER
    tables, blk_e, nvalid = _routing_tables(cnt[::8, :N_EXPERTS], n_slots)
    xs = _regroup(x_all, idx[:, :2 * TOP_K].T, tables, n_slots)
    ys = _grouped_swiglu(xs, blk_e, nvalid, layer, wg, wu, wd)
    return _combine(x_all, idx, gates, tables, ys, ln_g, ln_b, alpha)


def kernel(x_prompt, x_sample, state_pool, state_ssm_re, state_ssm_im, ln1_g, ln1_b, w_in, b_in, w_pool, pool_scale, lambda_re, lambda_im, log_dt, b_re, b_im, c_re, c_im, d_skip, w_glu, b_glu, w_proj_a, w_proj_b, w_out, ln2_g, ln2_b, w_ffn_gate, w_ffn_up, w_ffn_down, w_router, b_router, w_moe_gate, w_moe_up, w_moe_down):
    batch, seq, d_model = x_prompt.shape
    dec_batch = x_sample.shape[0]
    depth = w_in.shape[0]
    n_groups, n_state = lambda_re.shape[1], lambda_re.shape[2]
    chan = b_re.shape[-1]
    nst = n_groups * n_state
    pw = state_pool.shape[-1]
    alpha = (2.0 * depth) ** 0.25
    n_prompt = batch * seq
    n_tokens = n_prompt + dec_batch
    total_rows = -(-n_tokens // ROW_TILE) * ROW_TILE
    assert x_sample.shape[1] == 1 and n_prompt % ROW_TILE == 0 and seq % MIX_STEPS == 0
    assert batch * MIX_STEPS == ROW_TILE and dec_batch <= ROW_TILE

    y_all = jnp.concatenate([
        jnp.transpose(x_prompt, (1, 0, 2)).reshape(n_prompt, d_model),
        x_sample.reshape(dec_batch, d_model),
        jnp.zeros((total_rows - n_tokens, d_model), F32)], axis=0)
    zero_prev = jnp.zeros((POOL_STATE * batch, pw), F32)
    zero_h = jnp.zeros((batch, 2 * nst), F32)
    row = lambda a: a[None, :]
    outs = [[] for _ in range(6)]
    for l in range(depth):
        a_re, a_im, bb_re, bb_im = _ssm_params(lambda_re[l], lambda_im[l], log_dt[l], b_re[l], b_im[l])
        lam = jnp.concatenate([a_re[::chan].reshape(1, nst), a_im[::chan].reshape(1, nst)], axis=1)
        wts = (w_in[l].astype(BF16), row(b_in[l]), w_pool[l].astype(BF16), row(pool_scale[l]), lam,
               _block_diag_in(bb_re, n_groups, chan, n_state).astype(BF16),
               _block_diag_in(bb_im, n_groups, chan, n_state).astype(BF16),
               _block_diag_out(c_re[l], n_groups, chan, n_state).astype(BF16),
               _block_diag_out(c_im[l], n_groups, chan, n_state).astype(BF16),
               d_skip[l].reshape(1, -1), w_glu[l].astype(BF16), row(b_glu[l]),
               w_proj_a[l].astype(BF16), w_proj_b[l].astype(BF16), w_out[l].astype(BF16),
               row(ln1_g[l]), row(ln1_b[l]))
        prev_s = jnp.transpose(state_pool[l], (1, 0, 2)).reshape(POOL_STATE * dec_batch, pw)
        h0_s = jnp.concatenate([state_ssm_re[l].reshape(dec_batch, nst),
                                state_ssm_im[l].reshape(dec_batch, nst)], axis=1)
        y_mix, pool_p, h_p = _token_mix(y_all, 0, seq // MIX_STEPS, batch, MIX_STEPS, 0, alpha,
                                        zero_prev, zero_h, wts, total_rows)
        y_mix, pool_s, h_s = _token_mix(y_all, n_prompt, 1, dec_batch, 1, PAST_LEN, alpha,
                                        prev_s, h0_s, wts, total_rows, y_buf=y_mix)
        outs[0].append(jnp.transpose(pool_p.reshape(POOL_STATE, batch, pw), (1, 0, 2)))
        outs[1].append(h_p[:, :nst].reshape(batch, n_groups, n_state))
        outs[2].append(h_p[:, nst:].reshape(batch, n_groups, n_state))
        outs[3].append(jnp.transpose(pool_s.reshape(POOL_STATE, dec_batch, pw), (1, 0, 2)))
        outs[4].append(h_s[:, :nst].reshape(dec_batch, n_groups, n_state))
        outs[5].append(h_s[:, nst:].reshape(dec_batch, n_groups, n_state))
        j = l // 2
        if l % 2 == 0:
            y_all = _dense_ffn(y_mix, _interleave(w_ffn_gate[j], w_ffn_up[j]).astype(BF16),
                               w_ffn_down[j].astype(BF16), row(ln2_g[l]), row(ln2_b[l]), alpha)
        else:
            y_all = _moe_ffn(y_mix, n_tokens, j, w_router[j], b_router[j], w_moe_gate, w_moe_up,
                             w_moe_down, row(ln2_g[l]), row(ln2_b[l]), alpha)
    y_p = jnp.transpose(y_all[:n_prompt].reshape(seq, batch, d_model), (1, 0, 2))
    y_s = y_all[n_prompt:n_tokens].reshape(dec_batch, 1, d_model)
    return (y_p, y_s, *[jnp.stack(o) for o in outs])
```

```python
import functools
import math

import jax
import jax.numpy as jnp
from jax import lax
from jax.experimental import pallas as pl
from jax.experimental.pallas import tpu as pltpu

F32 = jnp.float32
BF16 = jnp.bfloat16

POOL_WINDOWS = (2, 4, 8, 16)
POOL_STATE = max(POOL_WINDOWS) - 1
POOL_GROUP = 128
SSM_GROUP = 16
SSM_STATE = 64
N_EXPERTS = 8
TOP_K = 2
LN_EPS = 1e-5
PAST_LEN = 16384

LANES = 128
MXU_COLS = 256
ROW_TILE = 512
MIX_STEPS = 64
SCAN_COLS = 512
MOE_ROWS = 512
MOE_SUPER = 2048
MOE_FF_CHUNK = 896
GROUP = 16
STAGE_ROWS = TOP_K * ROW_TILE + N_EXPERTS * 2 * GROUP
VMEM_LIMIT = 56 * 1024 * 1024


def _dot(a, b):
    return jnp.dot(a, b, preferred_element_type=F32)


def _layer_norm(r, g, b):
    mu = jnp.mean(r, axis=-1, keepdims=True)
    d = r - mu
    var = jnp.mean(d * d, axis=-1, keepdims=True)
    return d * lax.rsqrt(var + LN_EPS) * g + b


def _swiglu(xb, gate_up_piece, n_pieces, wd_ref):
    pieces = []
    for p in range(n_pieces):
        gate, up = gate_up_piece(p)
        pieces.append((jax.nn.silu(gate) * up).astype(BF16))
    return _dot(jnp.concatenate(pieces, axis=-1), wd_ref[...])


def _swiglu_split(xb, wg_ref, wu_ref, wd_ref):
    def piece(p):
        cols = slice(p * MXU_COLS, (p + 1) * MXU_COLS)
        return _dot(xb, wg_ref[:, cols]), _dot(xb, wu_ref[:, cols])
    return _swiglu(xb, piece, wd_ref.shape[0] // MXU_COLS, wd_ref)


def _swiglu_interleaved(xb, wgu_ref, wd_ref):
    def piece(p):
        gate_up = _dot(xb, wgu_ref[:, 2 * p * LANES:2 * (p + 1) * LANES])
        return gate_up[:, :LANES], gate_up[:, LANES:]
    return _swiglu(xb, piece, wd_ref.shape[0] // LANES, wd_ref)


def _params(*sem):
    return pltpu.CompilerParams(dimension_semantics=sem, vmem_limit_bytes=VMEM_LIMIT)


def _ssm_param_kernel(lre_ref, lim_ref, ldt_ref, bre_ref, bim_ref, are_ref, aim_ref, bbre_ref, bbim_ref):
    lre, lim = lre_ref[...], lim_ref[...]
    dt = jnp.exp(ldt_ref[...])
    mag = jnp.exp(lre * dt)
    a_re = mag * jnp.cos(lim * dt)
    a_im = mag * jnp.sin(lim * dt)
    den = lre * lre + lim * lim
    n_re, n_im = a_re - 1.0, a_im
    k_re = (n_re * lre + n_im * lim) / den
    k_im = (n_im * lre - n_re * lim) / den
    b_re, b_im = bre_ref[...], bim_ref[...]
    are_ref[...] = a_re
    aim_ref[...] = a_im
    bbre_ref[...] = k_re * b_re - k_im * b_im
    bbim_ref[...] = k_re * b_im + k_im * b_re


def _ssm_params(lambda_re, lambda_im, log_dt, b_re, b_im):
    depth, g, p = lambda_re.shape
    c = b_re.shape[-1]
    rep = lambda a: jnp.repeat(a.reshape(depth * g, p), c, axis=0)
    bt = lambda a: jnp.transpose(a, (0, 1, 3, 2)).reshape(depth * g * c, p)
    shape = jax.ShapeDtypeStruct((depth * g * c, p), F32)
    return pl.pallas_call(_ssm_param_kernel, out_shape=(shape,) * 4, name="s5_discretise")(
        rep(lambda_re), rep(lambda_im), rep(jnp.broadcast_to(log_dt[:, :, None], (depth, g, p))),
        bt(b_re), bt(b_im))


def _block_diag_in(bt, depth, g, c, p):
    m = g // 8
    a = bt.reshape(depth, m, 8, c, p)
    eye = jnp.eye(8, dtype=bt.dtype)
    return jnp.einsum('lmgcp,gh->lmgchp', a, eye).reshape(depth, m, 8 * c, 8 * p)


def _block_diag_out(cm, depth, g, c, p):
    m = g // 8
    a = jnp.transpose(cm, (0, 1, 3, 2)).reshape(depth, m, 8, p, c)
    eye = jnp.eye(8, dtype=cm.dtype)
    return jnp.einsum('lmgpc,gh->lmgphc', a, eye).reshape(depth, m, 8 * p, 8 * c)


def _mix_kernel(*refs, n_steps, zero_tail, **kw):
    if not zero_tail:
        _mix_body(*refs, **kw)
        return
    step = pl.program_id(0)
    y_ref = refs[-6]

    @pl.when(step < n_steps)
    def _():
        _mix_body(*refs, **kw)

    @pl.when(step >= n_steps)
    def _():
        y_ref[...] = jnp.zeros(y_ref.shape, F32)


def _mix_body(x_ref, prev_ref, h0_ref, win_ref, bin_ref, wpool_ref, pscale_ref, lam_ref,
              bre_ref, bim_ref, cre_ref, cim_ref, dskip_ref, wglu_ref, bglu_ref,
              wpa_ref, wpb_ref, wout_ref, lng_ref, lnb_ref, *rest,
              nb, tt, start_pos, alpha, aliased):
    if aliased:
        rest = rest[1:]
    y_ref, pool_ref, hlast_ref, ext_ref, bu_ref, hs_ref = rest
    rows = nb * tt
    halo = POOL_STATE * nb
    d_model = x_ref.shape[-1]
    pw = pscale_ref.shape[1]
    nst = lam_ref.shape[1] // 2
    step = pl.program_id(0)

    @pl.when(step == 0)
    def _():
        ext_ref[0:halo, :] = prev_ref[...]
        hs_ref[...] = h0_ref[...]

    if x_ref.ndim == 3:
        x = jnp.concatenate([x_ref[:, t, :] for t in range(tt)], axis=0)
    else:
        x = x_ref[...]
    xb = x.astype(BF16)

    u_a = _dot(xb, win_ref[:, 0:pw]) + bin_ref[:, 0:pw]
    ext_ref[halo:halo + rows, :] = u_a
    t_idx = lax.shift_right_logical(lax.broadcasted_iota(jnp.int32, (rows, 1), 0), int(math.log2(nb)))
    pos1 = t_idx + (step * tt + start_pos + 1)
    mixed = []
    for gi, w in enumerate(POOL_WINDOWS):
        sl = slice(gi * POOL_GROUP, (gi + 1) * POOL_GROUP)
        cur = u_a[:, sl]
        acc = cur
        for k in range(1, w):
            acc = acc + ext_ref[halo - k * nb:halo - k * nb + rows, sl]
        count = jnp.minimum(pos1, w).astype(F32)
        pooled = acc / count - cur
        mixed.append(_dot(pooled.astype(BF16), wpool_ref[gi]))
    out_a = jnp.concatenate(mixed, axis=-1) * pscale_ref[...]
    new_halo = ext_ref[rows:rows + halo, :]
    ext_ref[0:halo, :] = new_halo
    pool_ref[...] = new_halo
    proj_a = _dot(out_a.astype(BF16), wpa_ref[...])

    u_b = _dot(xb, win_ref[:, pw:2 * pw]) + bin_ref[:, pw:2 * pw]
    ub16 = u_b.astype(BF16)
    n_m = bre_ref.shape[0]
    kin, kst = bre_ref.shape[1], bre_ref.shape[2]
    for m in range(n_m):
        um = ub16[:, m * kin:(m + 1) * kin]
        bu_ref[:, m * kst:(m + 1) * kst] = _dot(um, bre_ref[m])
        bu_ref[:, nst + m * kst:nst + (m + 1) * kst] = _dot(um, bim_ref[m])
    if tt == 1:
        a_re, a_im = lam_ref[:, 0:nst], lam_ref[:, nst:]
        h_re, h_im = hs_ref[:, 0:nst], hs_ref[:, nst:]
        n_re = a_re * h_re - a_im * h_im + bu_ref[:, 0:nst]
        n_im = a_re * h_im + a_im * h_re + bu_ref[:, nst:]
        bu_ref[:, 0:nst] = n_re
        bu_ref[:, nst:] = n_im
        hs_ref[:, 0:nst] = n_re
        hs_ref[:, nst:] = n_im
    else:
        for q in range(nst // SCAN_COLS):
            c_re = slice(q * SCAN_COLS, (q + 1) * SCAN_COLS)
            c_im = slice(nst + q * SCAN_COLS, nst + (q + 1) * SCAN_COLS)
            a_re = jnp.broadcast_to(lam_ref[:, c_re], (nb, SCAN_COLS))
            a_im = jnp.broadcast_to(lam_ref[:, c_im], (nb, SCAN_COLS))

            def body(t, carry, c_re=c_re, c_im=c_im, a_re=a_re, a_im=a_im):
                h_re, h_im = carry
                r = pl.ds(pl.multiple_of(t * nb, nb), nb)
                n_re = a_re * h_re - a_im * h_im + bu_ref[r, c_re]
                n_im = a_re * h_im + a_im * h_re + bu_ref[r, c_im]
                bu_ref[r, c_re] = n_re
                bu_ref[r, c_im] = n_im
                return n_re, n_im

            h_re, h_im = lax.fori_loop(0, tt, body, (hs_ref[:, c_re], hs_ref[:, c_im]), unroll=True)
            hs_ref[:, c_re] = h_re
            hs_ref[:, c_im] = h_im
    hlast_ref[...] = hs_ref[...]
    ys = []
    for m in range(n_m):
        h_re = bu_ref[:, m * kst:(m + 1) * kst].astype(BF16)
        h_im = bu_ref[:, nst + m * kst:nst + (m + 1) * kst].astype(BF16)
        ys.append(_dot(h_re, cre_ref[m]) - _dot(h_im, cim_ref[m]))
    y = jnp.concatenate(ys, axis=-1) + dskip_ref[...] * u_b
    y = jax.nn.gelu(y)
    out_b = y * jax.nn.sigmoid(_dot(y.astype(BF16), wglu_ref[...]) + bglu_ref[...])
    proj_b = _dot(out_b.astype(BF16), wpb_ref[...])

    z_ga = _dot(xb, win_ref[:, 2 * pw:2 * pw + d_model]) + bin_ref[:, 2 * pw:2 * pw + d_model]
    merged = jax.nn.sigmoid(z_ga) * proj_a
    z_gb = _dot(xb, win_ref[:, 2 * pw + d_model:]) + bin_ref[:, 2 * pw + d_model:]
    merged = merged + jax.nn.sigmoid(z_gb) * proj_b
    mix = _dot(merged.astype(BF16), wout_ref[...])
    out = _layer_norm(alpha * x + mix, lng_ref[...], lnb_ref[...])
    if y_ref.shape[0] == rows:
        y_ref[...] = out
    else:
        y_ref[0:rows, :] = out
        y_ref[rows:, :] = jnp.zeros((y_ref.shape[0] - rows, d_model), F32)


def _token_mix(x_src, x_row0, row0, n_steps, nb, tt, start_pos, alpha, layer, prev, h0, wts, total_rows,
               y_buf=None, state_layer=None):
    state_layer = layer if state_layer is None else state_layer
    rows = nb * tt
    d_model = x_src.shape[-1]
    nst2 = h0.shape[-1]
    pw = prev.shape[-1]
    halo = POOL_STATE * nb
    aliased = y_buf is not None
    const = lambda a: _layer_spec(a, layer)
    tile = lambda i: jnp.minimum(i, n_steps - 1)
    if x_src.ndim == 3:
        x_spec = pl.BlockSpec((nb, tt, d_model), lambda i: (0, tile(i), 0))
    else:
        x_spec = pl.BlockSpec((rows, d_model), lambda i: (x_row0 // rows + tile(i), 0))
    in_specs = [x_spec, _layer_spec(prev, state_layer), _layer_spec(h0, state_layer)]
    in_specs += [const(w) for w in wts]
    args = [x_src, prev, h0, *wts]
    if aliased:
        in_specs.append(pl.BlockSpec(memory_space=pl.ANY))
        args.append(y_buf)
        y_spec = pl.BlockSpec((ROW_TILE, d_model), lambda i: (row0 // ROW_TILE, 0))
    else:
        y_spec = pl.BlockSpec((rows, d_model), lambda i: (row0 // rows + i, 0))
    out_shape = (jax.ShapeDtypeStruct((total_rows, d_model), F32),
                 jax.ShapeDtypeStruct((halo, pw), F32),
                 jax.ShapeDtypeStruct((nb, nst2), F32))
    out_specs = (y_spec,
                 pl.BlockSpec((halo, pw), lambda i: (0, 0)),
                 pl.BlockSpec((nb, nst2), lambda i: (0, 0)))
    zero_tail = not aliased
    assert aliased or row0 + (n_steps + 1) * rows == total_rows
    kern = functools.partial(_mix_kernel, n_steps=n_steps, zero_tail=zero_tail,
                             nb=nb, tt=tt, start_pos=start_pos, alpha=alpha, aliased=aliased)
    return pl.pallas_call(
        kern, grid=(n_steps + int(zero_tail),), in_specs=in_specs, out_specs=out_specs, out_shape=out_shape,
        scratch_shapes=[pltpu.VMEM((halo + rows, pw), F32), pltpu.VMEM((rows, nst2), F32),
                        pltpu.VMEM((nb, nst2), F32)],
        input_output_aliases={len(args) - 1: 0} if aliased else {},
        name="token_mix_decode" if aliased else "token_mix_prompt",
        compiler_params=_params("arbitrary"))(*args)


def _ffn_kernel(x_ref, wg_ref, wu_ref, wd_ref, g_ref, b_ref, o_ref, *, alpha):
    x = x_ref[...]
    ffn = _swiglu_split(x.astype(BF16), wg_ref, wu_ref, wd_ref)
    o_ref[...] = _layer_norm(alpha * x + ffn, g_ref[...], b_ref[...])


def _layer_spec(a, layer, **kw):
    return pl.BlockSpec((None,) + a.shape[1:], lambda *_, n=a.ndim - 1: (layer,) + (0,) * n, **kw)


def _dense_ffn(x_all, layer, wg, wu, wd, ln_g, ln_b, alpha):
    n, d_model = x_all.shape
    resident = lambda a: _layer_spec(a, layer, pipeline_mode=pl.Buffered(1))
    return pl.pallas_call(
        functools.partial(_ffn_kernel, alpha=alpha),
        grid=(n // ROW_TILE,),
        in_specs=[pl.BlockSpec((ROW_TILE, d_model), lambda i: (i, 0)),
                  resident(wg), resident(wu), resident(wd),
                  pl.BlockSpec((1, d_model), lambda i: (0, 0)), pl.BlockSpec((1, d_model), lambda i: (0, 0))],
        out_specs=pl.BlockSpec((ROW_TILE, d_model), lambda i: (i, 0)),
        out_shape=jax.ShapeDtypeStruct((n, d_model), F32),
        name="dense_swiglu_ln",
        compiler_params=_params("parallel"))(x_all, wg, wu, wd, ln_g, ln_b)


def _split_bf16(a):
    hi = a.astype(BF16)
    lo = (a - hi.astype(F32)).astype(BF16)
    return hi, lo


DEAD_RANK = -(1 << 20)


def _router_kernel(x_ref, w_ref, b_ref, idx_ref, gate_ref, cnt_ref, *, n_tokens):
    x_hi, x_lo = _split_bf16(x_ref[...])
    w_hi, w_lo = _split_bf16(w_ref[...])
    logits = _dot(x_hi, w_hi) + (_dot(x_lo, w_hi) + _dot(x_hi, w_lo)) + b_ref[...]
    lane = lax.broadcasted_iota(jnp.int32, logits.shape, 1)
    logits = jnp.where(lane < N_EXPERTS, logits, -jnp.inf)
    m1 = jnp.max(logits, axis=-1, keepdims=True)
    i1 = jnp.min(jnp.where(logits == m1, lane, LANES), axis=-1, keepdims=True)
    rest = jnp.where(lane == i1, -jnp.inf, logits)
    m2 = jnp.max(rest, axis=-1, keepdims=True)
    i2 = jnp.min(jnp.where(rest == m2, lane, LANES), axis=-1, keepdims=True)
    e2 = jnp.exp(m2 - m1)
    den = 1.0 + e2
    gate_ref[...] = jnp.where(lane == 0, 1.0 / den, e2 / den)

    row = lax.broadcasted_iota(jnp.int32, (ROW_TILE, 1), 0) + pl.program_id(0) * ROW_TILE
    live = row < n_tokens
    pick1 = jnp.where(jnp.logical_and(lane == i1, live), 1.0, 0.0)
    pick2 = jnp.where(jnp.logical_and(lane == i2, live), 1.0, 0.0)
    picks = pick1 + pick2
    r = lax.broadcasted_iota(jnp.int32, (ROW_TILE, ROW_TILE), 0)
    c = lax.broadcasted_iota(jnp.int32, (ROW_TILE, ROW_TILE), 1)
    earlier = _dot(jnp.where(c < r, 1.0, 0.0).astype(BF16), picks.astype(BF16))
    rank1 = jnp.sum(earlier * pick1, axis=-1, keepdims=True).astype(jnp.int32)
    rank2 = jnp.sum(earlier * pick2, axis=-1, keepdims=True).astype(jnp.int32)
    rank1 = jnp.where(live, rank1, DEAD_RANK)
    rank2 = jnp.where(live, rank2, DEAD_RANK)
    idx_ref[...] = jnp.where(lane == 0, i1, jnp.where(lane == 1, i2, jnp.where(lane == 2, rank1, rank2)))
    counts = jnp.sum(picks, axis=0, keepdims=True).astype(jnp.int32)
    cnt_ref[...] = jnp.broadcast_to(counts, cnt_ref.shape)


def _router(x_all, n_tokens, w_router, b_router):
    n, d_model = x_all.shape
    w = jnp.pad(w_router, ((0, 0), (0, LANES - N_EXPERTS)))
    b = jnp.pad(b_router, (0, LANES - N_EXPERTS))[None, :]
    n_tiles = n // ROW_TILE
    return pl.pallas_call(
        functools.partial(_router_kernel, n_tokens=n_tokens), grid=(n_tiles,),
        in_specs=[pl.BlockSpec((ROW_TILE, d_model), lambda i: (i, 0)),
                  pl.BlockSpec((d_model, LANES), lambda i: (0, 0)),
                  pl.BlockSpec((1, LANES), lambda i: (0, 0))],
        out_specs=(pl.BlockSpec((ROW_TILE, LANES), lambda i: (i, 0)),
                   pl.BlockSpec((ROW_TILE, LANES), lambda i: (i, 0)),
                   pl.BlockSpec((8, LANES), lambda i: (i, 0))),
        out_shape=(jax.ShapeDtypeStruct((n, LANES), jnp.int32), jax.ShapeDtypeStruct((n, LANES), F32),
                   jax.ShapeDtypeStruct((n_tiles * 8, LANES), jnp.int32)),
        name="moe_router",
        compiler_params=_params("parallel"))(x_all, w, b)


def _group_copy(src_ref, src_row, dst_ref, dst_row, sem):
    return pltpu.make_async_copy(src_ref.at[pl.ds(pl.multiple_of(src_row, GROUP), GROUP)],
                                 dst_ref.at[pl.ds(pl.multiple_of(dst_row, GROUP), GROUP)], sem)


def _wait_groups(count, src_ref, dst_ref, sem):
    def wait(_, carry):
        _group_copy(src_ref, 0, dst_ref, 0, sem).wait()
        return carry
    lax.fori_loop(0, count, wait, 0)


def _stage_rows(expert, rank, first_ref, step):
    pos = rank
    for e in range(N_EXPERTS):
        pos = pos + jnp.where(expert == e, first_ref[step * N_EXPERTS + e], 0)
    return pos


def _regroup_kernel(w0_ref, off_ref, ngrp_ref, part_ref, pend_ref, first_ref, x_ref, idx_ref, xs_hbm,
                    stage_ref, carry_ref, zero_ref, sem):
    step = pl.program_id(0)

    @pl.when(step == 0)
    def _():
        carry_ref[...] = jnp.zeros(carry_ref.shape, BF16)
        zero_ref[...] = jnp.zeros(zero_ref.shape, BF16)

    xb = x_ref[...].astype(BF16)
    stage_row = lax.broadcasted_iota(jnp.int32, (STAGE_ROWS, ROW_TILE), 0)
    idx = idx_ref[...]
    hit = None
    for choice in range(TOP_K):
        pos = _stage_rows(idx[choice:choice + 1, :], idx[TOP_K + choice:TOP_K + choice + 1, :], first_ref, step)
        hit = stage_row == pos if hit is None else jnp.logical_or(hit, stage_row == pos)
    stage_ref[...] = _dot(jnp.where(hit, 1.0, 0.0).astype(BF16), xb).astype(BF16)

    started = 0
    for e in range(N_EXPERTS):
        k = step * N_EXPERTS + e
        off, w0, ngrp = off_ref[k], w0_ref[k], ngrp_ref[k]
        head = pl.ds(pl.multiple_of(off, GROUP), GROUP)
        stage_ref[head, :] = stage_ref[head, :] + carry_ref[e]

        def send(q, carry, off=off, w0=w0):
            _group_copy(stage_ref, off + q * GROUP, xs_hbm, w0 + q * GROUP, sem).start()
            return carry

        lax.fori_loop(0, ngrp, send, 0)
        tail = stage_ref[pl.ds(pl.multiple_of(off + ngrp * GROUP, GROUP), GROUP), :]
        carry_ref[e] = jnp.where(part_ref[k] != 0, tail, jnp.zeros_like(tail))
        started = started + ngrp
    _wait_groups(started, stage_ref, xs_hbm, sem)

    @pl.when(step == pl.num_programs(0) - 1)
    def _():
        flushed = 0
        for e in range(N_EXPERTS):
            k = step * N_EXPERTS + e
            row = w0_ref[k] + ngrp_ref[k] * GROUP
            is_open = part_ref[k]

            @pl.when(is_open != 0)
            def _(row=row, e=e):
                pltpu.make_async_copy(carry_ref.at[e], xs_hbm.at[pl.ds(pl.multiple_of(row, GROUP), GROUP)],
                                      sem).start()

            row = row + is_open * GROUP
            n_zero = (pend_ref[e] - row) // GROUP

            def fill(q, carry, row=row):
                _group_copy(zero_ref, 0, xs_hbm, row + q * GROUP, sem).start()
                return carry

            lax.fori_loop(0, n_zero, fill, 0)
            flushed = flushed + is_open + n_zero
        _wait_groups(flushed, zero_ref, xs_hbm, sem)


def _regroup(x_all, idx_rows, tables, n_slots):
    n, d_model = x_all.shape
    return pl.pallas_call(
        _regroup_kernel,
        grid_spec=pltpu.PrefetchScalarGridSpec(
            num_scalar_prefetch=6, grid=(n // ROW_TILE,),
            in_specs=[pl.BlockSpec((ROW_TILE, d_model), lambda i, *_: (i, 0)),
                      pl.BlockSpec((2 * TOP_K, ROW_TILE), lambda i, *_: (0, i))],
            out_specs=pl.BlockSpec(memory_space=pl.ANY),
            scratch_shapes=[pltpu.VMEM((STAGE_ROWS, d_model), BF16),
                            pltpu.VMEM((N_EXPERTS, GROUP, d_model), BF16),
                            pltpu.VMEM((GROUP, d_model), BF16),
                            pltpu.SemaphoreType.DMA(())]),
        out_shape=jax.ShapeDtypeStruct((n_slots, d_model), BF16),
        name="moe_regroup",
        compiler_params=_params("arbitrary"))(
            tables["w0"], tables["off"], tables["ngrp"], tables["part"], tables["pend"], tables["first"],
            x_all, idx_rows)


def _moe_kernel(blk_e_ref, nvalid_ref, xs_ref, wg_ref, wu_ref, wd_ref, o_ref, wgub_ref, wdb_ref, acc_ref):
    s, c, j = pl.program_id(0), pl.program_id(1), pl.program_id(2)
    b = s * pl.num_programs(2) + j
    changed = jnp.logical_or(j == 0, blk_e_ref[b] != blk_e_ref[jnp.maximum(b - 1, 0)])

    @pl.when(jnp.logical_and(s == 0, jnp.logical_and(c == 0, j == 0)))
    def _():
        acc_ref[...] = jnp.zeros(acc_ref.shape, F32)

    @pl.when(changed)
    def _():
        for p in range(wd_ref.shape[0] // LANES):
            cols = slice(p * LANES, (p + 1) * LANES)
            wgub_ref[:, 2 * p * LANES:(2 * p + 1) * LANES] = wg_ref[:, cols].astype(BF16)
            wgub_ref[:, (2 * p + 1) * LANES:(2 * p + 2) * LANES] = wu_ref[:, cols].astype(BF16)
        wdb_ref[...] = wd_ref[...].astype(BF16)

    rows = pl.ds(pl.multiple_of(j * MOE_ROWS, MOE_ROWS), MOE_ROWS)
    valid = b < nvalid_ref[0]

    @pl.when(valid)
    def _():
        part = _swiglu_interleaved(xs_ref[rows, :], wgub_ref, wdb_ref)
        total = jnp.where(c == 0, part, acc_ref[rows, :] + part)
        acc_ref[rows, :] = total
        o_ref[rows, :] = total.astype(BF16)

    @pl.when(jnp.logical_not(valid))
    def _():
        o_ref[rows, :] = jnp.zeros((MOE_ROWS, o_ref.shape[1]), BF16)


def _grouped_swiglu(xs, blk_e, nvalid, layer, wg, wu, wd):
    n_slots, d_model = xs.shape
    d_ff = wg.shape[3]
    fc = MOE_FF_CHUNK
    jb = MOE_SUPER // MOE_ROWS
    assert d_ff % fc == 0 and d_ff // fc >= 2 and n_slots % MOE_SUPER == 0
    w_in_map = lambda s, c, j, be, nv: (layer, be[s * jb + j], 0, c)
    w_out_map = lambda s, c, j, be, nv: (layer, be[s * jb + j], c, 0)
    return pl.pallas_call(
        _moe_kernel,
        grid_spec=pltpu.PrefetchScalarGridSpec(
            num_scalar_prefetch=2, grid=(n_slots // MOE_SUPER, d_ff // fc, jb),
            in_specs=[pl.BlockSpec((MOE_SUPER, d_model), lambda s, c, j, be, nv: (s, 0)),
                      pl.BlockSpec((None, None, d_model, fc), w_in_map),
                      pl.BlockSpec((None, None, d_model, fc), w_in_map),
                      pl.BlockSpec((None, None, fc, d_model), w_out_map)],
            out_specs=pl.BlockSpec((MOE_SUPER, d_model), lambda s, c, j, be, nv: (s, 0)),
            scratch_shapes=[pltpu.VMEM((d_model, 2 * fc), BF16), pltpu.VMEM((fc, d_model), BF16),
                            pltpu.VMEM((MOE_SUPER, d_model), F32)]),
        out_shape=jax.ShapeDtypeStruct((n_slots, d_model), BF16),
        name="moe_grouped_swiglu",
        compiler_params=_params("arbitrary", "arbitrary", "arbitrary"))(blk_e, nvalid, xs, wg, wu, wd)


def _combine_kernel(w0_ref, off_ref, nwin_ref, first_ref, x_ref, idx_ref, gate_ref, g_ref, b_ref, ys_hbm,
                    *rest, alpha, batch_major):
    if batch_major:
        prompt_ref, tail_ref, stage_ref, sem = rest
    else:
        o_ref, stage_ref, sem = rest
    step = pl.program_id(0)

    def fetch_tile(tile, buf):
        stage_ref[buf] = jnp.zeros(stage_ref.shape[1:], BF16)
        for e in range(N_EXPERTS):
            k = tile * N_EXPERTS + e
            off, w0 = off_ref[k], w0_ref[k]

            def fetch(q, carry, off=off, w0=w0):
                _group_copy(ys_hbm, w0 + q * GROUP, stage_ref.at[buf], off + q * GROUP, sem.at[buf]).start()
                return carry

            lax.fori_loop(0, nwin_ref[k], fetch, 0)

    @pl.when(step == 0)
    def _():
        fetch_tile(step, 0)

    buf = lax.rem(step, 2)

    @pl.when(step + 1 < pl.num_programs(0))
    def _():
        fetch_tile(step + 1, 1 - buf)

    n_groups = 0
    for e in range(N_EXPERTS):
        n_groups = n_groups + nwin_ref[step * N_EXPERTS + e]
    _wait_groups(n_groups, ys_hbm, stage_ref.at[buf], sem.at[buf])

    stage_col = lax.broadcasted_iota(jnp.int32, (ROW_TILE, STAGE_ROWS), 1)
    idx = idx_ref[...]
    gates = gate_ref[...]
    ys = stage_ref[buf]
    f = None
    for choice in range(TOP_K):
        pos = _stage_rows(idx[:, choice:choice + 1], idx[:, TOP_K + choice:TOP_K + choice + 1], first_ref, step)
        pick = jnp.where(stage_col == pos, 1.0, 0.0).astype(BF16)
        term = gates[:, choice:choice + 1] * _dot(pick, ys)
        f = term if f is None else f + term
    out = _layer_norm(alpha * x_ref[...] + f, g_ref[...], b_ref[...])
    if not batch_major:
        o_ref[...] = out
        return
    nb, tt = prompt_ref.shape[0], prompt_ref.shape[1]
    last = pl.num_programs(0) - 1

    @pl.when(step < last)
    def _():
        for t in range(tt):
            prompt_ref[:, t, :] = out[t * nb:(t + 1) * nb, :]

    @pl.when(step == last)
    def _():
        tail_ref[...] = out


def _combine(x_all, idx, gates, tables, ys, ln_g, ln_b, alpha, prompt_shape=None):
    n, d_model = x_all.shape
    n_tiles = n // ROW_TILE
    if prompt_shape is None:
        out_specs = pl.BlockSpec((ROW_TILE, d_model), lambda i, *_: (i, 0))
        out_shape = jax.ShapeDtypeStruct((n, d_model), F32)
    else:
        batch, seq = prompt_shape
        tt = ROW_TILE // batch
        assert batch * seq == (n_tiles - 1) * ROW_TILE
        out_specs = (pl.BlockSpec((batch, tt, d_model), lambda i, *_: (0, jnp.minimum(i, n_tiles - 2), 0)),
                     pl.BlockSpec((ROW_TILE, d_model), lambda i, *_: (0, 0)))
        out_shape = (jax.ShapeDtypeStruct((batch, seq, d_model), F32),
                     jax.ShapeDtypeStruct((ROW_TILE, d_model), F32))
    return pl.pallas_call(
        functools.partial(_combine_kernel, alpha=alpha, batch_major=prompt_shape is not None),
        grid_spec=pltpu.PrefetchScalarGridSpec(
            num_scalar_prefetch=4, grid=(n_tiles,),
            in_specs=[pl.BlockSpec((ROW_TILE, d_model), lambda i, *_: (i, 0)),
                      pl.BlockSpec((ROW_TILE, LANES), lambda i, *_: (i, 0)),
                      pl.BlockSpec((ROW_TILE, LANES), lambda i, *_: (i, 0)),
                      pl.BlockSpec((1, d_model), lambda i, *_: (0, 0)),
                      pl.BlockSpec((1, d_model), lambda i, *_: (0, 0)),
                      pl.BlockSpec(memory_space=pl.ANY)],
            out_specs=out_specs,
            scratch_shapes=[pltpu.VMEM((2, STAGE_ROWS, d_model), BF16), pltpu.SemaphoreType.DMA((2,))]),
        out_shape=out_shape,
        name="moe_combine_ln",
        compiler_params=_params("arbitrary"))(
            tables["w0"], tables["off"], tables["nwin"], tables["first"], x_all, idx, gates, ln_g, ln_b, ys)


def _routing_tables(counts, n_slots):
    count = jnp.sum(counts, axis=0)
    blocks = (count + MOE_ROWS - 1) // MOE_ROWS
    blk_end = jnp.cumsum(blocks)
    base = (blk_end - blocks) * MOE_ROWS
    start = base[None, :] + jnp.cumsum(counts, axis=0) - counts
    end = start + counts
    w0 = start // GROUP * GROUP
    ngrp = end // GROUP - start // GROUP
    part = (end % GROUP != 0).astype(jnp.int32)
    nwin = ngrp + part
    off = (jnp.cumsum(nwin, axis=1) - nwin) * GROUP
    nvalid = blk_end[-1]
    blk = jnp.minimum(jnp.arange(n_slots // MOE_ROWS, dtype=jnp.int32), nvalid - 1)
    blk_e = jnp.minimum(jnp.sum((blk[:, None] >= blk_end[None, :]).astype(jnp.int32), axis=1), N_EXPERTS - 1)
    pend = (blk_end * MOE_ROWS).at[-1].set(n_slots)
    flat = lambda a: a.reshape(-1).astype(jnp.int32)
    tables = dict(w0=flat(w0), off=flat(off), first=flat(start - w0 + off), ngrp=flat(ngrp), part=flat(part),
                  nwin=flat(nwin), pend=flat(pend))
    return tables, blk_e.astype(jnp.int32), nvalid.reshape(1).astype(jnp.int32)


def _moe_ffn(x_all, n_tokens, layer, w_router, b_router, wg, wu, wd, ln_g, ln_b, alpha, prompt_shape=None):
    idx, gates, cnt = _router(x_all, n_tokens, w_router, b_router)
    n_slots = TOP_K * n_tokens + N_EXPERTS * MOE_ROWS
    n_slots = -(-n_slots // MOE_SUPER) * MOE_SUPER
    tables, blk_e, nvalid = _routing_tables(cnt[::8, :N_EXPERTS], n_slots)
    xs = _regroup(x_all, idx[:, :2 * TOP_K].T, tables, n_slots)
    ys = _grouped_swiglu(xs, blk_e, nvalid, layer, wg, wu, wd)
    return _combine(x_all, idx, gates, tables, ys, ln_g, ln_b, alpha, prompt_shape)


def kernel(x_prompt, x_sample, state_pool, state_ssm_re, state_ssm_im, ln1_g, ln1_b, w_in, b_in, w_pool, pool_scale, lambda_re, lambda_im, log_dt, b_re, b_im, c_re, c_im, d_skip, w_glu, b_glu, w_proj_a, w_proj_b, w_out, ln2_g, ln2_b, w_ffn_gate, w_ffn_up, w_ffn_down, w_router, b_router, w_moe_gate, w_moe_up, w_moe_down):
    batch, seq, d_model = x_prompt.shape
    dec_batch = x_sample.shape[0]
    depth = w_in.shape[0]
    n_groups, n_state = lambda_re.shape[1], lambda_re.shape[2]
    chan = b_re.shape[-1]
    nst = n_groups * n_state
    pw = state_pool.shape[-1]
    alpha = (2.0 * depth) ** 0.25
    n_prompt = batch * seq
    n_tokens = n_prompt + dec_batch
    total_rows = -(-n_tokens // ROW_TILE) * ROW_TILE
    assert x_sample.shape[1] == 1 and n_prompt % ROW_TILE == 0 and seq % MIX_STEPS == 0
    assert batch * MIX_STEPS == ROW_TILE and dec_batch <= ROW_TILE

    rows_of = lambda a: a[:, None, :]
    a_re, a_im, bb_re, bb_im = _ssm_params(lambda_re, lambda_im, log_dt, b_re, b_im)
    lam = jnp.concatenate([a_re[::chan].reshape(depth, 1, nst), a_im[::chan].reshape(depth, 1, nst)], axis=2)
    wts = (w_in.astype(BF16), rows_of(b_in), w_pool.astype(BF16), rows_of(pool_scale), lam,
           _block_diag_in(bb_re, depth, n_groups, chan, n_state).astype(BF16),
           _block_diag_in(bb_im, depth, n_groups, chan, n_state).astype(BF16),
           _block_diag_out(c_re, depth, n_groups, chan, n_state).astype(BF16),
           _block_diag_out(c_im, depth, n_groups, chan, n_state).astype(BF16),
           d_skip.reshape(depth, 1, -1), w_glu.astype(BF16), rows_of(b_glu),
           w_proj_a.astype(BF16), w_proj_b.astype(BF16), w_out.astype(BF16),
           rows_of(ln1_g), rows_of(ln1_b))
    ffn_wts = (w_ffn_gate.astype(BF16), w_ffn_up.astype(BF16), w_ffn_down.astype(BF16))
    prev_s = jnp.transpose(state_pool, (0, 2, 1, 3)).reshape(depth, POOL_STATE * dec_batch, pw)
    h0_s = jnp.concatenate([state_ssm_re.reshape(depth, dec_batch, nst),
                            state_ssm_im.reshape(depth, dec_batch, nst)], axis=2)
    zero_prev = jnp.zeros((1, POOL_STATE * batch, pw), F32)
    zero_h = jnp.zeros((1, batch, 2 * nst), F32)

    x_p = x_prompt
    x_s = x_sample.reshape(dec_batch, d_model)
    s_row0 = 0
    outs = [[] for _ in range(6)]
    for l in range(depth):
        y_mix, pool_p, h_p = _token_mix(x_p, 0, 0, seq // MIX_STEPS, batch, MIX_STEPS, 0, alpha, l,
                                        zero_prev, zero_h, wts, total_rows, state_layer=0)
        y_mix, pool_s, h_s = _token_mix(x_s, s_row0, n_prompt, 1, dec_batch, 1, PAST_LEN, alpha, l,
                                        prev_s, h0_s, wts, total_rows, y_buf=y_mix)
        outs[0].append(jnp.transpose(pool_p.reshape(POOL_STATE, batch, pw), (1, 0, 2)))
        outs[1].append(h_p[:, :nst].reshape(batch, n_groups, n_state))
        outs[2].append(h_p[:, nst:].reshape(batch, n_groups, n_state))
        outs[3].append(jnp.transpose(pool_s.reshape(POOL_STATE, dec_batch, pw), (1, 0, 2)))
        outs[4].append(h_s[:, :nst].reshape(dec_batch, n_groups, n_state))
        outs[5].append(h_s[:, nst:].reshape(dec_batch, n_groups, n_state))
        j = l // 2
        if l % 2 == 0:
            y_all = _dense_ffn(y_mix, j, *ffn_wts, ln2_g[l][None, :], ln2_b[l][None, :], alpha)
        else:
            prompt_shape = (batch, seq) if l == depth - 1 else None
            y_all = _moe_ffn(y_mix, n_tokens, j, w_router[j], b_router[j], w_moe_gate, w_moe_up,
                             w_moe_down, ln2_g[l][None, :], ln2_b[l][None, :], alpha, prompt_shape)
        x_p = x_s = y_all
        s_row0 = n_prompt
    if isinstance(y_all, (tuple, list)):
        y_p, tail = y_all
        y_s = tail[:dec_batch].reshape(dec_batch, 1, d_model)
    else:
        y_p = jnp.transpose(y_all[:n_prompt].reshape(seq, batch, d_model), (1, 0, 2))
        y_s = y_all[n_prompt:n_tokens].reshape(dec_batch, 1, d_model)
    return (y_p, y_s, *[jnp.stack(o) for o in outs])
```

```python
import functools
import math

import jax
import jax.numpy as jnp
from jax import lax
from jax.experimental import pallas as pl
from jax.experimental.pallas import tpu as pltpu

F32 = jnp.float32
BF16 = jnp.bfloat16

POOL_WINDOWS = (2, 4, 8, 16)
POOL_STATE = max(POOL_WINDOWS) - 1
POOL_GROUP = 128
SSM_GROUP = 16
SSM_STATE = 64
N_EXPERTS = 8
TOP_K = 2
LN_EPS = 1e-5
PAST_LEN = 16384

LANES = 128
MXU_COLS = 256
ROW_TILE = 512
MIX_STEPS = 64
SCAN_COLS = 512
MOE_ROWS = 512
ITEM_BLOCKS = 4
MOE_FF_CHUNK = 896
GROUP = 16
STAGE_ROWS = TOP_K * ROW_TILE + N_EXPERTS * 2 * GROUP
VMEM_LIMIT = 56 * 1024 * 1024


def _dot(a, b):
    return jnp.dot(a, b, preferred_element_type=F32)


def _layer_norm(r, g, b):
    mu = jnp.mean(r, axis=-1, keepdims=True)
    d = r - mu
    var = jnp.mean(d * d, axis=-1, keepdims=True)
    return d * lax.rsqrt(var + LN_EPS) * g + b


def _swiglu(xb, gate_up_piece, n_pieces, wd_ref):
    pieces = []
    for p in range(n_pieces):
        gate, up = gate_up_piece(p)
        pieces.append((jax.nn.silu(gate) * up).astype(BF16))
    return _dot(jnp.concatenate(pieces, axis=-1), wd_ref[...])


def _swiglu_split(xb, wg_ref, wu_ref, wd_ref):
    def piece(p):
        cols = slice(p * MXU_COLS, (p + 1) * MXU_COLS)
        return _dot(xb, wg_ref[:, cols]), _dot(xb, wu_ref[:, cols])
    return _swiglu(xb, piece, wd_ref.shape[0] // MXU_COLS, wd_ref)


def _swiglu_interleaved(xb, wgu_ref, wd_ref):
    def piece(p):
        gate_up = _dot(xb, wgu_ref[:, 2 * p * LANES:2 * (p + 1) * LANES])
        return gate_up[:, :LANES], gate_up[:, LANES:]
    return _swiglu(xb, piece, wd_ref.shape[0] // LANES, wd_ref)


def _params(*sem):
    return pltpu.CompilerParams(dimension_semantics=sem, vmem_limit_bytes=VMEM_LIMIT)


def _ssm_param_kernel(lre_ref, lim_ref, ldt_ref, bre_ref, bim_ref, are_ref, aim_ref, bbre_ref, bbim_ref):
    lre, lim = lre_ref[...], lim_ref[...]
    dt = jnp.exp(ldt_ref[...])
    mag = jnp.exp(lre * dt)
    a_re = mag * jnp.cos(lim * dt)
    a_im = mag * jnp.sin(lim * dt)
    den = lre * lre + lim * lim
    n_re, n_im = a_re - 1.0, a_im
    k_re = (n_re * lre + n_im * lim) / den
    k_im = (n_im * lre - n_re * lim) / den
    b_re, b_im = bre_ref[...], bim_ref[...]
    are_ref[...] = a_re
    aim_ref[...] = a_im
    bbre_ref[...] = k_re * b_re - k_im * b_im
    bbim_ref[...] = k_re * b_im + k_im * b_re


def _ssm_params(lambda_re, lambda_im, log_dt, b_re, b_im):
    depth, g, p = lambda_re.shape
    c = b_re.shape[-1]
    rep = lambda a: jnp.repeat(a.reshape(depth * g, p), c, axis=0)
    bt = lambda a: jnp.transpose(a, (0, 1, 3, 2)).reshape(depth * g * c, p)
    shape = jax.ShapeDtypeStruct((depth * g * c, p), F32)
    return pl.pallas_call(_ssm_param_kernel, out_shape=(shape,) * 4, name="s5_discretise")(
        rep(lambda_re), rep(lambda_im), rep(jnp.broadcast_to(log_dt[:, :, None], (depth, g, p))),
        bt(b_re), bt(b_im))


def _block_diag_in(bt, depth, g, c, p):
    m = g // 8
    a = bt.reshape(depth, m, 8, c, p)
    eye = jnp.eye(8, dtype=bt.dtype)
    return jnp.einsum('lmgcp,gh->lmgchp', a, eye).reshape(depth, m, 8 * c, 8 * p)


def _block_diag_out(cm, depth, g, c, p):
    m = g // 8
    a = jnp.transpose(cm, (0, 1, 3, 2)).reshape(depth, m, 8, p, c)
    eye = jnp.eye(8, dtype=cm.dtype)
    return jnp.einsum('lmgpc,gh->lmgphc', a, eye).reshape(depth, m, 8 * p, 8 * c)


def _mix_kernel(*refs, n_steps, zero_tail, **kw):
    if not zero_tail:
        _mix_body(*refs, **kw)
        return
    step = pl.program_id(0)
    y_ref = refs[-6]

    @pl.when(step < n_steps)
    def _():
        _mix_body(*refs, **kw)

    @pl.when(step >= n_steps)
    def _():
        y_ref[...] = jnp.zeros(y_ref.shape, F32)


def _mix_body(x_ref, prev_ref, h0_ref, win_ref, bin_ref, wpool_ref, pscale_ref, lam_ref,
              bre_ref, bim_ref, cre_ref, cim_ref, dskip_ref, wglu_ref, bglu_ref,
              wpa_ref, wpb_ref, wout_ref, lng_ref, lnb_ref, *rest,
              nb, tt, start_pos, alpha, aliased):
    if aliased:
        rest = rest[1:]
    y_ref, pool_ref, hlast_ref, ext_ref, bu_ref, hs_ref = rest
    rows = nb * tt
    halo = POOL_STATE * nb
    d_model = x_ref.shape[-1]
    pw = pscale_ref.shape[1]
    nst = lam_ref.shape[1] // 2
    step = pl.program_id(0)

    @pl.when(step == 0)
    def _():
        ext_ref[0:halo, :] = prev_ref[...]
        hs_ref[...] = h0_ref[...]

    if x_ref.ndim == 3:
        x = jnp.concatenate([x_ref[:, t, :] for t in range(tt)], axis=0)
    else:
        x = x_ref[...]
    xb = x.astype(BF16)

    u_a = _dot(xb, win_ref[:, 0:pw]) + bin_ref[:, 0:pw]
    ext_ref[halo:halo + rows, :] = u_a
    t_idx = lax.shift_right_logical(lax.broadcasted_iota(jnp.int32, (rows, 1), 0), int(math.log2(nb)))
    pos1 = t_idx + (step * tt + start_pos + 1)
    mixed = []
    for gi, w in enumerate(POOL_WINDOWS):
        sl = slice(gi * POOL_GROUP, (gi + 1) * POOL_GROUP)
        cur = u_a[:, sl]
        acc = cur
        for k in range(1, w):
            acc = acc + ext_ref[halo - k * nb:halo - k * nb + rows, sl]
        count = jnp.minimum(pos1, w).astype(F32)
        pooled = acc / count - cur
        mixed.append(_dot(pooled.astype(BF16), wpool_ref[gi]))
    out_a = jnp.concatenate(mixed, axis=-1) * pscale_ref[...]
    new_halo = ext_ref[rows:rows + halo, :]
    ext_ref[0:halo, :] = new_halo
    pool_ref[...] = new_halo
    proj_a = _dot(out_a.astype(BF16), wpa_ref[...])

    u_b = _dot(xb, win_ref[:, pw:2 * pw]) + bin_ref[:, pw:2 * pw]
    ub16 = u_b.astype(BF16)
    n_m = bre_ref.shape[0]
    kin, kst = bre_ref.shape[1], bre_ref.shape[2]
    for m in range(n_m):
        um = ub16[:, m * kin:(m + 1) * kin]
        bu_ref[:, m * kst:(m + 1) * kst] = _dot(um, bre_ref[m])
        bu_ref[:, nst + m * kst:nst + (m + 1) * kst] = _dot(um, bim_ref[m])
    if tt == 1:
        a_re, a_im = lam_ref[:, 0:nst], lam_ref[:, nst:]
        h_re, h_im = hs_ref[:, 0:nst], hs_ref[:, nst:]
        n_re = a_re * h_re - a_im * h_im + bu_ref[:, 0:nst]
        n_im = a_re * h_im + a_im * h_re + bu_ref[:, nst:]
        bu_ref[:, 0:nst] = n_re
        bu_ref[:, nst:] = n_im
        hs_ref[:, 0:nst] = n_re
        hs_ref[:, nst:] = n_im
    else:
        for q in range(nst // SCAN_COLS):
            c_re = slice(q * SCAN_COLS, (q + 1) * SCAN_COLS)
            c_im = slice(nst + q * SCAN_COLS, nst + (q + 1) * SCAN_COLS)
            a_re = jnp.broadcast_to(lam_ref[:, c_re], (nb, SCAN_COLS))
            a_im = jnp.broadcast_to(lam_ref[:, c_im], (nb, SCAN_COLS))

            def body(t, carry, c_re=c_re, c_im=c_im, a_re=a_re, a_im=a_im):
                h_re, h_im = carry
                r = pl.ds(pl.multiple_of(t * nb, nb), nb)
                n_re = a_re * h_re - a_im * h_im + bu_ref[r, c_re]
                n_im = a_re * h_im + a_im * h_re + bu_ref[r, c_im]
                bu_ref[r, c_re] = n_re
                bu_ref[r, c_im] = n_im
                return n_re, n_im

            h_re, h_im = lax.fori_loop(0, tt, body, (hs_ref[:, c_re], hs_ref[:, c_im]), unroll=True)
            hs_ref[:, c_re] = h_re
            hs_ref[:, c_im] = h_im
    hlast_ref[...] = hs_ref[...]
    ys = []
    for m in range(n_m):
        h_re = bu_ref[:, m * kst:(m + 1) * kst].astype(BF16)
        h_im = bu_ref[:, nst + m * kst:nst + (m + 1) * kst].astype(BF16)
        ys.append(_dot(h_re, cre_ref[m]) - _dot(h_im, cim_ref[m]))
    y = jnp.concatenate(ys, axis=-1) + dskip_ref[...] * u_b
    y = jax.nn.gelu(y)
    out_b = y * jax.nn.sigmoid(_dot(y.astype(BF16), wglu_ref[...]) + bglu_ref[...])
    proj_b = _dot(out_b.astype(BF16), wpb_ref[...])

    z_ga = _dot(xb, win_ref[:, 2 * pw:2 * pw + d_model]) + bin_ref[:, 2 * pw:2 * pw + d_model]
    merged = jax.nn.sigmoid(z_ga) * proj_a
    z_gb = _dot(xb, win_ref[:, 2 * pw + d_model:]) + bin_ref[:, 2 * pw + d_model:]
    merged = merged + jax.nn.sigmoid(z_gb) * proj_b
    mix = _dot(merged.astype(BF16), wout_ref[...])
    out = _layer_norm(alpha * x + mix, lng_ref[...], lnb_ref[...])
    if y_ref.shape[0] == rows:
        y_ref[...] = out
    else:
        y_ref[0:rows, :] = out
        y_ref[rows:, :] = jnp.zeros((y_ref.shape[0] - rows, d_model), F32)


def _token_mix(x_src, x_row0, row0, n_steps, nb, tt, start_pos, alpha, layer, prev, h0, wts, total_rows,
               y_buf=None, state_layer=None):
    state_layer = layer if state_layer is None else state_layer
    rows = nb * tt
    d_model = x_src.shape[-1]
    nst2 = h0.shape[-1]
    pw = prev.shape[-1]
    halo = POOL_STATE * nb
    aliased = y_buf is not None
    const = lambda a: _layer_spec(a, layer)
    tile = lambda i: jnp.minimum(i, n_steps - 1)
    if x_src.ndim == 3:
        x_spec = pl.BlockSpec((nb, tt, d_model), lambda i: (0, tile(i), 0))
    else:
        x_spec = pl.BlockSpec((rows, d_model), lambda i: (x_row0 // rows + tile(i), 0))
    in_specs = [x_spec, _layer_spec(prev, state_layer), _layer_spec(h0, state_layer)]
    in_specs += [const(w) for w in wts]
    args = [x_src, prev, h0, *wts]
    if aliased:
        in_specs.append(pl.BlockSpec(memory_space=pl.ANY))
        args.append(y_buf)
        y_spec = pl.BlockSpec((ROW_TILE, d_model), lambda i: (row0 // ROW_TILE, 0))
    else:
        y_spec = pl.BlockSpec((rows, d_model), lambda i: (row0 // rows + i, 0))
    out_shape = (jax.ShapeDtypeStruct((total_rows, d_model), F32),
                 jax.ShapeDtypeStruct((halo, pw), F32),
                 jax.ShapeDtypeStruct((nb, nst2), F32))
    out_specs = (y_spec,
                 pl.BlockSpec((halo, pw), lambda i: (0, 0)),
                 pl.BlockSpec((nb, nst2), lambda i: (0, 0)))
    zero_tail = not aliased
    assert aliased or row0 + (n_steps + 1) * rows == total_rows
    kern = functools.partial(_mix_kernel, n_steps=n_steps, zero_tail=zero_tail,
                             nb=nb, tt=tt, start_pos=start_pos, alpha=alpha, aliased=aliased)
    return pl.pallas_call(
        kern, grid=(n_steps + int(zero_tail),), in_specs=in_specs, out_specs=out_specs, out_shape=out_shape,
        scratch_shapes=[pltpu.VMEM((halo + rows, pw), F32), pltpu.VMEM((rows, nst2), F32),
                        pltpu.VMEM((nb, nst2), F32)],
        input_output_aliases={len(args) - 1: 0} if aliased else {},
        name="token_mix_decode" if aliased else "token_mix_prompt",
        compiler_params=_params("arbitrary"))(*args)


def _ffn_kernel(x_ref, wg_ref, wu_ref, wd_ref, g_ref, b_ref, o_ref, *, alpha):
    x = x_ref[...]
    ffn = _swiglu_split(x.astype(BF16), wg_ref, wu_ref, wd_ref)
    o_ref[...] = _layer_norm(alpha * x + ffn, g_ref[...], b_ref[...])


def _layer_spec(a, layer, **kw):
    return pl.BlockSpec((None,) + a.shape[1:], lambda *_, n=a.ndim - 1: (layer,) + (0,) * n, **kw)


def _dense_ffn(x_all, layer, wg, wu, wd, ln_g, ln_b, alpha):
    n, d_model = x_all.shape
    resident = lambda a: _layer_spec(a, layer, pipeline_mode=pl.Buffered(1))
    return pl.pallas_call(
        functools.partial(_ffn_kernel, alpha=alpha),
        grid=(n // ROW_TILE,),
        in_specs=[pl.BlockSpec((ROW_TILE, d_model), lambda i: (i, 0)),
                  resident(wg), resident(wu), resident(wd),
                  pl.BlockSpec((1, d_model), lambda i: (0, 0)), pl.BlockSpec((1, d_model), lambda i: (0, 0))],
        out_specs=pl.BlockSpec((ROW_TILE, d_model), lambda i: (i, 0)),
        out_shape=jax.ShapeDtypeStruct((n, d_model), F32),
        name="dense_swiglu_ln",
        compiler_params=_params("parallel"))(x_all, wg, wu, wd, ln_g, ln_b)


def _split_bf16(a):
    hi = a.astype(BF16)
    lo = (a - hi.astype(F32)).astype(BF16)
    return hi, lo


DEAD_RANK = -(1 << 20)


def _router_kernel(x_ref, w_ref, b_ref, idx_ref, gate_ref, cnt_ref, *, n_tokens):
    x_hi, x_lo = _split_bf16(x_ref[...])
    w_hi, w_lo = _split_bf16(w_ref[...])
    logits = _dot(x_hi, w_hi) + (_dot(x_lo, w_hi) + _dot(x_hi, w_lo)) + b_ref[...]
    lane = lax.broadcasted_iota(jnp.int32, logits.shape, 1)
    logits = jnp.where(lane < N_EXPERTS, logits, -jnp.inf)
    m1 = jnp.max(logits, axis=-1, keepdims=True)
    i1 = jnp.min(jnp.where(logits == m1, lane, LANES), axis=-1, keepdims=True)
    rest = jnp.where(lane == i1, -jnp.inf, logits)
    m2 = jnp.max(rest, axis=-1, keepdims=True)
    i2 = jnp.min(jnp.where(rest == m2, lane, LANES), axis=-1, keepdims=True)
    e2 = jnp.exp(m2 - m1)
    den = 1.0 + e2
    gate_ref[...] = jnp.where(lane == 0, 1.0 / den, e2 / den)

    row = lax.broadcasted_iota(jnp.int32, (ROW_TILE, 1), 0) + pl.program_id(0) * ROW_TILE
    live = row < n_tokens
    pick1 = jnp.where(jnp.logical_and(lane == i1, live), 1.0, 0.0)
    pick2 = jnp.where(jnp.logical_and(lane == i2, live), 1.0, 0.0)
    picks = pick1 + pick2
    r = lax.broadcasted_iota(jnp.int32, (ROW_TILE, ROW_TILE), 0)
    c = lax.broadcasted_iota(jnp.int32, (ROW_TILE, ROW_TILE), 1)
    earlier = _dot(jnp.where(c < r, 1.0, 0.0).astype(BF16), picks.astype(BF16))
    rank1 = jnp.sum(earlier * pick1, axis=-1, keepdims=True).astype(jnp.int32)
    rank2 = jnp.sum(earlier * pick2, axis=-1, keepdims=True).astype(jnp.int32)
    rank1 = jnp.where(live, rank1, DEAD_RANK)
    rank2 = jnp.where(live, rank2, DEAD_RANK)
    idx_ref[...] = jnp.where(lane == 0, i1, jnp.where(lane == 1, i2, jnp.where(lane == 2, rank1, rank2)))
    counts = jnp.sum(picks, axis=0, keepdims=True).astype(jnp.int32)
    cnt_ref[...] = jnp.broadcast_to(counts, cnt_ref.shape)


def _router(x_all, n_tokens, w_router, b_router):
    n, d_model = x_all.shape
    w = jnp.pad(w_router, ((0, 0), (0, LANES - N_EXPERTS)))
    b = jnp.pad(b_router, (0, LANES - N_EXPERTS))[None, :]
    n_tiles = n // ROW_TILE
    return pl.pallas_call(
        functools.partial(_router_kernel, n_tokens=n_tokens), grid=(n_tiles,),
        in_specs=[pl.BlockSpec((ROW_TILE, d_model), lambda i: (i, 0)),
                  pl.BlockSpec((d_model, LANES), lambda i: (0, 0)),
                  pl.BlockSpec((1, LANES), lambda i: (0, 0))],
        out_specs=(pl.BlockSpec((ROW_TILE, LANES), lambda i: (i, 0)),
                   pl.BlockSpec((ROW_TILE, LANES), lambda i: (i, 0)),
                   pl.BlockSpec((8, LANES), lambda i: (i, 0))),
        out_shape=(jax.ShapeDtypeStruct((n, LANES), jnp.int32), jax.ShapeDtypeStruct((n, LANES), F32),
                   jax.ShapeDtypeStruct((n_tiles * 8, LANES), jnp.int32)),
        name="moe_router",
        compiler_params=_params("parallel"))(x_all, w, b)


def _group_copy(src_ref, src_row, dst_ref, dst_row, sem):
    return pltpu.make_async_copy(src_ref.at[pl.ds(pl.multiple_of(src_row, GROUP), GROUP)],
                                 dst_ref.at[pl.ds(pl.multiple_of(dst_row, GROUP), GROUP)], sem)


def _wait_groups(count, src_ref, dst_ref, sem):
    def wait(_, carry):
        _group_copy(src_ref, 0, dst_ref, 0, sem).wait()
        return carry
    lax.fori_loop(0, count, wait, 0)


def _stage_rows(expert, rank, first_ref, step):
    pos = rank
    for e in range(N_EXPERTS):
        pos = pos + jnp.where(expert == e, first_ref[step * N_EXPERTS + e], 0)
    return pos


def _regroup_kernel(w0_ref, off_ref, ngrp_ref, part_ref, pend_ref, first_ref, x_ref, idx_ref, xs_hbm,
                    stage_ref, carry_ref, zero_ref, sem):
    step = pl.program_id(0)

    @pl.when(step == 0)
    def _():
        carry_ref[...] = jnp.zeros(carry_ref.shape, BF16)
        zero_ref[...] = jnp.zeros(zero_ref.shape, BF16)

    xb = x_ref[...].astype(BF16)
    stage_row = lax.broadcasted_iota(jnp.int32, (STAGE_ROWS, ROW_TILE), 0)
    idx = idx_ref[...]
    hit = None
    for choice in range(TOP_K):
        pos = _stage_rows(idx[choice:choice + 1, :], idx[TOP_K + choice:TOP_K + choice + 1, :], first_ref, step)
        hit = stage_row == pos if hit is None else jnp.logical_or(hit, stage_row == pos)
    stage_ref[...] = _dot(jnp.where(hit, 1.0, 0.0).astype(BF16), xb).astype(BF16)

    started = 0
    for e in range(N_EXPERTS):
        k = step * N_EXPERTS + e
        off, w0, ngrp = off_ref[k], w0_ref[k], ngrp_ref[k]
        head = pl.ds(pl.multiple_of(off, GROUP), GROUP)
        stage_ref[head, :] = stage_ref[head, :] + carry_ref[e]

        def send(q, carry, off=off, w0=w0):
            _group_copy(stage_ref, off + q * GROUP, xs_hbm, w0 + q * GROUP, sem).start()
            return carry

        lax.fori_loop(0, ngrp, send, 0)
        tail = stage_ref[pl.ds(pl.multiple_of(off + ngrp * GROUP, GROUP), GROUP), :]
        carry_ref[e] = jnp.where(part_ref[k] != 0, tail, jnp.zeros_like(tail))
        started = started + ngrp
    _wait_groups(started, stage_ref, xs_hbm, sem)

    @pl.when(step == pl.num_programs(0) - 1)
    def _():
        flushed = 0
        for e in range(N_EXPERTS):
            k = step * N_EXPERTS + e
            row = w0_ref[k] + ngrp_ref[k] * GROUP
            is_open = part_ref[k]

            @pl.when(is_open != 0)
            def _(row=row, e=e):
                pltpu.make_async_copy(carry_ref.at[e], xs_hbm.at[pl.ds(pl.multiple_of(row, GROUP), GROUP)],
                                      sem).start()

            row = row + is_open * GROUP
            n_zero = (pend_ref[e] - row) // GROUP

            def fill(q, carry, row=row):
                _group_copy(zero_ref, 0, xs_hbm, row + q * GROUP, sem).start()
                return carry

            lax.fori_loop(0, n_zero, fill, 0)
            flushed = flushed + is_open + n_zero
        _wait_groups(flushed, zero_ref, xs_hbm, sem)


def _regroup(x_all, idx_rows, tables, n_slots):
    n, d_model = x_all.shape
    return pl.pallas_call(
        _regroup_kernel,
        grid_spec=pltpu.PrefetchScalarGridSpec(
            num_scalar_prefetch=6, grid=(n // ROW_TILE,),
            in_specs=[pl.BlockSpec((ROW_TILE, d_model), lambda i, *_: (i, 0)),
                      pl.BlockSpec((2 * TOP_K, ROW_TILE), lambda i, *_: (0, i))],
            out_specs=pl.BlockSpec(memory_space=pl.ANY),
            scratch_shapes=[pltpu.VMEM((STAGE_ROWS, d_model), BF16),
                            pltpu.VMEM((N_EXPERTS, GROUP, d_model), BF16),
                            pltpu.VMEM((GROUP, d_model), BF16),
                            pltpu.SemaphoreType.DMA(())]),
        out_shape=jax.ShapeDtypeStruct((n_slots, d_model), BF16),
        name="moe_regroup",
        compiler_params=_params("arbitrary"))(
            tables["w0"], tables["off"], tables["ngrp"], tables["part"], tables["pend"], tables["first"],
            x_all, idx_rows)


def _moe_kernel(item_e_ref, item_b0_ref, item_nb_ref, nvalid_ref, *refs):
    xs_refs = refs[:ITEM_BLOCKS]
    wg_ref, wu_ref, wd_ref, ys_hbm, wgub_ref, wdb_ref, acc_ref, stage_ref, sem = refs[ITEM_BLOCKS:]
    item, c = pl.program_id(0), pl.program_id(1)
    last_item, last_c = pl.num_programs(0) - 1, pl.num_programs(1) - 1
    n_blocks = ys_hbm.shape[0] // MOE_ROWS
    nb, b0 = item_nb_ref[item], item_b0_ref[item]

    def block_out(src_ref, block):
        rows = pl.ds(pl.multiple_of(block * MOE_ROWS, MOE_ROWS), MOE_ROWS)
        return pltpu.make_async_copy(src_ref, ys_hbm.at[rows], sem)

    def start_and_wait(count, copy_of):
        def start(q, carry):
            copy_of(q).start()
            return carry

        def wait(q, carry):
            copy_of(q).wait()
            return carry

        lax.fori_loop(0, count, start, 0)
        lax.fori_loop(0, count, wait, 0)

    @pl.when(jnp.logical_and(item == 0, c == 0))
    def _():
        acc_ref[...] = jnp.zeros(acc_ref.shape, F32)

    @pl.when(nb > 0)
    def _():
        for p in range(wd_ref.shape[0] // LANES):
            cols = slice(p * LANES, (p + 1) * LANES)
            wgub_ref[:, 2 * p * LANES:(2 * p + 1) * LANES] = wg_ref[:, cols].astype(BF16)
            wgub_ref[:, (2 * p + 1) * LANES:(2 * p + 2) * LANES] = wu_ref[:, cols].astype(BF16)
        wdb_ref[...] = wd_ref[...].astype(BF16)

    for j in range(ITEM_BLOCKS):
        @pl.when(j < nb)
        def _(j=j):
            part = _swiglu_interleaved(xs_refs[j][...], wgub_ref, wdb_ref)
            total = jnp.where(c == 0, part, acc_ref[j] + part)
            acc_ref[j] = total
            stage_ref[j] = total.astype(BF16)

    @pl.when(c == last_c)
    def _():
        start_and_wait(nb, lambda q: block_out(stage_ref.at[q], b0 + q))

    @pl.when(jnp.logical_and(item == last_item, c == last_c))
    def _():
        stage_ref[0] = jnp.zeros(stage_ref.shape[1:], BF16)
        first_unused = nvalid_ref[0]
        start_and_wait(n_blocks - first_unused, lambda q: block_out(stage_ref.at[0], first_unused + q))


def _grouped_swiglu(xs, items, nvalid, layer, wg, wu, wd):
    n_slots, d_model = xs.shape
    d_ff = wg.shape[3]
    fc = MOE_FF_CHUNK
    n_chunks = d_ff // fc
    n_blocks = n_slots // MOE_ROWS
    item_e, item_b0, item_nb = items
    assert d_ff % fc == 0 and n_slots % MOE_ROWS == 0
    chunk = lambda c, nb, it: jnp.where(nb[it] > 0, c, n_chunks - 1)
    w_in_map = lambda it, c, e, b0, nb, nv: (layer, e[it], 0, chunk(c, nb, it))
    w_out_map = lambda it, c, e, b0, nb, nv: (layer, e[it], chunk(c, nb, it), 0)
    xs_spec = lambda k: pl.BlockSpec(
        (MOE_ROWS, d_model), lambda it, c, e, b0, nb, nv: (jnp.minimum(b0[it] + k, n_blocks - 1), 0))
    return pl.pallas_call(
        _moe_kernel,
        grid_spec=pltpu.PrefetchScalarGridSpec(
            num_scalar_prefetch=4, grid=(item_e.shape[0], n_chunks),
            in_specs=[xs_spec(k) for k in range(ITEM_BLOCKS)] + [
                pl.BlockSpec((None, None, d_model, fc), w_in_map),
                pl.BlockSpec((None, None, d_model, fc), w_in_map),
                pl.BlockSpec((None, None, fc, d_model), w_out_map)],
            out_specs=pl.BlockSpec(memory_space=pl.ANY),
            scratch_shapes=[pltpu.VMEM((d_model, 2 * fc), BF16), pltpu.VMEM((fc, d_model), BF16),
                            pltpu.VMEM((ITEM_BLOCKS, MOE_ROWS, d_model), F32),
                            pltpu.VMEM((ITEM_BLOCKS, MOE_ROWS, d_model), BF16),
                            pltpu.SemaphoreType.DMA(())]),
        out_shape=jax.ShapeDtypeStruct((n_slots, d_model), BF16),
        name="moe_grouped_swiglu",
        compiler_params=_params("arbitrary", "arbitrary"))(
            item_e, item_b0, item_nb, nvalid, *([xs] * ITEM_BLOCKS), wg, wu, wd)


def _combine_kernel(w0_ref, off_ref, nwin_ref, first_ref, x_ref, idx_ref, gate_ref, g_ref, b_ref, ys_hbm,
                    *rest, alpha, batch_major):
    if batch_major:
        prompt_ref, tail_ref, stage_ref, sem = rest
    else:
        o_ref, stage_ref, sem = rest
    step = pl.program_id(0)

    def fetch_tile(tile, buf):
        stage_ref[buf] = jnp.zeros(stage_ref.shape[1:], BF16)
        for e in range(N_EXPERTS):
            k = tile * N_EXPERTS + e
            off, w0 = off_ref[k], w0_ref[k]

            def fetch(q, carry, off=off, w0=w0):
                _group_copy(ys_hbm, w0 + q * GROUP, stage_ref.at[buf], off + q * GROUP, sem.at[buf]).start()
                return carry

            lax.fori_loop(0, nwin_ref[k], fetch, 0)

    @pl.when(step == 0)
    def _():
        fetch_tile(step, 0)

    buf = lax.rem(step, 2)

    @pl.when(step + 1 < pl.num_programs(0))
    def _():
        fetch_tile(step + 1, 1 - buf)

    n_groups = 0
    for e in range(N_EXPERTS):
        n_groups = n_groups + nwin_ref[step * N_EXPERTS + e]
    _wait_groups(n_groups, ys_hbm, stage_ref.at[buf], sem.at[buf])

    stage_col = lax.broadcasted_iota(jnp.int32, (ROW_TILE, STAGE_ROWS), 1)
    idx = idx_ref[...]
    gates = gate_ref[...]
    ys = stage_ref[buf]
    f = None
    for choice in range(TOP_K):
        pos = _stage_rows(idx[:, choice:choice + 1], idx[:, TOP_K + choice:TOP_K + choice + 1], first_ref, step)
        pick = jnp.where(stage_col == pos, 1.0, 0.0).astype(BF16)
        term = gates[:, choice:choice + 1] * _dot(pick, ys)
        f = term if f is None else f + term
    out = _layer_norm(alpha * x_ref[...] + f, g_ref[...], b_ref[...])
    if not batch_major:
        o_ref[...] = out
        return
    nb, tt = prompt_ref.shape[0], prompt_ref.shape[1]
    last = pl.num_programs(0) - 1

    @pl.when(step < last)
    def _():
        for t in range(tt):
            prompt_ref[:, t, :] = out[t * nb:(t + 1) * nb, :]

    @pl.when(step == last)
    def _():
        tail_ref[...] = out


def _combine(x_all, idx, gates, tables, ys, ln_g, ln_b, alpha, prompt_shape=None):
    n, d_model = x_all.shape
    n_tiles = n // ROW_TILE
    if prompt_shape is None:
        out_specs = pl.BlockSpec((ROW_TILE, d_model), lambda i, *_: (i, 0))
        out_shape = jax.ShapeDtypeStruct((n, d_model), F32)
    else:
        batch, seq = prompt_shape
        tt = ROW_TILE // batch
        assert batch * seq == (n_tiles - 1) * ROW_TILE
        out_specs = (pl.BlockSpec((batch, tt, d_model), lambda i, *_: (0, jnp.minimum(i, n_tiles - 2), 0)),
                     pl.BlockSpec((ROW_TILE, d_model), lambda i, *_: (0, 0)))
        out_shape = (jax.ShapeDtypeStruct((batch, seq, d_model), F32),
                     jax.ShapeDtypeStruct((ROW_TILE, d_model), F32))
    return pl.pallas_call(
        functools.partial(_combine_kernel, alpha=alpha, batch_major=prompt_shape is not None),
        grid_spec=pltpu.PrefetchScalarGridSpec(
            num_scalar_prefetch=4, grid=(n_tiles,),
            in_specs=[pl.BlockSpec((ROW_TILE, d_model), lambda i, *_: (i, 0)),
                      pl.BlockSpec((ROW_TILE, LANES), lambda i, *_: (i, 0)),
                      pl.BlockSpec((ROW_TILE, LANES), lambda i, *_: (i, 0)),
                      pl.BlockSpec((1, d_model), lambda i, *_: (0, 0)),
                      pl.BlockSpec((1, d_model), lambda i, *_: (0, 0)),
                      pl.BlockSpec(memory_space=pl.ANY)],
            out_specs=out_specs,
            scratch_shapes=[pltpu.VMEM((2, STAGE_ROWS, d_model), BF16), pltpu.SemaphoreType.DMA((2,))]),
        out_shape=out_shape,
        name="moe_combine_ln",
        compiler_params=_params("arbitrary"))(
            tables["w0"], tables["off"], tables["nwin"], tables["first"], x_all, idx, gates, ln_g, ln_b, ys)


def _routing_tables(counts, n_slots):
    count = jnp.sum(counts, axis=0)
    blocks = (count + MOE_ROWS - 1) // MOE_ROWS
    blk_end = jnp.cumsum(blocks)
    base = (blk_end - blocks) * MOE_ROWS
    n_items = n_slots // MOE_ROWS // ITEM_BLOCKS + N_EXPERTS
    items_e = (blocks + ITEM_BLOCKS - 1) // ITEM_BLOCKS
    item_end = jnp.cumsum(items_e)
    n_used = item_end[-1]
    item = jnp.arange(n_items, dtype=jnp.int32)
    used_item = jnp.minimum(item, n_used - 1)
    item_e = jnp.minimum(jnp.sum((used_item[:, None] >= item_end[None, :]).astype(jnp.int32), axis=1),
                         N_EXPERTS - 1)
    of_item = lambda per_expert: jnp.sum(
        jnp.where(item_e[:, None] == jnp.arange(N_EXPERTS)[None, :], per_expert[None, :], 0), axis=1)
    local = used_item - of_item(item_end - items_e)
    item_b0 = of_item(blk_end - blocks) + ITEM_BLOCKS * local
    item_nb = jnp.where(item < n_used, jnp.clip(of_item(blocks) - ITEM_BLOCKS * local, 0, ITEM_BLOCKS), 0)
    items = tuple(a.astype(jnp.int32) for a in (item_e, item_b0, item_nb))
    start = base[None, :] + jnp.cumsum(counts, axis=0) - counts
    end = start + counts
    w0 = start // GROUP * GROUP
    ngrp = end // GROUP - start // GROUP
    part = (end % GROUP != 0).astype(jnp.int32)
    nwin = ngrp + part
    off = (jnp.cumsum(nwin, axis=1) - nwin) * GROUP
    nvalid = blk_end[-1]
    pend = (blk_end * MOE_ROWS).at[-1].set(n_slots)
    flat = lambda a: a.reshape(-1).astype(jnp.int32)
    tables = dict(w0=flat(w0), off=flat(off), first=flat(start - w0 + off), ngrp=flat(ngrp), part=flat(part),
                  nwin=flat(nwin), pend=flat(pend))
    return tables, items, nvalid.reshape(1).astype(jnp.int32)


def _moe_ffn(x_all, n_tokens, layer, w_router, b_router, wg, wu, wd, ln_g, ln_b, alpha, prompt_shape=None):
    idx, gates, cnt = _router(x_all, n_tokens, w_router, b_router)
    n_slots = TOP_K * n_tokens + N_EXPERTS * MOE_ROWS
    n_slots = -(-n_slots // (MOE_ROWS * ITEM_BLOCKS)) * (MOE_ROWS * ITEM_BLOCKS)
    tables, items, nvalid = _routing_tables(cnt[::8, :N_EXPERTS], n_slots)
    xs = _regroup(x_all, idx[:, :2 * TOP_K].T, tables, n_slots)
    ys = _grouped_swiglu(xs, items, nvalid, layer, wg, wu, wd)
    return _combine(x_all, idx, gates, tables, ys, ln_g, ln_b, alpha, prompt_shape)


def kernel(x_prompt, x_sample, state_pool, state_ssm_re, state_ssm_im, ln1_g, ln1_b, w_in, b_in, w_pool, pool_scale, lambda_re, lambda_im, log_dt, b_re, b_im, c_re, c_im, d_skip, w_glu, b_glu, w_proj_a, w_proj_b, w_out, ln2_g, ln2_b, w_ffn_gate, w_ffn_up, w_ffn_down, w_router, b_router, w_moe_gate, w_moe_up, w_moe_down):
    batch, seq, d_model = x_prompt.shape
    dec_batch = x_sample.shape[0]
    depth = w_in.shape[0]
    n_groups, n_state = lambda_re.shape[1], lambda_re.shape[2]
    chan = b_re.shape[-1]
    nst = n_groups * n_state
    pw = state_pool.shape[-1]
    alpha = (2.0 * depth) ** 0.25
    n_prompt = batch * seq
    n_tokens = n_prompt + dec_batch
    total_rows = -(-n_tokens // ROW_TILE) * ROW_TILE
    assert x_sample.shape[1] == 1 and n_prompt % ROW_TILE == 0 and seq % MIX_STEPS == 0
    assert batch * MIX_STEPS == ROW_TILE and dec_batch <= ROW_TILE

    rows_of = lambda a: a[:, None, :]
    a_re, a_im, bb_re, bb_im = _ssm_params(lambda_re, lambda_im, log_dt, b_re, b_im)
    lam = jnp.concatenate([a_re[::chan].reshape(depth, 1, nst), a_im[::chan].reshape(depth, 1, nst)], axis=2)
    wts = (w_in.astype(BF16), rows_of(b_in), w_pool.astype(BF16), rows_of(pool_scale), lam,
           _block_diag_in(bb_re, depth, n_groups, chan, n_state).astype(BF16),
           _block_diag_in(bb_im, depth, n_groups, chan, n_state).astype(BF16),
           _block_diag_out(c_re, depth, n_groups, chan, n_state).astype(BF16),
           _block_diag_out(c_im, depth, n_groups, chan, n_state).astype(BF16),
           d_skip.reshape(depth, 1, -1), w_glu.astype(BF16), rows_of(b_glu),
           w_proj_a.astype(BF16), w_proj_b.astype(BF16), w_out.astype(BF16),
           rows_of(ln1_g), rows_of(ln1_b))
    ffn_wts = (w_ffn_gate.astype(BF16), w_ffn_up.astype(BF16), w_ffn_down.astype(BF16))
    prev_s = jnp.transpose(state_pool, (0, 2, 1, 3)).reshape(depth, POOL_STATE * dec_batch, pw)
    h0_s = jnp.concatenate([state_ssm_re.reshape(depth, dec_batch, nst),
                            state_ssm_im.reshape(depth, dec_batch, nst)], axis=2)
    zero_prev = jnp.zeros((1, POOL_STATE * batch, pw), F32)
    zero_h = jnp.zeros((1, batch, 2 * nst), F32)

    x_p = x_prompt
    x_s = x_sample.reshape(dec_batch, d_model)
    s_row0 = 0
    outs = [[] for _ in range(6)]
    for l in range(depth):
        y_mix, pool_p, h_p = _token_mix(x_p, 0, 0, seq // MIX_STEPS, batch, MIX_STEPS, 0, alpha, l,
                                        zero_prev, zero_h, wts, total_rows, state_layer=0)
        y_mix, pool_s, h_s = _token_mix(x_s, s_row0, n_prompt, 1, dec_batch, 1, PAST_LEN, alpha, l,
                                        prev_s, h0_s, wts, total_rows, y_buf=y_mix)
        outs[0].append(jnp.transpose(pool_p.reshape(POOL_STATE, batch, pw), (1, 0, 2)))
        outs[1].append(h_p[:, :nst].reshape(batch, n_groups, n_state))
        outs[2].append(h_p[:, nst:].reshape(batch, n_groups, n_state))
        outs[3].append(jnp.transpose(pool_s.reshape(POOL_STATE, dec_batch, pw), (1, 0, 2)))
        outs[4].append(h_s[:, :nst].reshape(dec_batch, n_groups, n_state))
        outs[5].append(h_s[:, nst:].reshape(dec_batch, n_groups, n_state))
        j = l // 2
        if l % 2 == 0:
            y_all = _dense_ffn(y_mix, j, *ffn_wts, ln2_g[l][None, :], ln2_b[l][None, :], alpha)
        else:
            prompt_shape = (batch, seq) if l == depth - 1 else None
            y_all = _moe_ffn(y_mix, n_tokens, j, w_router[j], b_router[j], w_moe_gate, w_moe_up,
                             w_moe_down, ln2_g[l][None, :], ln2_b[l][None, :], alpha, prompt_shape)
        x_p = x_s = y_all
        s_row0 = n_prompt
    if isinstance(y_all, (tuple, list)):
        y_p, tail = y_all
        y_s = tail[:dec_batch].reshape(dec_batch, 1, d_model)
    else:
        y_p = jnp.transpose(y_all[:n_prompt].reshape(seq, batch, d_model), (1, 0, 2))
        y_s = y_all[n_prompt:n_tokens].reshape(dec_batch, 1, d_model)
    return (y_p, y_s, *[jnp.stack(o) for o in outs])
```

```python
import functools
import math

import jax
import jax.numpy as jnp
from jax import lax
from jax.experimental import pallas as pl
from jax.experimental.pallas import tpu as pltpu

F32 = jnp.float32
BF16 = jnp.bfloat16

POOL_WINDOWS = (2, 4, 8, 16)
POOL_STATE = max(POOL_WINDOWS) - 1
POOL_GROUP = 128
SSM_GROUP = 16
SSM_STATE = 64
N_EXPERTS = 8
TOP_K = 2
LN_EPS = 1e-5
PAST_LEN = 16384

LANES = 128
MXU_COLS = 256
ROW_TILE = 512
MIX_STEPS = 64
SCAN_COLS = 512
MOE_ROWS = 512
ITEM_BLOCKS = 4
MOE_FF_CHUNK = 896
GROUP = 16
STAGE_ROWS = TOP_K * ROW_TILE + N_EXPERTS * 2 * GROUP
VMEM_LIMIT = 56 * 1024 * 1024


def _dot(a, b):
    return jnp.dot(a, b, preferred_element_type=F32)


def _layer_norm(r, g, b):
    mu = jnp.mean(r, axis=-1, keepdims=True)
    d = r - mu
    var = jnp.mean(d * d, axis=-1, keepdims=True)
    return d * lax.rsqrt(var + LN_EPS) * g + b


def _swiglu(xb, gate_up_piece, n_pieces, wd_ref):
    pieces = []
    for p in range(n_pieces):
        gate, up = gate_up_piece(p)
        pieces.append((jax.nn.silu(gate) * up).astype(BF16))
    return _dot(jnp.concatenate(pieces, axis=-1), wd_ref[...])


def _swiglu_split(xb, wg_ref, wu_ref, wd_ref):
    def piece(p):
        cols = slice(p * MXU_COLS, (p + 1) * MXU_COLS)
        return _dot(xb, wg_ref[:, cols]), _dot(xb, wu_ref[:, cols])
    return _swiglu(xb, piece, wd_ref.shape[0] // MXU_COLS, wd_ref)


def _swiglu_interleaved(xb, wgu_piece, wd_ref):
    def piece(p):
        gate_up = _dot(xb, wgu_piece(p))
        return gate_up[:, :LANES], gate_up[:, LANES:]
    return _swiglu(xb, piece, wd_ref.shape[0] // LANES, wd_ref)


def _params(*sem):
    return pltpu.CompilerParams(dimension_semantics=sem, vmem_limit_bytes=VMEM_LIMIT)


def _ssm_param_kernel(lre_ref, lim_ref, ldt_ref, bre_ref, bim_ref, are_ref, aim_ref, bbre_ref, bbim_ref):
    lre, lim = lre_ref[...], lim_ref[...]
    dt = jnp.exp(ldt_ref[...])
    mag = jnp.exp(lre * dt)
    a_re = mag * jnp.cos(lim * dt)
    a_im = mag * jnp.sin(lim * dt)
    den = lre * lre + lim * lim
    n_re, n_im = a_re - 1.0, a_im
    k_re = (n_re * lre + n_im * lim) / den
    k_im = (n_im * lre - n_re * lim) / den
    b_re, b_im = bre_ref[...], bim_ref[...]
    are_ref[...] = a_re
    aim_ref[...] = a_im
    bbre_ref[...] = k_re * b_re - k_im * b_im
    bbim_ref[...] = k_re * b_im + k_im * b_re


def _ssm_params(lambda_re, lambda_im, log_dt, b_re, b_im):
    depth, g, p = lambda_re.shape
    c = b_re.shape[-1]
    rep = lambda a: jnp.repeat(a.reshape(depth * g, p), c, axis=0)
    bt = lambda a: jnp.transpose(a, (0, 1, 3, 2)).reshape(depth * g * c, p)
    shape = jax.ShapeDtypeStruct((depth * g * c, p), F32)
    return pl.pallas_call(_ssm_param_kernel, out_shape=(shape,) * 4, name="s5_discretise")(
        rep(lambda_re), rep(lambda_im), rep(jnp.broadcast_to(log_dt[:, :, None], (depth, g, p))),
        bt(b_re), bt(b_im))


def _block_diag_in(bt, depth, g, c, p):
    m = g // 8
    a = bt.reshape(depth, m, 8, c, p)
    eye = jnp.eye(8, dtype=bt.dtype)
    return jnp.einsum('lmgcp,gh->lmgchp', a, eye).reshape(depth, m, 8 * c, 8 * p)


def _block_diag_out(cm, depth, g, c, p):
    m = g // 8
    a = jnp.transpose(cm, (0, 1, 3, 2)).reshape(depth, m, 8, p, c)
    eye = jnp.eye(8, dtype=cm.dtype)
    return jnp.einsum('lmgpc,gh->lmgphc', a, eye).reshape(depth, m, 8 * p, 8 * c)


def _mix_kernel(*refs, n_steps, zero_tail, **kw):
    if not zero_tail:
        _mix_body(*refs, **kw)
        return
    step = pl.program_id(0)
    y_ref = refs[-6]

    @pl.when(step < n_steps)
    def _():
        _mix_body(*refs, **kw)

    @pl.when(step >= n_steps)
    def _():
        y_ref[...] = jnp.zeros(y_ref.shape, F32)


def _mix_body(x_ref, prev_ref, h0_ref, win_ref, bin_ref, wpool_ref, pscale_ref, lam_ref,
              bre_ref, bim_ref, cre_ref, cim_ref, dskip_ref, wglu_ref, bglu_ref,
              wpa_ref, wpb_ref, wout_ref, lng_ref, lnb_ref, *rest,
              nb, tt, start_pos, alpha, aliased):
    if aliased:
        rest = rest[1:]
    y_ref, pool_ref, hlast_ref, ext_ref, bu_ref, hs_ref = rest
    rows = nb * tt
    halo = POOL_STATE * nb
    d_model = x_ref.shape[-1]
    pw = pscale_ref.shape[1]
    nst = lam_ref.shape[1] // 2
    step = pl.program_id(0)

    @pl.when(step == 0)
    def _():
        ext_ref[0:halo, :] = prev_ref[...]
        hs_ref[...] = h0_ref[...]

    if x_ref.ndim == 3:
        x = jnp.concatenate([x_ref[:, t, :] for t in range(tt)], axis=0)
    else:
        x = x_ref[...]
    xb = x.astype(BF16)

    u_a = _dot(xb, win_ref[:, 0:pw]) + bin_ref[:, 0:pw]
    ext_ref[halo:halo + rows, :] = u_a
    t_idx = lax.shift_right_logical(lax.broadcasted_iota(jnp.int32, (rows, 1), 0), int(math.log2(nb)))
    pos1 = t_idx + (step * tt + start_pos + 1)
    mixed = []
    for gi, w in enumerate(POOL_WINDOWS):
        sl = slice(gi * POOL_GROUP, (gi + 1) * POOL_GROUP)
        cur = u_a[:, sl]
        acc = cur
        for k in range(1, w):
            acc = acc + ext_ref[halo - k * nb:halo - k * nb + rows, sl]
        count = jnp.minimum(pos1, w).astype(F32)
        pooled = acc / count - cur
        mixed.append(_dot(pooled.astype(BF16), wpool_ref[gi]))
    out_a = jnp.concatenate(mixed, axis=-1) * pscale_ref[...]
    new_halo = ext_ref[rows:rows + halo, :]
    ext_ref[0:halo, :] = new_halo
    pool_ref[...] = new_halo
    proj_a = _dot(out_a.astype(BF16), wpa_ref[...])

    u_b = _dot(xb, win_ref[:, pw:2 * pw]) + bin_ref[:, pw:2 * pw]
    ub16 = u_b.astype(BF16)
    n_m = bre_ref.shape[0]
    kin, kst = bre_ref.shape[1], bre_ref.shape[2]
    for m in range(n_m):
        um = ub16[:, m * kin:(m + 1) * kin]
        bu_ref[:, m * kst:(m + 1) * kst] = _dot(um, bre_ref[m])
        bu_ref[:, nst + m * kst:nst + (m + 1) * kst] = _dot(um, bim_ref[m])
    if tt == 1:
        a_re, a_im = lam_ref[:, 0:nst], lam_ref[:, nst:]
        h_re, h_im = hs_ref[:, 0:nst], hs_ref[:, nst:]
        n_re = a_re * h_re - a_im * h_im + bu_ref[:, 0:nst]
        n_im = a_re * h_im + a_im * h_re + bu_ref[:, nst:]
        bu_ref[:, 0:nst] = n_re
        bu_ref[:, nst:] = n_im
        hs_ref[:, 0:nst] = n_re
        hs_ref[:, nst:] = n_im
    else:
        for q in range(nst // SCAN_COLS):
            c_re = slice(q * SCAN_COLS, (q + 1) * SCAN_COLS)
            c_im = slice(nst + q * SCAN_COLS, nst + (q + 1) * SCAN_COLS)
            a_re = jnp.broadcast_to(lam_ref[:, c_re], (nb, SCAN_COLS))
            a_im = jnp.broadcast_to(lam_ref[:, c_im], (nb, SCAN_COLS))

            def body(t, carry, c_re=c_re, c_im=c_im, a_re=a_re, a_im=a_im):
                h_re, h_im = carry
                r = pl.ds(pl.multiple_of(t * nb, nb), nb)
                n_re = a_re * h_re - a_im * h_im + bu_ref[r, c_re]
                n_im = a_re * h_im + a_im * h_re + bu_ref[r, c_im]
                bu_ref[r, c_re] = n_re
                bu_ref[r, c_im] = n_im
                return n_re, n_im

            h_re, h_im = lax.fori_loop(0, tt, body, (hs_ref[:, c_re], hs_ref[:, c_im]), unroll=True)
            hs_ref[:, c_re] = h_re
            hs_ref[:, c_im] = h_im
    hlast_ref[...] = hs_ref[...]
    ys = []
    for m in range(n_m):
        h_re = bu_ref[:, m * kst:(m + 1) * kst].astype(BF16)
        h_im = bu_ref[:, nst + m * kst:nst + (m + 1) * kst].astype(BF16)
        ys.append(_dot(h_re, cre_ref[m]) - _dot(h_im, cim_ref[m]))
    y = jnp.concatenate(ys, axis=-1) + dskip_ref[...] * u_b
    y = jax.nn.gelu(y)
    out_b = y * jax.nn.sigmoid(_dot(y.astype(BF16), wglu_ref[...]) + bglu_ref[...])
    proj_b = _dot(out_b.astype(BF16), wpb_ref[...])

    z_ga = _dot(xb, win_ref[:, 2 * pw:2 * pw + d_model]) + bin_ref[:, 2 * pw:2 * pw + d_model]
    merged = jax.nn.sigmoid(z_ga) * proj_a
    z_gb = _dot(xb, win_ref[:, 2 * pw + d_model:]) + bin_ref[:, 2 * pw + d_model:]
    merged = merged + jax.nn.sigmoid(z_gb) * proj_b
    mix = _dot(merged.astype(BF16), wout_ref[...])
    out = _layer_norm(alpha * x + mix, lng_ref[...], lnb_ref[...])
    if y_ref.shape[0] == rows:
        y_ref[...] = out
    else:
        y_ref[0:rows, :] = out
        y_ref[rows:, :] = jnp.zeros((y_ref.shape[0] - rows, d_model), F32)


def _token_mix(x_src, x_row0, row0, n_steps, nb, tt, start_pos, alpha, layer, prev, h0, wts, total_rows,
               y_buf=None, state_layer=None):
    state_layer = layer if state_layer is None else state_layer
    rows = nb * tt
    d_model = x_src.shape[-1]
    nst2 = h0.shape[-1]
    pw = prev.shape[-1]
    halo = POOL_STATE * nb
    aliased = y_buf is not None
    const = lambda a: _layer_spec(a, layer)
    tile = lambda i: jnp.minimum(i, n_steps - 1)
    if x_src.ndim == 3:
        x_spec = pl.BlockSpec((nb, tt, d_model), lambda i: (0, tile(i), 0))
    else:
        x_spec = pl.BlockSpec((rows, d_model), lambda i: (x_row0 // rows + tile(i), 0))
    in_specs = [x_spec, _layer_spec(prev, state_layer), _layer_spec(h0, state_layer)]
    in_specs += [const(w) for w in wts]
    args = [x_src, prev, h0, *wts]
    if aliased:
        in_specs.append(pl.BlockSpec(memory_space=pl.ANY))
        args.append(y_buf)
        y_spec = pl.BlockSpec((ROW_TILE, d_model), lambda i: (row0 // ROW_TILE, 0))
    else:
        y_spec = pl.BlockSpec((rows, d_model), lambda i: (row0 // rows + i, 0))
    out_shape = (jax.ShapeDtypeStruct((total_rows, d_model), F32),
                 jax.ShapeDtypeStruct((halo, pw), F32),
                 jax.ShapeDtypeStruct((nb, nst2), F32))
    out_specs = (y_spec,
                 pl.BlockSpec((halo, pw), lambda i: (0, 0)),
                 pl.BlockSpec((nb, nst2), lambda i: (0, 0)))
    zero_tail = not aliased
    assert aliased or row0 + (n_steps + 1) * rows == total_rows
    kern = functools.partial(_mix_kernel, n_steps=n_steps, zero_tail=zero_tail,
                             nb=nb, tt=tt, start_pos=start_pos, alpha=alpha, aliased=aliased)
    return pl.pallas_call(
        kern, grid=(n_steps + int(zero_tail),), in_specs=in_specs, out_specs=out_specs, out_shape=out_shape,
        scratch_shapes=[pltpu.VMEM((halo + rows, pw), F32), pltpu.VMEM((rows, nst2), F32),
                        pltpu.VMEM((nb, nst2), F32)],
        input_output_aliases={len(args) - 1: 0} if aliased else {},
        name="token_mix_decode" if aliased else "token_mix_prompt",
        compiler_params=_params("arbitrary"))(*args)


def _ffn_kernel(x_ref, wg_ref, wu_ref, wd_ref, g_ref, b_ref, o_ref, *, alpha):
    x = x_ref[...]
    ffn = _swiglu_split(x.astype(BF16), wg_ref, wu_ref, wd_ref)
    o_ref[...] = _layer_norm(alpha * x + ffn, g_ref[...], b_ref[...])


def _layer_spec(a, layer, **kw):
    return pl.BlockSpec((None,) + a.shape[1:], lambda *_, n=a.ndim - 1: (layer,) + (0,) * n, **kw)


def _dense_ffn(x_all, layer, wg, wu, wd, ln_g, ln_b, alpha):
    n, d_model = x_all.shape
    resident = lambda a: _layer_spec(a, layer, pipeline_mode=pl.Buffered(1))
    return pl.pallas_call(
        functools.partial(_ffn_kernel, alpha=alpha),
        grid=(n // ROW_TILE,),
        in_specs=[pl.BlockSpec((ROW_TILE, d_model), lambda i: (i, 0)),
                  resident(wg), resident(wu), resident(wd),
                  pl.BlockSpec((1, d_model), lambda i: (0, 0)), pl.BlockSpec((1, d_model), lambda i: (0, 0))],
        out_specs=pl.BlockSpec((ROW_TILE, d_model), lambda i: (i, 0)),
        out_shape=jax.ShapeDtypeStruct((n, d_model), F32),
        name="dense_swiglu_ln",
        compiler_params=_params("parallel"))(x_all, wg, wu, wd, ln_g, ln_b)


def _split_bf16(a):
    hi = a.astype(BF16)
    lo = (a - hi.astype(F32)).astype(BF16)
    return hi, lo


DEAD_RANK = -(1 << 20)


def _router_kernel(x_ref, w_ref, b_ref, idx_ref, gate_ref, cnt_ref, *, n_tokens):
    x_hi, x_lo = _split_bf16(x_ref[...])
    w_hi, w_lo = _split_bf16(w_ref[...])
    logits = _dot(x_hi, w_hi) + (_dot(x_lo, w_hi) + _dot(x_hi, w_lo)) + b_ref[...]
    lane = lax.broadcasted_iota(jnp.int32, logits.shape, 1)
    logits = jnp.where(lane < N_EXPERTS, logits, -jnp.inf)
    m1 = jnp.max(logits, axis=-1, keepdims=True)
    i1 = jnp.min(jnp.where(logits == m1, lane, LANES), axis=-1, keepdims=True)
    rest = jnp.where(lane == i1, -jnp.inf, logits)
    m2 = jnp.max(rest, axis=-1, keepdims=True)
    i2 = jnp.min(jnp.where(rest == m2, lane, LANES), axis=-1, keepdims=True)
    e2 = jnp.exp(m2 - m1)
    den = 1.0 + e2
    gate_ref[...] = jnp.where(lane == 0, 1.0 / den, e2 / den)

    row = lax.broadcasted_iota(jnp.int32, (ROW_TILE, 1), 0) + pl.program_id(0) * ROW_TILE
    live = row < n_tokens
    pick1 = jnp.where(jnp.logical_and(lane == i1, live), 1.0, 0.0)
    pick2 = jnp.where(jnp.logical_and(lane == i2, live), 1.0, 0.0)
    picks = pick1 + pick2
    r = lax.broadcasted_iota(jnp.int32, (ROW_TILE, ROW_TILE), 0)
    c = lax.broadcasted_iota(jnp.int32, (ROW_TILE, ROW_TILE), 1)
    earlier = _dot(jnp.where(c < r, 1.0, 0.0).astype(BF16), picks.astype(BF16))
    rank1 = jnp.sum(earlier * pick1, axis=-1, keepdims=True).astype(jnp.int32)
    rank2 = jnp.sum(earlier * pick2, axis=-1, keepdims=True).astype(jnp.int32)
    rank1 = jnp.where(live, rank1, DEAD_RANK)
    rank2 = jnp.where(live, rank2, DEAD_RANK)
    idx_ref[...] = jnp.where(lane == 0, i1, jnp.where(lane == 1, i2, jnp.where(lane == 2, rank1, rank2)))
    counts = jnp.sum(picks, axis=0, keepdims=True).astype(jnp.int32)
    cnt_ref[...] = jnp.broadcast_to(counts, cnt_ref.shape)


def _router(x_all, n_tokens, w_router, b_router):
    n, d_model = x_all.shape
    w = jnp.pad(w_router, ((0, 0), (0, LANES - N_EXPERTS)))
    b = jnp.pad(b_router, (0, LANES - N_EXPERTS))[None, :]
    n_tiles = n // ROW_TILE
    return pl.pallas_call(
        functools.partial(_router_kernel, n_tokens=n_tokens), grid=(n_tiles,),
        in_specs=[pl.BlockSpec((ROW_TILE, d_model), lambda i: (i, 0)),
                  pl.BlockSpec((d_model, LANES), lambda i: (0, 0)),
                  pl.BlockSpec((1, LANES), lambda i: (0, 0))],
        out_specs=(pl.BlockSpec((ROW_TILE, LANES), lambda i: (i, 0)),
                   pl.BlockSpec((ROW_TILE, LANES), lambda i: (i, 0)),
                   pl.BlockSpec((8, LANES), lambda i: (i, 0))),
        out_shape=(jax.ShapeDtypeStruct((n, LANES), jnp.int32), jax.ShapeDtypeStruct((n, LANES), F32),
                   jax.ShapeDtypeStruct((n_tiles * 8, LANES), jnp.int32)),
        name="moe_router",
        compiler_params=_params("parallel"))(x_all, w, b)


def _group_copy(src_ref, src_row, dst_ref, dst_row, sem):
    return pltpu.make_async_copy(src_ref.at[pl.ds(pl.multiple_of(src_row, GROUP), GROUP)],
                                 dst_ref.at[pl.ds(pl.multiple_of(dst_row, GROUP), GROUP)], sem)


def _wait_groups(count, src_ref, dst_ref, sem):
    def wait(_, carry):
        _group_copy(src_ref, 0, dst_ref, 0, sem).wait()
        return carry
    lax.fori_loop(0, count, wait, 0)


def _stage_rows(expert, rank, first_ref, step):
    pos = rank
    for e in range(N_EXPERTS):
        pos = pos + jnp.where(expert == e, first_ref[step * N_EXPERTS + e], 0)
    return pos


def _regroup_kernel(w0_ref, off_ref, ngrp_ref, part_ref, pend_ref, first_ref, x_ref, idx_ref, xs_hbm,
                    stage_ref, carry_ref, zero_ref, sem):
    step = pl.program_id(0)

    @pl.when(step == 0)
    def _():
        carry_ref[...] = jnp.zeros(carry_ref.shape, BF16)
        zero_ref[...] = jnp.zeros(zero_ref.shape, BF16)

    xb = x_ref[...].astype(BF16)
    stage_row = lax.broadcasted_iota(jnp.int32, (STAGE_ROWS, ROW_TILE), 0)
    idx = idx_ref[...]
    hit = None
    for choice in range(TOP_K):
        pos = _stage_rows(idx[choice:choice + 1, :], idx[TOP_K + choice:TOP_K + choice + 1, :], first_ref, step)
        hit = stage_row == pos if hit is None else jnp.logical_or(hit, stage_row == pos)
    stage_ref[...] = _dot(jnp.where(hit, 1.0, 0.0).astype(BF16), xb).astype(BF16)

    started = 0
    for e in range(N_EXPERTS):
        k = step * N_EXPERTS + e
        off, w0, ngrp = off_ref[k], w0_ref[k], ngrp_ref[k]
        head = pl.ds(pl.multiple_of(off, GROUP), GROUP)
        stage_ref[head, :] = stage_ref[head, :] + carry_ref[e]

        def send(q, carry, off=off, w0=w0):
            _group_copy(stage_ref, off + q * GROUP, xs_hbm, w0 + q * GROUP, sem).start()
            return carry

        lax.fori_loop(0, ngrp, send, 0)
        tail = stage_ref[pl.ds(pl.multiple_of(off + ngrp * GROUP, GROUP), GROUP), :]
        carry_ref[e] = jnp.where(part_ref[k] != 0, tail, jnp.zeros_like(tail))
        started = started + ngrp
    _wait_groups(started, stage_ref, xs_hbm, sem)

    @pl.when(step == pl.num_programs(0) - 1)
    def _():
        flushed = 0
        for e in range(N_EXPERTS):
            k = step * N_EXPERTS + e
            row = w0_ref[k] + ngrp_ref[k] * GROUP
            is_open = part_ref[k]

            @pl.when(is_open != 0)
            def _(row=row, e=e):
                pltpu.make_async_copy(carry_ref.at[e], xs_hbm.at[pl.ds(pl.multiple_of(row, GROUP), GROUP)],
                                      sem).start()

            row = row + is_open * GROUP
            n_zero = (pend_ref[e] - row) // GROUP

            def fill(q, carry, row=row):
                _group_copy(zero_ref, 0, xs_hbm, row + q * GROUP, sem).start()
                return carry

            lax.fori_loop(0, n_zero, fill, 0)
            flushed = flushed + is_open + n_zero
        _wait_groups(flushed, zero_ref, xs_hbm, sem)


def _regroup(x_all, idx_rows, tables, n_slots):
    n, d_model = x_all.shape
    return pl.pallas_call(
        _regroup_kernel,
        grid_spec=pltpu.PrefetchScalarGridSpec(
            num_scalar_prefetch=6, grid=(n // ROW_TILE,),
            in_specs=[pl.BlockSpec((ROW_TILE, d_model), lambda i, *_: (i, 0)),
                      pl.BlockSpec((2 * TOP_K, ROW_TILE), lambda i, *_: (0, i))],
            out_specs=pl.BlockSpec(memory_space=pl.ANY),
            scratch_shapes=[pltpu.VMEM((STAGE_ROWS, d_model), BF16),
                            pltpu.VMEM((N_EXPERTS, GROUP, d_model), BF16),
                            pltpu.VMEM((GROUP, d_model), BF16),
                            pltpu.SemaphoreType.DMA(())]),
        out_shape=jax.ShapeDtypeStruct((n_slots, d_model), BF16),
        name="moe_regroup",
        compiler_params=_params("arbitrary"))(
            tables["w0"], tables["off"], tables["ngrp"], tables["part"], tables["pend"], tables["first"],
            x_all, idx_rows)


def _moe_kernel(item_e_ref, item_b0_ref, item_nb_ref, nvalid_ref, *refs):
    xs_refs = refs[:ITEM_BLOCKS]
    wg_ref, wu_ref, wd_ref, ys_hbm = refs[ITEM_BLOCKS:ITEM_BLOCKS + 4]
    *wgub_refs, wdb_ref, acc_ref, stage_ref, sem = refs[ITEM_BLOCKS + 4:]
    item, c = pl.program_id(0), pl.program_id(1)
    last_item, last_c = pl.num_programs(0) - 1, pl.num_programs(1) - 1
    n_blocks = ys_hbm.shape[0] // MOE_ROWS
    nb, b0 = item_nb_ref[item], item_b0_ref[item]

    def block_out(src_ref, block):
        rows = pl.ds(pl.multiple_of(block * MOE_ROWS, MOE_ROWS), MOE_ROWS)
        return pltpu.make_async_copy(src_ref, ys_hbm.at[rows], sem)

    def start_and_wait(count, copy_of):
        def start(q, carry):
            copy_of(q).start()
            return carry

        def wait(q, carry):
            copy_of(q).wait()
            return carry

        lax.fori_loop(0, count, start, 0)
        lax.fori_loop(0, count, wait, 0)

    @pl.when(jnp.logical_and(item == 0, c == 0))
    def _():
        acc_ref[...] = jnp.zeros(acc_ref.shape, F32)

    def cast_then_load(p):
        cols = slice(p * LANES, (p + 1) * LANES)
        wgub_refs[p][:, :LANES] = wg_ref[:, cols].astype(BF16)
        wgub_refs[p][:, LANES:] = wu_ref[:, cols].astype(BF16)
        return wgub_refs[p][...]

    for j in range(ITEM_BLOCKS):
        @pl.when(j < nb)
        def _(j=j):
            if j == 0:
                wdb_ref[...] = wd_ref[...].astype(BF16)
            piece = cast_then_load if j == 0 else (lambda p: wgub_refs[p][...])
            part = _swiglu_interleaved(xs_refs[j][...], piece, wdb_ref)
            total = jnp.where(c == 0, part, acc_ref[j] + part)
            acc_ref[j] = total
            stage_ref[j] = total.astype(BF16)

    @pl.when(c == last_c)
    def _():
        start_and_wait(nb, lambda q: block_out(stage_ref.at[q], b0 + q))

    @pl.when(jnp.logical_and(item == last_item, c == last_c))
    def _():
        stage_ref[0] = jnp.zeros(stage_ref.shape[1:], BF16)
        first_unused = nvalid_ref[0]
        start_and_wait(n_blocks - first_unused, lambda q: block_out(stage_ref.at[0], first_unused + q))


def _grouped_swiglu(xs, items, nvalid, layer, wg, wu, wd):
    n_slots, d_model = xs.shape
    d_ff = wg.shape[3]
    fc = MOE_FF_CHUNK
    n_chunks = d_ff // fc
    n_blocks = n_slots // MOE_ROWS
    item_e, item_b0, item_nb = items
    assert d_ff % fc == 0 and n_slots % MOE_ROWS == 0
    chunk = lambda c, nb, it: jnp.where(nb[it] > 0, c, n_chunks - 1)
    w_in_map = lambda it, c, e, b0, nb, nv: (layer, e[it], 0, chunk(c, nb, it))
    w_out_map = lambda it, c, e, b0, nb, nv: (layer, e[it], chunk(c, nb, it), 0)
    xs_spec = lambda k: pl.BlockSpec(
        (MOE_ROWS, d_model), lambda it, c, e, b0, nb, nv: (jnp.minimum(b0[it] + k, n_blocks - 1), 0))
    return pl.pallas_call(
        _moe_kernel,
        grid_spec=pltpu.PrefetchScalarGridSpec(
            num_scalar_prefetch=4, grid=(item_e.shape[0], n_chunks),
            in_specs=[xs_spec(k) for k in range(ITEM_BLOCKS)] + [
                pl.BlockSpec((None, None, d_model, fc), w_in_map),
                pl.BlockSpec((None, None, d_model, fc), w_in_map),
                pl.BlockSpec((None, None, fc, d_model), w_out_map)],
            out_specs=pl.BlockSpec(memory_space=pl.ANY),
            scratch_shapes=[pltpu.VMEM((d_model, 2 * LANES), BF16) for _ in range(fc // LANES)] + [
                            pltpu.VMEM((fc, d_model), BF16),
                            pltpu.VMEM((ITEM_BLOCKS, MOE_ROWS, d_model), F32),
                            pltpu.VMEM((ITEM_BLOCKS, MOE_ROWS, d_model), BF16),
                            pltpu.SemaphoreType.DMA(())]),
        out_shape=jax.ShapeDtypeStruct((n_slots, d_model), BF16),
        name="moe_grouped_swiglu",
        compiler_params=_params("arbitrary", "arbitrary"))(
            item_e, item_b0, item_nb, nvalid, *([xs] * ITEM_BLOCKS), wg, wu, wd)


def _combine_kernel(w0_ref, off_ref, nwin_ref, first_ref, x_ref, idx_ref, gate_ref, g_ref, b_ref, ys_hbm,
                    *rest, alpha, batch_major):
    if batch_major:
        prompt_ref, tail_ref, stage_ref, sem = rest
    else:
        o_ref, stage_ref, sem = rest
    step = pl.program_id(0)

    def fetch_tile(tile, buf):
        stage_ref[buf] = jnp.zeros(stage_ref.shape[1:], BF16)
        for e in range(N_EXPERTS):
            k = tile * N_EXPERTS + e
            off, w0 = off_ref[k], w0_ref[k]

            def fetch(q, carry, off=off, w0=w0):
                _group_copy(ys_hbm, w0 + q * GROUP, stage_ref.at[buf], off + q * GROUP, sem.at[buf]).start()
                return carry

            lax.fori_loop(0, nwin_ref[k], fetch, 0)

    @pl.when(step == 0)
    def _():
        fetch_tile(step, 0)

    buf = lax.rem(step, 2)

    @pl.when(step + 1 < pl.num_programs(0))
    def _():
        fetch_tile(step + 1, 1 - buf)

    n_groups = 0
    for e in range(N_EXPERTS):
        n_groups = n_groups + nwin_ref[step * N_EXPERTS + e]
    _wait_groups(n_groups, ys_hbm, stage_ref.at[buf], sem.at[buf])

    stage_col = lax.broadcasted_iota(jnp.int32, (ROW_TILE, STAGE_ROWS), 1)
    idx = idx_ref[...]
    gates = gate_ref[...]
    ys = stage_ref[buf]
    f = None
    for choice in range(TOP_K):
        pos = _stage_rows(idx[:, choice:choice + 1], idx[:, TOP_K + choice:TOP_K + choice + 1], first_ref, step)
        pick = jnp.where(stage_col == pos, 1.0, 0.0).astype(BF16)
        term = gates[:, choice:choice + 1] * _dot(pick, ys)
        f = term if f is None else f + term
    out = _layer_norm(alpha * x_ref[...] + f, g_ref[...], b_ref[...])
    if not batch_major:
        o_ref[...] = out
        return
    nb, tt = prompt_ref.shape[0], prompt_ref.shape[1]
    last = pl.num_programs(0) - 1

    @pl.when(step < last)
    def _():
        for t in range(tt):
            prompt_ref[:, t, :] = out[t * nb:(t + 1) * nb, :]

    @pl.when(step == last)
    def _():
        tail_ref[...] = out


def _combine(x_all, idx, gates, tables, ys, ln_g, ln_b, alpha, prompt_shape=None):
    n, d_model = x_all.shape
    n_tiles = n // ROW_TILE
    if prompt_shape is None:
        out_specs = pl.BlockSpec((ROW_TILE, d_model), lambda i, *_: (i, 0))
        out_shape = jax.ShapeDtypeStruct((n, d_model), F32)
    else:
        batch, seq = prompt_shape
        tt = ROW_TILE // batch
        assert batch * seq == (n_tiles - 1) * ROW_TILE
        out_specs = (pl.BlockSpec((batch, tt, d_model), lambda i, *_: (0, jnp.minimum(i, n_tiles - 2), 0)),
                     pl.BlockSpec((ROW_TILE, d_model), lambda i, *_: (0, 0)))
        out_shape = (jax.ShapeDtypeStruct((batch, seq, d_model), F32),
                     jax.ShapeDtypeStruct((ROW_TILE, d_model), F32))
    return pl.pallas_call(
        functools.partial(_combine_kernel, alpha=alpha, batch_major=prompt_shape is not None),
        grid_spec=pltpu.PrefetchScalarGridSpec(
            num_scalar_prefetch=4, grid=(n_tiles,),
            in_specs=[pl.BlockSpec((ROW_TILE, d_model), lambda i, *_: (i, 0)),
                      pl.BlockSpec((ROW_TILE, LANES), lambda i, *_: (i, 0)),
                      pl.BlockSpec((ROW_TILE, LANES), lambda i, *_: (i, 0)),
                      pl.BlockSpec((1, d_model), lambda i, *_: (0, 0)),
                      pl.BlockSpec((1, d_model), lambda i, *_: (0, 0)),
                      pl.BlockSpec(memory_space=pl.ANY)],
            out_specs=out_specs,
            scratch_shapes=[pltpu.VMEM((2, STAGE_ROWS, d_model), BF16), pltpu.SemaphoreType.DMA((2,))]),
        out_shape=out_shape,
        name="moe_combine_ln",
        compiler_params=_params("arbitrary"))(
            tables["w0"], tables["off"], tables["nwin"], tables["first"], x_all, idx, gates, ln_g, ln_b, ys)


def _routing_tables(counts, n_slots):
    count = jnp.sum(counts, axis=0)
    blocks = (count + MOE_ROWS - 1) // MOE_ROWS
    blk_end = jnp.cumsum(blocks)
    base = (blk_end - blocks) * MOE_ROWS
    n_items = n_slots // MOE_ROWS // ITEM_BLOCKS + N_EXPERTS
    items_e = (blocks + ITEM_BLOCKS - 1) // ITEM_BLOCKS
    item_end = jnp.cumsum(items_e)
    n_used = item_end[-1]
    item = jnp.arange(n_items, dtype=jnp.int32)
    used_item = jnp.minimum(item, n_used - 1)
    item_e = jnp.minimum(jnp.sum((used_item[:, None] >= item_end[None, :]).astype(jnp.int32), axis=1),
                         N_EXPERTS - 1)
    of_item = lambda per_expert: jnp.sum(
        jnp.where(item_e[:, None] == jnp.arange(N_EXPERTS)[None, :], per_expert[None, :], 0), axis=1)
    local = used_item - of_item(item_end - items_e)
    item_b0 = of_item(blk_end - blocks) + ITEM_BLOCKS * local
    item_nb = jnp.where(item < n_used, jnp.clip(of_item(blocks) - ITEM_BLOCKS * local, 0, ITEM_BLOCKS), 0)
    items = tuple(a.astype(jnp.int32) for a in (item_e, item_b0, item_nb))
    start = base[None, :] + jnp.cumsum(counts, axis=0) - counts
    end = start + counts
    w0 = start // GROUP * GROUP
    ngrp = end // GROUP - start // GROUP
    part = (end % GROUP != 0).astype(jnp.int32)
    nwin = ngrp + part
    off = (jnp.cumsum(nwin, axis=1) - nwin) * GROUP
    nvalid = blk_end[-1]
    pend = (blk_end * MOE_ROWS).at[-1].set(n_slots)
    flat = lambda a: a.reshape(-1).astype(jnp.int32)
    tables = dict(w0=flat(w0), off=flat(off), first=flat(start - w0 + off), ngrp=flat(ngrp), part=flat(part),
                  nwin=flat(nwin), pend=flat(pend))
    return tables, items, nvalid.reshape(1).astype(jnp.int32)


def _moe_ffn(x_all, n_tokens, layer, w_router, b_router, wg, wu, wd, ln_g, ln_b, alpha, prompt_shape=None):
    idx, gates, cnt = _router(x_all, n_tokens, w_router, b_router)
    n_slots = TOP_K * n_tokens + N_EXPERTS * MOE_ROWS
    n_slots = -(-n_slots // (MOE_ROWS * ITEM_BLOCKS)) * (MOE_ROWS * ITEM_BLOCKS)
    tables, items, nvalid = _routing_tables(cnt[::8, :N_EXPERTS], n_slots)
    xs = _regroup(x_all, idx[:, :2 * TOP_K].T, tables, n_slots)
    ys = _grouped_swiglu(xs, items, nvalid, layer, wg, wu, wd)
    return _combine(x_all, idx, gates, tables, ys, ln_g, ln_b, alpha, prompt_shape)


def kernel(x_prompt, x_sample, state_pool, state_ssm_re, state_ssm_im, ln1_g, ln1_b, w_in, b_in, w_pool, pool_scale, lambda_re, lambda_im, log_dt, b_re, b_im, c_re, c_im, d_skip, w_glu, b_glu, w_proj_a, w_proj_b, w_out, ln2_g, ln2_b, w_ffn_gate, w_ffn_up, w_ffn_down, w_router, b_router, w_moe_gate, w_moe_up, w_moe_down):
    batch, seq, d_model = x_prompt.shape
    dec_batch = x_sample.shape[0]
    depth = w_in.shape[0]
    n_groups, n_state = lambda_re.shape[1], lambda_re.shape[2]
    chan = b_re.shape[-1]
    nst = n_groups * n_state
    pw = state_pool.shape[-1]
    alpha = (2.0 * depth) ** 0.25
    n_prompt = batch * seq
    n_tokens = n_prompt + dec_batch
    total_rows = -(-n_tokens // ROW_TILE) * ROW_TILE
    assert x_sample.shape[1] == 1 and n_prompt % ROW_TILE == 0 and seq % MIX_STEPS == 0
    assert batch * MIX_STEPS == ROW_TILE and dec_batch <= ROW_TILE

    rows_of = lambda a: a[:, None, :]
    a_re, a_im, bb_re, bb_im = _ssm_params(lambda_re, lambda_im, log_dt, b_re, b_im)
    lam = jnp.concatenate([a_re[::chan].reshape(depth, 1, nst), a_im[::chan].reshape(depth, 1, nst)], axis=2)
    wts = (w_in.astype(BF16), rows_of(b_in), w_pool.astype(BF16), rows_of(pool_scale), lam,
           _block_diag_in(bb_re, depth, n_groups, chan, n_state).astype(BF16),
           _block_diag_in(bb_im, depth, n_groups, chan, n_state).astype(BF16),
           _block_diag_out(c_re, depth, n_groups, chan, n_state).astype(BF16),
           _block_diag_out(c_im, depth, n_groups, chan, n_state).astype(BF16),
           d_skip.reshape(depth, 1, -1), w_glu.astype(BF16), rows_of(b_glu),
           w_proj_a.astype(BF16), w_proj_b.astype(BF16), w_out.astype(BF16),
           rows_of(ln1_g), rows_of(ln1_b))
    ffn_wts = (w_ffn_gate.astype(BF16), w_ffn_up.astype(BF16), w_ffn_down.astype(BF16))
    prev_s = jnp.transpose(state_pool, (0, 2, 1, 3)).reshape(depth, POOL_STATE * dec_batch, pw)
    h0_s = jnp.concatenate([state_ssm_re.reshape(depth, dec_batch, nst),
                            state_ssm_im.reshape(depth, dec_batch, nst)], axis=2)
    zero_prev = jnp.zeros((1, POOL_STATE * batch, pw), F32)
    zero_h = jnp.zeros((1, batch, 2 * nst), F32)

    x_p = x_prompt
    x_s = x_sample.reshape(dec_batch, d_model)
    s_row0 = 0
    outs = [[] for _ in range(6)]
    for l in range(depth):
        y_mix, pool_p, h_p = _token_mix(x_p, 0, 0, seq // MIX_STEPS, batch, MIX_STEPS, 0, alpha, l,
                                        zero_prev, zero_h, wts, total_rows, state_layer=0)
        y_mix, pool_s, h_s = _token_mix(x_s, s_row0, n_prompt, 1, dec_batch, 1, PAST_LEN, alpha, l,
                                        prev_s, h0_s, wts, total_rows, y_buf=y_mix)
        outs[0].append(jnp.transpose(pool_p.reshape(POOL_STATE, batch, pw), (1, 0, 2)))
        outs[1].append(h_p[:, :nst].reshape(batch, n_groups, n_state))
        outs[2].append(h_p[:, nst:].reshape(batch, n_groups, n_state))
        outs[3].append(jnp.transpose(pool_s.reshape(POOL_STATE, dec_batch, pw), (1, 0, 2)))
        outs[4].append(h_s[:, :nst].reshape(dec_batch, n_groups, n_state))
        outs[5].append(h_s[:, nst:].reshape(dec_batch, n_groups, n_state))
        j = l // 2
        if l % 2 == 0:
            y_all = _dense_ffn(y_mix, j, *ffn_wts, ln2_g[l][None, :], ln2_b[l][None, :], alpha)
        else:
            prompt_shape = (batch, seq) if l == depth - 1 else None
            y_all = _moe_ffn(y_mix, n_tokens, j, w_router[j], b_router[j], w_moe_gate, w_moe_up,
                             w_moe_down, ln2_g[l][None, :], ln2_b[l][None, :], alpha, prompt_shape)
        x_p = x_s = y_all
        s_row0 = n_prompt
    if isinstance(y_all, (tuple, list)):
        y_p, tail = y_all
        y_s = tail[:dec_batch].reshape(dec_batch, 1, d_model)
    else:
        y_p = jnp.transpose(y_all[:n_prompt].reshape(seq, batch, d_model), (1, 0, 2))
        y_s = y_all[n_prompt:n_tokens].reshape(dec_batch, 1, d_model)
    return (y_p, y_s, *[jnp.stack(o) for o in outs])
```

```python
import functools
import math

import jax
import jax.numpy as jnp
from jax import lax
from jax.experimental import pallas as pl
from jax.experimental.pallas import tpu as pltpu

F32 = jnp.float32
BF16 = jnp.bfloat16

POOL_WINDOWS = (2, 4, 8, 16)
POOL_STATE = max(POOL_WINDOWS) - 1
POOL_GROUP = 128
SSM_GROUP = 16
SSM_STATE = 64
N_EXPERTS = 8
TOP_K = 2
LN_EPS = 1e-5
PAST_LEN = 16384

LANES = 128
MXU_COLS = 256
ROW_TILE = 512
MIX_STEPS = 64
SCAN_COLS = 512
MOE_ROWS = 512
ITEM_BLOCKS = 5
MOE_FF_CHUNK = 896
GROUP = 16
STAGE_ROWS = TOP_K * ROW_TILE + N_EXPERTS * 2 * GROUP
VMEM_LIMIT = 56 * 1024 * 1024


def _dot(a, b):
    return jnp.dot(a, b, preferred_element_type=F32)


def _layer_norm(r, g, b):
    mu = jnp.mean(r, axis=-1, keepdims=True)
    d = r - mu
    var = jnp.mean(d * d, axis=-1, keepdims=True)
    return d * lax.rsqrt(var + LN_EPS) * g + b


def _swiglu(xb, gate_up_piece, n_pieces, wd_ref):
    pieces = []
    for p in range(n_pieces):
        gate, up = gate_up_piece(p)
        pieces.append((jax.nn.silu(gate) * up).astype(BF16))
    return _dot(jnp.concatenate(pieces, axis=-1), wd_ref[...])


def _swiglu_split(xb, wg_ref, wu_ref, wd_ref):
    def piece(p):
        cols = slice(p * MXU_COLS, (p + 1) * MXU_COLS)
        return _dot(xb, wg_ref[:, cols]), _dot(xb, wu_ref[:, cols])
    return _swiglu(xb, piece, wd_ref.shape[0] // MXU_COLS, wd_ref)


def _swiglu_interleaved(xb, wgu_piece, wd_ref):
    def piece(p):
        gate_up = _dot(xb, wgu_piece(p))
        return gate_up[:, :LANES], gate_up[:, LANES:]
    return _swiglu(xb, piece, wd_ref.shape[0] // LANES, wd_ref)


def _params(*sem):
    return pltpu.CompilerParams(dimension_semantics=sem, vmem_limit_bytes=VMEM_LIMIT)


def _ssm_param_kernel(lre_ref, lim_ref, ldt_ref, bre_ref, bim_ref, are_ref, aim_ref, bbre_ref, bbim_ref):
    lre, lim = lre_ref[...], lim_ref[...]
    dt = jnp.exp(ldt_ref[...])
    mag = jnp.exp(lre * dt)
    a_re = mag * jnp.cos(lim * dt)
    a_im = mag * jnp.sin(lim * dt)
    den = lre * lre + lim * lim
    n_re, n_im = a_re - 1.0, a_im
    k_re = (n_re * lre + n_im * lim) / den
    k_im = (n_im * lre - n_re * lim) / den
    b_re, b_im = bre_ref[...], bim_ref[...]
    are_ref[...] = a_re
    aim_ref[...] = a_im
    bbre_ref[...] = k_re * b_re - k_im * b_im
    bbim_ref[...] = k_re * b_im + k_im * b_re


def _ssm_params(lambda_re, lambda_im, log_dt, b_re, b_im):
    depth, g, p = lambda_re.shape
    c = b_re.shape[-1]
    rep = lambda a: jnp.repeat(a.reshape(depth * g, p), c, axis=0)
    bt = lambda a: jnp.transpose(a, (0, 1, 3, 2)).reshape(depth * g * c, p)
    shape = jax.ShapeDtypeStruct((depth * g * c, p), F32)
    return pl.pallas_call(_ssm_param_kernel, out_shape=(shape,) * 4, name="s5_discretise")(
        rep(lambda_re), rep(lambda_im), rep(jnp.broadcast_to(log_dt[:, :, None], (depth, g, p))),
        bt(b_re), bt(b_im))


def _block_diag_in(bt, depth, g, c, p):
    m = g // 8
    a = bt.reshape(depth, m, 8, c, p)
    eye = jnp.eye(8, dtype=bt.dtype)
    return jnp.einsum('lmgcp,gh->lmgchp', a, eye).reshape(depth, m, 8 * c, 8 * p)


def _block_diag_out(cm, depth, g, c, p):
    m = g // 8
    a = jnp.transpose(cm, (0, 1, 3, 2)).reshape(depth, m, 8, p, c)
    eye = jnp.eye(8, dtype=cm.dtype)
    return jnp.einsum('lmgpc,gh->lmgphc', a, eye).reshape(depth, m, 8 * p, 8 * c)


def _mix_kernel(*refs, n_steps, zero_tail, **kw):
    if not zero_tail:
        _mix_body(*refs, **kw)
        return
    step = pl.program_id(0)
    y_ref = refs[-6]

    @pl.when(step < n_steps)
    def _():
        _mix_body(*refs, **kw)

    @pl.when(step >= n_steps)
    def _():
        y_ref[...] = jnp.zeros(y_ref.shape, F32)


def _mix_body(x_ref, prev_ref, h0_ref, win_ref, bin_ref, wpool_ref, pscale_ref, lam_ref,
              bre_ref, bim_ref, cre_ref, cim_ref, dskip_ref, wglu_ref, bglu_ref,
              wpa_ref, wpb_ref, wout_ref, lng_ref, lnb_ref, *rest,
              nb, tt, start_pos, alpha, aliased):
    if aliased:
        rest = rest[1:]
    y_ref, pool_ref, hlast_ref, ext_ref, bu_ref, hs_ref = rest
    rows = nb * tt
    halo = POOL_STATE * nb
    d_model = x_ref.shape[-1]
    pw = pscale_ref.shape[1]
    nst = lam_ref.shape[1] // 2
    step = pl.program_id(0)

    @pl.when(step == 0)
    def _():
        ext_ref[0:halo, :] = prev_ref[...]
        hs_ref[...] = h0_ref[...]

    if x_ref.ndim == 3:
        x = jnp.concatenate([x_ref[:, t, :] for t in range(tt)], axis=0)
    else:
        x = x_ref[...]
    xb = x.astype(BF16)

    u_a = _dot(xb, win_ref[:, 0:pw]) + bin_ref[:, 0:pw]
    ext_ref[halo:halo + rows, :] = u_a
    t_idx = lax.shift_right_logical(lax.broadcasted_iota(jnp.int32, (rows, 1), 0), int(math.log2(nb)))
    pos1 = t_idx + (step * tt + start_pos + 1)
    mixed = []
    for gi, w in enumerate(POOL_WINDOWS):
        sl = slice(gi * POOL_GROUP, (gi + 1) * POOL_GROUP)
        cur = u_a[:, sl]
        acc = cur
        for k in range(1, w):
            acc = acc + ext_ref[halo - k * nb:halo - k * nb + rows, sl]
        count = jnp.minimum(pos1, w).astype(F32)
        pooled = acc / count - cur
        mixed.append(_dot(pooled.astype(BF16), wpool_ref[gi]))
    out_a = jnp.concatenate(mixed, axis=-1) * pscale_ref[...]
    new_halo = ext_ref[rows:rows + halo, :]
    ext_ref[0:halo, :] = new_halo
    pool_ref[...] = new_halo
    proj_a = _dot(out_a.astype(BF16), wpa_ref[...])

    u_b = _dot(xb, win_ref[:, pw:2 * pw]) + bin_ref[:, pw:2 * pw]
    ub16 = u_b.astype(BF16)
    n_m = bre_ref.shape[0]
    kin, kst = bre_ref.shape[1], bre_ref.shape[2]
    for m in range(n_m):
        um = ub16[:, m * kin:(m + 1) * kin]
        bu_ref[:, m * kst:(m + 1) * kst] = _dot(um, bre_ref[m])
        bu_ref[:, nst + m * kst:nst + (m + 1) * kst] = _dot(um, bim_ref[m])
    if tt == 1:
        a_re, a_im = lam_ref[:, 0:nst], lam_ref[:, nst:]
        h_re, h_im = hs_ref[:, 0:nst], hs_ref[:, nst:]
        n_re = a_re * h_re - a_im * h_im + bu_ref[:, 0:nst]
        n_im = a_re * h_im + a_im * h_re + bu_ref[:, nst:]
        bu_ref[:, 0:nst] = n_re
        bu_ref[:, nst:] = n_im
        hs_ref[:, 0:nst] = n_re
        hs_ref[:, nst:] = n_im
    else:
        for q in range(nst // SCAN_COLS):
            c_re = slice(q * SCAN_COLS, (q + 1) * SCAN_COLS)
            c_im = slice(nst + q * SCAN_COLS, nst + (q + 1) * SCAN_COLS)
            a_re = jnp.broadcast_to(lam_ref[:, c_re], (nb, SCAN_COLS))
            a_im = jnp.broadcast_to(lam_ref[:, c_im], (nb, SCAN_COLS))

            def body(t, carry, c_re=c_re, c_im=c_im, a_re=a_re, a_im=a_im):
                h_re, h_im = carry
                r = pl.ds(pl.multiple_of(t * nb, nb), nb)
                n_re = a_re * h_re - a_im * h_im + bu_ref[r, c_re]
                n_im = a_re * h_im + a_im * h_re + bu_ref[r, c_im]
                bu_ref[r, c_re] = n_re
                bu_ref[r, c_im] = n_im
                return n_re, n_im

            h_re, h_im = lax.fori_loop(0, tt, body, (hs_ref[:, c_re], hs_ref[:, c_im]), unroll=True)
            hs_ref[:, c_re] = h_re
            hs_ref[:, c_im] = h_im
    hlast_ref[...] = hs_ref[...]

    ys = []
    for m in range(n_m):
        h_re = bu_ref[:, m * kst:(m + 1) * kst].astype(BF16)
        h_im = bu_ref[:, nst + m * kst:nst + (m + 1) * kst].astype(BF16)
        ys.append(_dot(h_re, cre_ref[m]) - _dot(h_im, cim_ref[m]))
    y = jnp.concatenate(ys, axis=-1) + dskip_ref[...] * u_b
    y = jax.nn.gelu(y)
    out_b = y * jax.nn.sigmoid(_dot(y.astype(BF16), wglu_ref[...]) + bglu_ref[...])
    proj_b = _dot(out_b.astype(BF16), wpb_ref[...])

    z_ga = _dot(xb, win_ref[:, 2 * pw:2 * pw + d_model]) + bin_ref[:, 2 * pw:2 * pw + d_model]
    merged = jax.nn.sigmoid(z_ga) * proj_a
    z_gb = _dot(xb, win_ref[:, 2 * pw + d_model:]) + bin_ref[:, 2 * pw + d_model:]
    merged = merged + jax.nn.sigmoid(z_gb) * proj_b
    mix = _dot(merged.astype(BF16), wout_ref[...])
    out = _layer_norm(alpha * x + mix, lng_ref[...], lnb_ref[...])
    if y_ref.shape[0] == rows:
        y_ref[...] = out
    else:
        y_ref[0:rows, :] = out
        y_ref[rows:, :] = jnp.zeros((y_ref.shape[0] - rows, d_model), F32)


def _token_mix(x_src, x_row0, row0, n_steps, nb, tt, start_pos, alpha, layer, prev, h0, wts, total_rows,
               y_buf=None, state_layer=None):
    state_layer = layer if state_layer is None else state_layer
    rows = nb * tt
    d_model = x_src.shape[-1]
    nst2 = h0.shape[-1]
    pw = prev.shape[-1]
    halo = POOL_STATE * nb
    aliased = y_buf is not None
    const = lambda a: _layer_spec(a, layer)
    tile = lambda i: jnp.minimum(i, n_steps - 1)
    if x_src.ndim == 3:
        x_spec = pl.BlockSpec((nb, tt, d_model), lambda i: (0, tile(i), 0))
    else:
        x_spec = pl.BlockSpec((rows, d_model), lambda i: (x_row0 // rows + tile(i), 0))
    in_specs = [x_spec, _layer_spec(prev, state_layer), _layer_spec(h0, state_layer)]
    in_specs += [const(w) for w in wts]
    args = [x_src, prev, h0, *wts]
    if aliased:
        in_specs.append(pl.BlockSpec(memory_space=pl.ANY))
        args.append(y_buf)
        y_spec = pl.BlockSpec((ROW_TILE, d_model), lambda i: (row0 // ROW_TILE, 0))
    else:
        y_spec = pl.BlockSpec((rows, d_model), lambda i: (row0 // rows + i, 0))
    out_shape = (jax.ShapeDtypeStruct((total_rows, d_model), F32),
                 jax.ShapeDtypeStruct((halo, pw), F32),
                 jax.ShapeDtypeStruct((nb, nst2), F32))
    out_specs = (y_spec,
                 pl.BlockSpec((halo, pw), lambda i: (0, 0)),
                 pl.BlockSpec((nb, nst2), lambda i: (0, 0)))
    zero_tail = not aliased
    assert aliased or row0 + (n_steps + 1) * rows == total_rows
    kern = functools.partial(_mix_kernel, n_steps=n_steps, zero_tail=zero_tail,
                             nb=nb, tt=tt, start_pos=start_pos, alpha=alpha, aliased=aliased)
    return pl.pallas_call(
        kern, grid=(n_steps + int(zero_tail),), in_specs=in_specs, out_specs=out_specs, out_shape=out_shape,
        scratch_shapes=[pltpu.VMEM((halo + rows, pw), F32), pltpu.VMEM((rows, nst2), F32),
                        pltpu.VMEM((nb, nst2), F32)],
        input_output_aliases={len(args) - 1: 0} if aliased else {},
        name="token_mix_decode" if aliased else "token_mix_prompt",
        compiler_params=_params("arbitrary"))(*args)


def _ffn_kernel(x_ref, wg_ref, wu_ref, wd_ref, g_ref, b_ref, o_ref, *, alpha):
    x = x_ref[...]
    ffn = _swiglu_split(x.astype(BF16), wg_ref, wu_ref, wd_ref)
    o_ref[...] = _layer_norm(alpha * x + ffn, g_ref[...], b_ref[...])


def _layer_spec(a, layer, **kw):
    return pl.BlockSpec((None,) + a.shape[1:], lambda *_, n=a.ndim - 1: (layer,) + (0,) * n, **kw)


def _dense_ffn(x_all, layer, wg, wu, wd, ln_g, ln_b, alpha):
    n, d_model = x_all.shape
    resident = lambda a: _layer_spec(a, layer, pipeline_mode=pl.Buffered(1))
    return pl.pallas_call(
        functools.partial(_ffn_kernel, alpha=alpha),
        grid=(n // ROW_TILE,),
        in_specs=[pl.BlockSpec((ROW_TILE, d_model), lambda i: (i, 0)),
                  resident(wg), resident(wu), resident(wd),
                  pl.BlockSpec((1, d_model), lambda i: (0, 0)), pl.BlockSpec((1, d_model), lambda i: (0, 0))],
        out_specs=pl.BlockSpec((ROW_TILE, d_model), lambda i: (i, 0)),
        out_shape=jax.ShapeDtypeStruct((n, d_model), F32),
        name="dense_swiglu_ln",
        compiler_params=_params("parallel"))(x_all, wg, wu, wd, ln_g, ln_b)


def _split_bf16(a):
    hi = a.astype(BF16)
    lo = (a - hi.astype(F32)).astype(BF16)
    return hi, lo


DEAD_RANK = -(1 << 20)


def _router_kernel(x_ref, w_ref, b_ref, idx_ref, gate_ref, cnt_ref, *, n_tokens):
    x_hi, x_lo = _split_bf16(x_ref[...])
    w_hi, w_lo = _split_bf16(w_ref[...])
    logits = _dot(x_hi, w_hi) + (_dot(x_lo, w_hi) + _dot(x_hi, w_lo)) + b_ref[...]
    lane = lax.broadcasted_iota(jnp.int32, logits.shape, 1)
    logits = jnp.where(lane < N_EXPERTS, logits, -jnp.inf)
    m1 = jnp.max(logits, axis=-1, keepdims=True)
    i1 = jnp.min(jnp.where(logits == m1, lane, LANES), axis=-1, keepdims=True)
    rest = jnp.where(lane == i1, -jnp.inf, logits)
    m2 = jnp.max(rest, axis=-1, keepdims=True)
    i2 = jnp.min(jnp.where(rest == m2, lane, LANES), axis=-1, keepdims=True)
    e2 = jnp.exp(m2 - m1)
    den = 1.0 + e2
    gate_ref[...] = jnp.where(lane == 0, 1.0 / den, e2 / den)

    row = lax.broadcasted_iota(jnp.int32, (ROW_TILE, 1), 0) + pl.program_id(0) * ROW_TILE
    live = row < n_tokens
    pick1 = jnp.where(jnp.logical_and(lane == i1, live), 1.0, 0.0)
    pick2 = jnp.where(jnp.logical_and(lane == i2, live), 1.0, 0.0)
    picks = pick1 + pick2
    r = lax.broadcasted_iota(jnp.int32, (ROW_TILE, ROW_TILE), 0)
    c = lax.broadcasted_iota(jnp.int32, (ROW_TILE, ROW_TILE), 1)
    earlier = _dot(jnp.where(c < r, 1.0, 0.0).astype(BF16), picks.astype(BF16))
    rank1 = jnp.sum(earlier * pick1, axis=-1, keepdims=True).astype(jnp.int32)
    rank2 = jnp.sum(earlier * pick2, axis=-1, keepdims=True).astype(jnp.int32)
    rank1 = jnp.where(live, rank1, DEAD_RANK)
    rank2 = jnp.where(live, rank2, DEAD_RANK)
    idx_ref[...] = jnp.where(lane == 0, i1, jnp.where(lane == 1, i2, jnp.where(lane == 2, rank1, rank2)))
    counts = jnp.sum(picks, axis=0, keepdims=True).astype(jnp.int32)
    cnt_ref[...] = jnp.broadcast_to(counts, cnt_ref.shape)


def _router(x_all, n_tokens, w_router, b_router):
    n, d_model = x_all.shape
    w = jnp.pad(w_router, ((0, 0), (0, LANES - N_EXPERTS)))
    b = jnp.pad(b_router, (0, LANES - N_EXPERTS))[None, :]
    n_tiles = n // ROW_TILE
    return pl.pallas_call(
        functools.partial(_router_kernel, n_tokens=n_tokens), grid=(n_tiles,),
        in_specs=[pl.BlockSpec((ROW_TILE, d_model), lambda i: (i, 0)),
                  pl.BlockSpec((d_model, LANES), lambda i: (0, 0)),
                  pl.BlockSpec((1, LANES), lambda i: (0, 0))],
        out_specs=(pl.BlockSpec((ROW_TILE, LANES), lambda i: (i, 0)),
                   pl.BlockSpec((ROW_TILE, LANES), lambda i: (i, 0)),
                   pl.BlockSpec((8, LANES), lambda i: (i, 0))),
        out_shape=(jax.ShapeDtypeStruct((n, LANES), jnp.int32), jax.ShapeDtypeStruct((n, LANES), F32),
                   jax.ShapeDtypeStruct((n_tiles * 8, LANES), jnp.int32)),
        name="moe_router",
        compiler_params=_params("parallel"))(x_all, w, b)


def _group_copy(src_ref, src_row, dst_ref, dst_row, sem):
    return pltpu.make_async_copy(src_ref.at[pl.ds(pl.multiple_of(src_row, GROUP), GROUP)],
                                 dst_ref.at[pl.ds(pl.multiple_of(dst_row, GROUP), GROUP)], sem)


def _wait_groups(count, src_ref, dst_ref, sem):
    def wait(_, carry):
        _group_copy(src_ref, 0, dst_ref, 0, sem).wait()
        return carry
    lax.fori_loop(0, count, wait, 0)


def _stage_rows(expert, rank, first_ref, step):
    pos = rank
    for e in range(N_EXPERTS):
        pos = pos + jnp.where(expert == e, first_ref[step * N_EXPERTS + e], 0)
    return pos


def _regroup_kernel(w0_ref, off_ref, ngrp_ref, part_ref, pend_ref, first_ref, x_ref, idx_ref, xs_hbm,
                    stage_ref, carry_ref, zero_ref, sem):
    step = pl.program_id(0)

    @pl.when(step == 0)
    def _():
        carry_ref[...] = jnp.zeros(carry_ref.shape, BF16)
        zero_ref[...] = jnp.zeros(zero_ref.shape, BF16)

    xb = x_ref[...].astype(BF16)
    stage_row = lax.broadcasted_iota(jnp.int32, (STAGE_ROWS, ROW_TILE), 0)
    idx = idx_ref[...]
    hit = None
    for choice in range(TOP_K):
        pos = _stage_rows(idx[choice:choice + 1, :], idx[TOP_K + choice:TOP_K + choice + 1, :], first_ref, step)
        hit = stage_row == pos if hit is None else jnp.logical_or(hit, stage_row == pos)
    stage_ref[...] = _dot(jnp.where(hit, 1.0, 0.0).astype(BF16), xb).astype(BF16)

    started = 0
    for e in range(N_EXPERTS):
        k = step * N_EXPERTS + e
        off, w0, ngrp = off_ref[k], w0_ref[k], ngrp_ref[k]
        head = pl.ds(pl.multiple_of(off, GROUP), GROUP)
        stage_ref[head, :] = stage_ref[head, :] + carry_ref[e]

        def send(q, carry, off=off, w0=w0):
            _group_copy(stage_ref, off + q * GROUP, xs_hbm, w0 + q * GROUP, sem).start()
            return carry

        lax.fori_loop(0, ngrp, send, 0)
        tail = stage_ref[pl.ds(pl.multiple_of(off + ngrp * GROUP, GROUP), GROUP), :]
        carry_ref[e] = jnp.where(part_ref[k] != 0, tail, jnp.zeros_like(tail))
        started = started + ngrp
    _wait_groups(started, stage_ref, xs_hbm, sem)

    @pl.when(step == pl.num_programs(0) - 1)
    def _():
        flushed = 0
        for e in range(N_EXPERTS):
            k = step * N_EXPERTS + e
            row = w0_ref[k] + ngrp_ref[k] * GROUP
            is_open = part_ref[k]

            @pl.when(is_open != 0)
            def _(row=row, e=e):
                pltpu.make_async_copy(carry_ref.at[e], xs_hbm.at[pl.ds(pl.multiple_of(row, GROUP), GROUP)],
                                      sem).start()

            row = row + is_open * GROUP
            n_zero = (pend_ref[e] - row) // GROUP

            def fill(q, carry, row=row):
                _group_copy(zero_ref, 0, xs_hbm, row + q * GROUP, sem).start()
                return carry

            lax.fori_loop(0, n_zero, fill, 0)
            flushed = flushed + is_open + n_zero
        _wait_groups(flushed, zero_ref, xs_hbm, sem)


def _regroup(x_all, idx_rows, tables, n_slots):
    n, d_model = x_all.shape
    return pl.pallas_call(
        _regroup_kernel,
        grid_spec=pltpu.PrefetchScalarGridSpec(
            num_scalar_prefetch=6, grid=(n // ROW_TILE,),
            in_specs=[pl.BlockSpec((ROW_TILE, d_model), lambda i, *_: (i, 0)),
                      pl.BlockSpec((2 * TOP_K, ROW_TILE), lambda i, *_: (0, i))],
            out_specs=pl.BlockSpec(memory_space=pl.ANY),
            scratch_shapes=[pltpu.VMEM((STAGE_ROWS, d_model), BF16),
                            pltpu.VMEM((N_EXPERTS, GROUP, d_model), BF16),
                            pltpu.VMEM((GROUP, d_model), BF16),
                            pltpu.SemaphoreType.DMA(())]),
        out_shape=jax.ShapeDtypeStruct((n_slots, d_model), BF16),
        name="moe_regroup",
        compiler_params=_params("arbitrary"))(
            tables["w0"], tables["off"], tables["ngrp"], tables["part"], tables["pend"], tables["first"],
            x_all, idx_rows)


def _moe_kernel(item_e_ref, item_b0_ref, item_nb_ref, item_half_ref, nvalid_ref, *refs):
    xs_refs = refs[:ITEM_BLOCKS]
    wg_ref, wu_ref, wd_ref, ys_hbm = refs[ITEM_BLOCKS:ITEM_BLOCKS + 4]
    *wgub_refs, wdb_ref, acc_ref, stage_ref, sem = refs[ITEM_BLOCKS + 4:]
    item, c = pl.program_id(0), pl.program_id(1)
    last_item, last_c = pl.num_programs(0) - 1, pl.num_programs(1) - 1
    n_blocks = ys_hbm.shape[0] // MOE_ROWS
    nb, b0 = item_nb_ref[item], item_b0_ref[item]

    def block_out(src_ref, block):
        rows = pl.ds(pl.multiple_of(block * MOE_ROWS, MOE_ROWS), MOE_ROWS)
        return pltpu.make_async_copy(src_ref, ys_hbm.at[rows], sem)

    def start_and_wait(count, copy_of):
        def start(q, carry):
            copy_of(q).start()
            return carry

        def wait(q, carry):
            copy_of(q).wait()
            return carry

        lax.fori_loop(0, count, start, 0)
        lax.fori_loop(0, count, wait, 0)

    @pl.when(jnp.logical_and(item == 0, c == 0))
    def _():
        acc_ref[...] = jnp.zeros(acc_ref.shape, F32)

    def cast_then_load(p):
        cols = slice(p * LANES, (p + 1) * LANES)
        wgub_refs[p][:, :LANES] = wg_ref[:, cols].astype(BF16)
        wgub_refs[p][:, LANES:] = wu_ref[:, cols].astype(BF16)
        return wgub_refs[p][...]

    def run_block(j, n_rows):
        if j == 0:
            wdb_ref[...] = wd_ref[...].astype(BF16)
        piece = cast_then_load if j == 0 else (lambda p: wgub_refs[p][...])
        part = _swiglu_interleaved(xs_refs[j][0:n_rows, :], piece, wdb_ref)
        total = jnp.where(c == 0, part, acc_ref[j, 0:n_rows, :] + part)
        acc_ref[j, 0:n_rows, :] = total
        stage_ref[j, 0:n_rows, :] = total.astype(BF16)
        if n_rows < MOE_ROWS:
            stage_ref[j, n_rows:, :] = jnp.zeros((MOE_ROWS - n_rows, stage_ref.shape[2]), BF16)

    for j in range(ITEM_BLOCKS):
        half = jnp.logical_and(j == nb - 1, item_half_ref[item] != 0)
        pl.when(jnp.logical_and(j < nb, jnp.logical_not(half)))(functools.partial(run_block, j, MOE_ROWS))
        pl.when(half)(functools.partial(run_block, j, MOE_ROWS // 2))

    @pl.when(c == last_c)
    def _():
        start_and_wait(nb, lambda q: block_out(stage_ref.at[q], b0 + q))

    @pl.when(jnp.logical_and(item == last_item, c == last_c))
    def _():
        stage_ref[0] = jnp.zeros(stage_ref.shape[1:], BF16)
        first_unused = nvalid_ref[0]
        start_and_wait(n_blocks - first_unused, lambda q: block_out(stage_ref.at[0], first_unused + q))


def _grouped_swiglu(xs, items, nvalid, layer, wg, wu, wd):
    n_slots, d_model = xs.shape
    d_ff = wg.shape[3]
    fc = MOE_FF_CHUNK
    n_chunks = d_ff // fc
    n_blocks = n_slots // MOE_ROWS
    item_e, item_b0, item_nb, item_half = items
    assert d_ff % fc == 0 and n_slots % MOE_ROWS == 0
    chunk = lambda c, nb, it: jnp.where(nb[it] > 0, c, n_chunks - 1)
    w_in_map = lambda it, c, e, b0, nb, *_: (layer, e[it], 0, chunk(c, nb, it))
    w_out_map = lambda it, c, e, b0, nb, *_: (layer, e[it], chunk(c, nb, it), 0)
    xs_spec = lambda k: pl.BlockSpec(
        (MOE_ROWS, d_model), lambda it, c, e, b0, *_: (jnp.minimum(b0[it] + k, n_blocks - 1), 0))
    return pl.pallas_call(
        _moe_kernel,
        grid_spec=pltpu.PrefetchScalarGridSpec(
            num_scalar_prefetch=5, grid=(item_e.shape[0], n_chunks),
            in_specs=[xs_spec(k) for k in range(ITEM_BLOCKS)] + [
                pl.BlockSpec((None, None, d_model, fc), w_in_map),
                pl.BlockSpec((None, None, d_model, fc), w_in_map),
                pl.BlockSpec((None, None, fc, d_model), w_out_map)],
            out_specs=pl.BlockSpec(memory_space=pl.ANY),
            scratch_shapes=[pltpu.VMEM((d_model, 2 * LANES), BF16) for _ in range(fc // LANES)] + [
                            pltpu.VMEM((fc, d_model), BF16),
                            pltpu.VMEM((ITEM_BLOCKS, MOE_ROWS, d_model), F32),
                            pltpu.VMEM((ITEM_BLOCKS, MOE_ROWS, d_model), BF16),
                            pltpu.SemaphoreType.DMA(())]),
        out_shape=jax.ShapeDtypeStruct((n_slots, d_model), BF16),
        name="moe_grouped_swiglu",
        compiler_params=_params("arbitrary", "arbitrary"))(
            item_e, item_b0, item_nb, item_half, nvalid, *([xs] * ITEM_BLOCKS), wg, wu, wd)


def _combine_kernel(w0_ref, off_ref, nwin_ref, first_ref, x_ref, idx_ref, gate_ref, g_ref, b_ref, ys_hbm,
                    *rest, alpha, batch_major):
    if batch_major:
        prompt_ref, tail_ref, stage_ref, sem = rest
    else:
        o_ref, stage_ref, sem = rest
    step = pl.program_id(0)

    def fetch_tile(tile, buf):
        stage_ref[buf] = jnp.zeros(stage_ref.shape[1:], BF16)
        for e in range(N_EXPERTS):
            k = tile * N_EXPERTS + e
            off, w0 = off_ref[k], w0_ref[k]

            def fetch(q, carry, off=off, w0=w0):
                _group_copy(ys_hbm, w0 + q * GROUP, stage_ref.at[buf], off + q * GROUP, sem.at[buf]).start()
                return carry

            lax.fori_loop(0, nwin_ref[k], fetch, 0)

    @pl.when(step == 0)
    def _():
        fetch_tile(step, 0)

    buf = lax.rem(step, 2)

    @pl.when(step + 1 < pl.num_programs(0))
    def _():
        fetch_tile(step + 1, 1 - buf)

    n_groups = 0
    for e in range(N_EXPERTS):
        n_groups = n_groups + nwin_ref[step * N_EXPERTS + e]
    _wait_groups(n_groups, ys_hbm, stage_ref.at[buf], sem.at[buf])

    stage_col = lax.broadcasted_iota(jnp.int32, (ROW_TILE, STAGE_ROWS), 1)
    idx = idx_ref[...]
    gates = gate_ref[...]
    ys = stage_ref[buf]
    f = None
    for choice in range(TOP_K):
        pos = _stage_rows(idx[:, choice:choice + 1], idx[:, TOP_K + choice:TOP_K + choice + 1], first_ref, step)
        pick = jnp.where(stage_col == pos, 1.0, 0.0).astype(BF16)
        term = gates[:, choice:choice + 1] * _dot(pick, ys)
        f = term if f is None else f + term
    out = _layer_norm(alpha * x_ref[...] + f, g_ref[...], b_ref[...])
    if not batch_major:
        o_ref[...] = out
        return
    nb, tt = prompt_ref.shape[0], prompt_ref.shape[1]
    last = pl.num_programs(0) - 1

    @pl.when(step < last)
    def _():
        for t in range(tt):
            prompt_ref[:, t, :] = out[t * nb:(t + 1) * nb, :]

    @pl.when(step == last)
    def _():
        tail_ref[...] = out


def _combine(x_all, idx, gates, tables, ys, ln_g, ln_b, alpha, prompt_shape=None):
    n, d_model = x_all.shape
    n_tiles = n // ROW_TILE
    if prompt_shape is None:
        out_specs = pl.BlockSpec((ROW_TILE, d_model), lambda i, *_: (i, 0))
        out_shape = jax.ShapeDtypeStruct((n, d_model), F32)
    else:
        batch, seq = prompt_shape
        tt = ROW_TILE // batch
        assert batch * seq == (n_tiles - 1) * ROW_TILE
        out_specs = (pl.BlockSpec((batch, tt, d_model), lambda i, *_: (0, jnp.minimum(i, n_tiles - 2), 0)),
                     pl.BlockSpec((ROW_TILE, d_model), lambda i, *_: (0, 0)))
        out_shape = (jax.ShapeDtypeStruct((batch, seq, d_model), F32),
                     jax.ShapeDtypeStruct((ROW_TILE, d_model), F32))
    return pl.pallas_call(
        functools.partial(_combine_kernel, alpha=alpha, batch_major=prompt_shape is not None),
        grid_spec=pltpu.PrefetchScalarGridSpec(
            num_scalar_prefetch=4, grid=(n_tiles,),
            in_specs=[pl.BlockSpec((ROW_TILE, d_model), lambda i, *_: (i, 0)),
                      pl.BlockSpec((ROW_TILE, LANES), lambda i, *_: (i, 0)),
                      pl.BlockSpec((ROW_TILE, LANES), lambda i, *_: (i, 0)),
                      pl.BlockSpec((1, d_model), lambda i, *_: (0, 0)),
                      pl.BlockSpec((1, d_model), lambda i, *_: (0, 0)),
                      pl.BlockSpec(memory_space=pl.ANY)],
            out_specs=out_specs,
            scratch_shapes=[pltpu.VMEM((2, STAGE_ROWS, d_model), BF16), pltpu.SemaphoreType.DMA((2,))]),
        out_shape=out_shape,
        name="moe_combine_ln",
        compiler_params=_params("arbitrary"))(
            tables["w0"], tables["off"], tables["nwin"], tables["first"], x_all, idx, gates, ln_g, ln_b, ys)


def _routing_tables(counts, n_slots):
    count = jnp.sum(counts, axis=0)
    blocks = (count + MOE_ROWS - 1) // MOE_ROWS
    blk_end = jnp.cumsum(blocks)
    base = (blk_end - blocks) * MOE_ROWS
    n_items = n_slots // MOE_ROWS // ITEM_BLOCKS + N_EXPERTS
    items_e = (blocks + ITEM_BLOCKS - 1) // ITEM_BLOCKS
    item_end = jnp.cumsum(items_e)
    n_used = item_end[-1]
    item = jnp.arange(n_items, dtype=jnp.int32)
    used_item = jnp.minimum(item, n_used - 1)
    item_e = jnp.minimum(jnp.sum((used_item[:, None] >= item_end[None, :]).astype(jnp.int32), axis=1),
                         N_EXPERTS - 1)
    of_item = lambda per_expert: jnp.sum(
        jnp.where(item_e[:, None] == jnp.arange(N_EXPERTS)[None, :], per_expert[None, :], 0), axis=1)
    local = used_item - of_item(item_end - items_e)
    item_b0 = of_item(blk_end - blocks) + ITEM_BLOCKS * local
    item_nb = jnp.where(item < n_used, jnp.clip(of_item(blocks) - ITEM_BLOCKS * local, 0, ITEM_BLOCKS), 0)
    tail_rows = count - (blocks - 1) * MOE_ROWS
    item_half = jnp.logical_and(jnp.logical_and(item < n_used, local == of_item(items_e) - 1),
                                of_item(tail_rows) <= MOE_ROWS // 2)
    items = tuple(a.astype(jnp.int32) for a in (item_e, item_b0, item_nb, item_half))
    start = base[None, :] + jnp.cumsum(counts, axis=0) - counts
    end = start + counts
    w0 = start // GROUP * GROUP
    ngrp = end // GROUP - start // GROUP
    part = (end % GROUP != 0).astype(jnp.int32)
    nwin = ngrp + part
    off = (jnp.cumsum(nwin, axis=1) - nwin) * GROUP
    nvalid = blk_end[-1]
    pend = (blk_end * MOE_ROWS).at[-1].set(n_slots)
    flat = lambda a: a.reshape(-1).astype(jnp.int32)
    tables = dict(w0=flat(w0), off=flat(off), first=flat(start - w0 + off), ngrp=flat(ngrp), part=flat(part),
                  nwin=flat(nwin), pend=flat(pend))
    return tables, items, nvalid.reshape(1).astype(jnp.int32)


def _moe_ffn(x_all, n_tokens, layer, w_router, b_router, wg, wu, wd, ln_g, ln_b, alpha, prompt_shape=None):
    idx, gates, cnt = _router(x_all, n_tokens, w_router, b_router)
    n_slots = TOP_K * n_tokens + N_EXPERTS * MOE_ROWS
    n_slots = -(-n_slots // (MOE_ROWS * ITEM_BLOCKS)) * (MOE_ROWS * ITEM_BLOCKS)
    tables, items, nvalid = _routing_tables(cnt[::8, :N_EXPERTS], n_slots)
    xs = _regroup(x_all, idx[:, :2 * TOP_K].T, tables, n_slots)
    ys = _grouped_swiglu(xs, items, nvalid, layer, wg, wu, wd)
    return _combine(x_all, idx, gates, tables, ys, ln_g, ln_b, alpha, prompt_shape)


def kernel(x_prompt, x_sample, state_pool, state_ssm_re, state_ssm_im, ln1_g, ln1_b, w_in, b_in, w_pool, pool_scale, lambda_re, lambda_im, log_dt, b_re, b_im, c_re, c_im, d_skip, w_glu, b_glu, w_proj_a, w_proj_b, w_out, ln2_g, ln2_b, w_ffn_gate, w_ffn_up, w_ffn_down, w_router, b_router, w_moe_gate, w_moe_up, w_moe_down):
    batch, seq, d_model = x_prompt.shape
    dec_batch = x_sample.shape[0]
    depth = w_in.shape[0]
    n_groups, n_state = lambda_re.shape[1], lambda_re.shape[2]
    chan = b_re.shape[-1]
    nst = n_groups * n_state
    pw = state_pool.shape[-1]
    alpha = (2.0 * depth) ** 0.25
    n_prompt = batch * seq
    n_tokens = n_prompt + dec_batch
    total_rows = -(-n_tokens // ROW_TILE) * ROW_TILE
    assert x_sample.shape[1] == 1 and n_prompt % ROW_TILE == 0 and seq % MIX_STEPS == 0
    assert batch * MIX_STEPS == ROW_TILE and dec_batch <= ROW_TILE

    rows_of = lambda a: a[:, None, :]
    a_re, a_im, bb_re, bb_im = _ssm_params(lambda_re, lambda_im, log_dt, b_re, b_im)
    lam = jnp.concatenate([a_re[::chan].reshape(depth, 1, nst), a_im[::chan].reshape(depth, 1, nst)], axis=2)
    wts = (w_in.astype(BF16), rows_of(b_in), w_pool.astype(BF16), rows_of(pool_scale), lam,
           _block_diag_in(bb_re, depth, n_groups, chan, n_state).astype(BF16),
           _block_diag_in(bb_im, depth, n_groups, chan, n_state).astype(BF16),
           _block_diag_out(c_re, depth, n_groups, chan, n_state).astype(BF16),
           _block_diag_out(c_im, depth, n_groups, chan, n_state).astype(BF16),
           d_skip.reshape(depth, 1, -1), w_glu.astype(BF16), rows_of(b_glu),
           w_proj_a.astype(BF16), w_proj_b.astype(BF16), w_out.astype(BF16),
           rows_of(ln1_g), rows_of(ln1_b))
    ffn_wts = (w_ffn_gate.astype(BF16), w_ffn_up.astype(BF16), w_ffn_down.astype(BF16))
    prev_s = jnp.transpose(state_pool, (0, 2, 1, 3)).reshape(depth, POOL_STATE * dec_batch, pw)
    h0_s = jnp.concatenate([state_ssm_re.reshape(depth, dec_batch, nst),
                            state_ssm_im.reshape(depth, dec_batch, nst)], axis=2)
    zero_prev = jnp.zeros((1, POOL_STATE * batch, pw), F32)
    zero_h = jnp.zeros((1, batch, 2 * nst), F32)

    x_p = x_prompt
    x_s = x_sample.reshape(dec_batch, d_model)
    s_row0 = 0
    outs = [[] for _ in range(6)]
    for l in range(depth):
        y_mix, pool_p, h_p = _token_mix(x_p, 0, 0, seq // MIX_STEPS, batch, MIX_STEPS, 0, alpha, l,
                                        zero_prev, zero_h, wts, total_rows, state_layer=0)
        y_mix, pool_s, h_s = _token_mix(x_s, s_row0, n_prompt, 1, dec_batch, 1, PAST_LEN, alpha, l,
                                        prev_s, h0_s, wts, total_rows, y_buf=y_mix)
        outs[0].append(jnp.transpose(pool_p.reshape(POOL_STATE, batch, pw), (1, 0, 2)))
        outs[1].append(h_p[:, :nst].reshape(batch, n_groups, n_state))
        outs[2].append(h_p[:, nst:].reshape(batch, n_groups, n_state))
        outs[3].append(jnp.transpose(pool_s.reshape(POOL_STATE, dec_batch, pw), (1, 0, 2)))
        outs[4].append(h_s[:, :nst].reshape(dec_batch, n_groups, n_state))
        outs[5].append(h_s[:, nst:].reshape(dec_batch, n_groups, n_state))
        j = l // 2
        if l % 2 == 0:
            y_all = _dense_ffn(y_mix, j, *ffn_wts, ln2_g[l][None, :], ln2_b[l][None, :], alpha)
        else:
            prompt_shape = (batch, seq) if l == depth - 1 else None
            y_all = _moe_ffn(y_mix, n_tokens, j, w_router[j], b_router[j], w_moe_gate, w_moe_up,
                             w_moe_down, ln2_g[l][None, :], ln2_b[l][None, :], alpha, prompt_shape)
        x_p = x_s = y_all
        s_row0 = n_prompt
    if isinstance(y_all, (tuple, list)):
        y_p, tail = y_all
        y_s = tail[:dec_batch].reshape(dec_batch, 1, d_model)
    else:
        y_p = jnp.transpose(y_all[:n_prompt].reshape(seq, batch, d_model), (1, 0, 2))
        y_s = y_all[n_prompt:n_tokens].reshape(dec_batch, 1, d_model)
    return (y_p, y_s, *[jnp.stack(o) for o in outs])
```

```python
import functools
import math

import jax
import jax.numpy as jnp
from jax import lax
from jax.experimental import pallas as pl
from jax.experimental.pallas import tpu as pltpu

F32 = jnp.float32
BF16 = jnp.bfloat16

POOL_WINDOWS = (2, 4, 8, 16)
POOL_STATE = max(POOL_WINDOWS) - 1
POOL_GROUP = 128
SSM_GROUP = 16
SSM_STATE = 64
N_EXPERTS = 8
TOP_K = 2
LN_EPS = 1e-5
PAST_LEN = 16384

LANES = 128
MXU_COLS = 256
ROW_TILE = 512
MIX_STEPS = 64
MIX_TILES_PER_STEP = 2
SCAN_COLS = 512
MOE_ROWS = 512
ITEM_BLOCKS = 5
MOE_FF_CHUNK = 896
GROUP = 16
STAGE_ROWS = TOP_K * ROW_TILE + N_EXPERTS * 2 * GROUP
VMEM_LIMIT = 56 * 1024 * 1024


def _dot(a, b):
    return jnp.dot(a, b, preferred_element_type=F32)


def _layer_norm(r, g, b):
    mu = jnp.mean(r, axis=-1, keepdims=True)
    d = r - mu
    var = jnp.mean(d * d, axis=-1, keepdims=True)
    return d * lax.rsqrt(var + LN_EPS) * g + b


def _swiglu(xb, gate_up_piece, n_pieces, wd_ref):
    pieces = []
    for p in range(n_pieces):
        gate, up = gate_up_piece(p)
        pieces.append((jax.nn.silu(gate) * up).astype(BF16))
    return _dot(jnp.concatenate(pieces, axis=-1), wd_ref[...])


def _swiglu_split(xb, wg_ref, wu_ref, wd_ref):
    def piece(p):
        cols = slice(p * MXU_COLS, (p + 1) * MXU_COLS)
        return _dot(xb, wg_ref[:, cols]), _dot(xb, wu_ref[:, cols])
    return _swiglu(xb, piece, wd_ref.shape[0] // MXU_COLS, wd_ref)


def _swiglu_interleaved(xb, wgu_piece, wd_ref):
    def piece(p):
        gate_up = _dot(xb, wgu_piece(p))
        return gate_up[:, :LANES], gate_up[:, LANES:]
    return _swiglu(xb, piece, wd_ref.shape[0] // LANES, wd_ref)


def _params(*sem):
    return pltpu.CompilerParams(dimension_semantics=sem, vmem_limit_bytes=VMEM_LIMIT)


def _ssm_param_kernel(lre_ref, lim_ref, ldt_ref, bre_ref, bim_ref, are_ref, aim_ref, bbre_ref, bbim_ref):
    lre, lim = lre_ref[...], lim_ref[...]
    dt = jnp.exp(ldt_ref[...])
    mag = jnp.exp(lre * dt)
    a_re = mag * jnp.cos(lim * dt)
    a_im = mag * jnp.sin(lim * dt)
    den = lre * lre + lim * lim
    n_re, n_im = a_re - 1.0, a_im
    k_re = (n_re * lre + n_im * lim) / den
    k_im = (n_im * lre - n_re * lim) / den
    b_re, b_im = bre_ref[...], bim_ref[...]
    are_ref[...] = a_re
    aim_ref[...] = a_im
    bbre_ref[...] = k_re * b_re - k_im * b_im
    bbim_ref[...] = k_re * b_im + k_im * b_re


def _ssm_params(lambda_re, lambda_im, log_dt, b_re, b_im):
    depth, g, p = lambda_re.shape
    c = b_re.shape[-1]
    rep = lambda a: jnp.repeat(a.reshape(depth * g, p), c, axis=0)
    bt = lambda a: jnp.transpose(a, (0, 1, 3, 2)).reshape(depth * g * c, p)
    shape = jax.ShapeDtypeStruct((depth * g * c, p), F32)
    return pl.pallas_call(_ssm_param_kernel, out_shape=(shape,) * 4, name="s5_discretise")(
        rep(lambda_re), rep(lambda_im), rep(jnp.broadcast_to(log_dt[:, :, None], (depth, g, p))),
        bt(b_re), bt(b_im))


def _block_diag_in(bt, depth, g, c, p):
    m = g // 8
    a = bt.reshape(depth, m, 8, c, p)
    eye = jnp.eye(8, dtype=bt.dtype)
    return jnp.einsum('lmgcp,gh->lmgchp', a, eye).reshape(depth, m, 8 * c, 8 * p)


def _block_diag_out(cm, depth, g, c, p):
    m = g // 8
    a = jnp.transpose(cm, (0, 1, 3, 2)).reshape(depth, m, 8, p, c)
    eye = jnp.eye(8, dtype=cm.dtype)
    return jnp.einsum('lmgpc,gh->lmgphc', a, eye).reshape(depth, m, 8 * p, 8 * c)


def _mix_kernel(*refs, n_steps, zero_tail, n_sub, **kw):
    if not zero_tail:
        _mix_body(*refs, n_sub=n_sub, **kw)
        return
    step = pl.program_id(0)
    y_ref = refs[-(5 + n_sub)]

    @pl.when(step < n_steps)
    def _():
        _mix_body(*refs, n_sub=n_sub, **kw)

    @pl.when(step >= n_steps)
    def _():
        y_ref[...] = jnp.zeros(y_ref.shape, F32)


def _mix_body(x_ref, prev_ref, h0_ref, *rest, nb, tt, start_pos, alpha, aliased, n_sub):
    wts, rest = rest[:N_MIX_WEIGHTS], rest[N_MIX_WEIGHTS:]
    if aliased:
        rest = rest[1:]
    y_ref, pool_ref, hlast_ref, ext_ref, hs_ref, *bu_refs = rest
    rows = nb * tt
    halo = POOL_STATE * nb
    step = pl.program_id(0)

    @pl.when(step == 0)
    def _():
        ext_ref[0:halo, :] = prev_ref[...]
        hs_ref[...] = h0_ref[...]

    for sub in range(n_sub):
        if x_ref.ndim == 3:
            x = jnp.concatenate([x_ref[:, sub * tt + t, :] for t in range(tt)], axis=0)
        else:
            x = x_ref[sub * rows:(sub + 1) * rows, :]
        out = _mix_tile(x, step * n_sub + sub, *wts, pool_ref, hlast_ref, ext_ref, bu_refs[sub], hs_ref,
                        nb=nb, tt=tt, start_pos=start_pos, alpha=alpha)
        y_ref[sub * rows:(sub + 1) * rows, :] = out
    if y_ref.shape[0] != n_sub * rows:
        y_ref[n_sub * rows:, :] = jnp.zeros((y_ref.shape[0] - n_sub * rows, y_ref.shape[1]), F32)


N_MIX_WEIGHTS = 17


def _mix_tile(x, step, win_ref, bin_ref, wpool_ref, pscale_ref, lam_ref,
              bre_ref, bim_ref, cre_ref, cim_ref, dskip_ref, wglu_ref, bglu_ref,
              wpa_ref, wpb_ref, wout_ref, lng_ref, lnb_ref, pool_ref, hlast_ref, ext_ref, bu_ref, hs_ref,
              *, nb, tt, start_pos, alpha):
    rows = nb * tt
    halo = POOL_STATE * nb
    d_model = x.shape[-1]
    pw = pscale_ref.shape[1]
    nst = lam_ref.shape[1] // 2
    xb = x.astype(BF16)

    u_a = _dot(xb, win_ref[:, 0:pw]) + bin_ref[:, 0:pw]
    ext_ref[halo:halo + rows, :] = u_a
    t_idx = lax.shift_right_logical(lax.broadcasted_iota(jnp.int32, (rows, 1), 0), int(math.log2(nb)))
    pos1 = t_idx + (step * tt + start_pos + 1)
    mixed = []
    for gi, w in enumerate(POOL_WINDOWS):
        sl = slice(gi * POOL_GROUP, (gi + 1) * POOL_GROUP)
        cur = u_a[:, sl]
        acc = cur
        for k in range(1, w):
            acc = acc + ext_ref[halo - k * nb:halo - k * nb + rows, sl]
        count = jnp.minimum(pos1, w).astype(F32)
        pooled = acc / count - cur
        mixed.append(_dot(pooled.astype(BF16), wpool_ref[gi]))
    out_a = jnp.concatenate(mixed, axis=-1) * pscale_ref[...]
    new_halo = ext_ref[rows:rows + halo, :]
    ext_ref[0:halo, :] = new_halo
    pool_ref[...] = new_halo
    proj_a = _dot(out_a.astype(BF16), wpa_ref[...])

    u_b = _dot(xb, win_ref[:, pw:2 * pw]) + bin_ref[:, pw:2 * pw]
    ub16 = u_b.astype(BF16)
    n_m = bre_ref.shape[0]
    kin, kst = bre_ref.shape[1], bre_ref.shape[2]
    for m in range(n_m):
        um = ub16[:, m * kin:(m + 1) * kin]
        bu_ref[:, m * kst:(m + 1) * kst] = _dot(um, bre_ref[m])
        bu_ref[:, nst + m * kst:nst + (m + 1) * kst] = _dot(um, bim_ref[m])
    if tt == 1:
        a_re, a_im = lam_ref[:, 0:nst], lam_ref[:, nst:]
        h_re, h_im = hs_ref[:, 0:nst], hs_ref[:, nst:]
        n_re = a_re * h_re - a_im * h_im + bu_ref[:, 0:nst]
        n_im = a_re * h_im + a_im * h_re + bu_ref[:, nst:]
        bu_ref[:, 0:nst] = n_re
        bu_ref[:, nst:] = n_im
        hs_ref[:, 0:nst] = n_re
        hs_ref[:, nst:] = n_im
    else:
        for q in range(nst // SCAN_COLS):
            c_re = slice(q * SCAN_COLS, (q + 1) * SCAN_COLS)
            c_im = slice(nst + q * SCAN_COLS, nst + (q + 1) * SCAN_COLS)
            a_re = jnp.broadcast_to(lam_ref[:, c_re], (nb, SCAN_COLS))
            a_im = jnp.broadcast_to(lam_ref[:, c_im], (nb, SCAN_COLS))

            def body(t, carry, c_re=c_re, c_im=c_im, a_re=a_re, a_im=a_im):
                h_re, h_im = carry
                r = pl.ds(pl.multiple_of(t * nb, nb), nb)
                n_re = a_re * h_re - a_im * h_im + bu_ref[r, c_re]
                n_im = a_re * h_im + a_im * h_re + bu_ref[r, c_im]
                bu_ref[r, c_re] = n_re
                bu_ref[r, c_im] = n_im
                return n_re, n_im

            h_re, h_im = lax.fori_loop(0, tt, body, (hs_ref[:, c_re], hs_ref[:, c_im]), unroll=True)
            hs_ref[:, c_re] = h_re
            hs_ref[:, c_im] = h_im
    hlast_ref[...] = hs_ref[...]

    ys = []
    for m in range(n_m):
        h_re = bu_ref[:, m * kst:(m + 1) * kst].astype(BF16)
        h_im = bu_ref[:, nst + m * kst:nst + (m + 1) * kst].astype(BF16)
        ys.append(_dot(h_re, cre_ref[m]) - _dot(h_im, cim_ref[m]))
    y = jnp.concatenate(ys, axis=-1) + dskip_ref[...] * u_b
    y = jax.nn.gelu(y)
    out_b = y * jax.nn.sigmoid(_dot(y.astype(BF16), wglu_ref[...]) + bglu_ref[...])
    proj_b = _dot(out_b.astype(BF16), wpb_ref[...])

    z_ga = _dot(xb, win_ref[:, 2 * pw:2 * pw + d_model]) + bin_ref[:, 2 * pw:2 * pw + d_model]
    merged = jax.nn.sigmoid(z_ga) * proj_a
    z_gb = _dot(xb, win_ref[:, 2 * pw + d_model:]) + bin_ref[:, 2 * pw + d_model:]
    merged = merged + jax.nn.sigmoid(z_gb) * proj_b
    mix = _dot(merged.astype(BF16), wout_ref[...])
    return _layer_norm(alpha * x + mix, lng_ref[...], lnb_ref[...])


def _token_mix(x_src, x_row0, row0, n_steps, nb, tt, start_pos, alpha, layer, prev, h0, wts, total_rows,
               y_buf=None, state_layer=None):
    state_layer = layer if state_layer is None else state_layer
    rows = nb * tt
    d_model = x_src.shape[-1]
    nst2 = h0.shape[-1]
    pw = prev.shape[-1]
    halo = POOL_STATE * nb
    aliased = y_buf is not None
    n_sub = MIX_TILES_PER_STEP if n_steps % MIX_TILES_PER_STEP == 0 else 1
    n_steps, step_rows = n_steps // n_sub, rows * n_sub
    assert len(wts) == N_MIX_WEIGHTS
    const = lambda a: _layer_spec(a, layer)
    tile = lambda i: jnp.minimum(i, n_steps - 1)
    if x_src.ndim == 3:
        x_spec = pl.BlockSpec((nb, tt * n_sub, d_model), lambda i: (0, tile(i), 0))
    else:
        x_spec = pl.BlockSpec((step_rows, d_model), lambda i: (x_row0 // step_rows + tile(i), 0))
    in_specs = [x_spec, _layer_spec(prev, state_layer), _layer_spec(h0, state_layer)]
    in_specs += [const(w) for w in wts]
    args = [x_src, prev, h0, *wts]
    if aliased:
        in_specs.append(pl.BlockSpec(memory_space=pl.ANY))
        args.append(y_buf)
        y_spec = pl.BlockSpec((ROW_TILE, d_model), lambda i: (row0 // ROW_TILE, 0))
    else:
        y_spec = pl.BlockSpec((step_rows, d_model), lambda i: (row0 // step_rows + i, 0))
    out_shape = (jax.ShapeDtypeStruct((total_rows, d_model), F32),
                 jax.ShapeDtypeStruct((halo, pw), F32),
                 jax.ShapeDtypeStruct((nb, nst2), F32))
    out_specs = (y_spec,
                 pl.BlockSpec((halo, pw), lambda i: (0, 0)),
                 pl.BlockSpec((nb, nst2), lambda i: (0, 0)))
    zero_tail = not aliased
    assert aliased or (row0 == 0 and n_steps * step_rows < total_rows <= (n_steps + 1) * step_rows)
    kern = functools.partial(_mix_kernel, n_steps=n_steps, zero_tail=zero_tail, n_sub=n_sub,
                             nb=nb, tt=tt, start_pos=start_pos, alpha=alpha, aliased=aliased)
    return pl.pallas_call(
        kern, grid=(n_steps + int(zero_tail),), in_specs=in_specs, out_specs=out_specs, out_shape=out_shape,
        scratch_shapes=[pltpu.VMEM((halo + rows, pw), F32), pltpu.VMEM((nb, nst2), F32)]
        + [pltpu.VMEM((rows, nst2), F32) for _ in range(n_sub)],
        input_output_aliases={len(args) - 1: 0} if aliased else {},
        name="token_mix_decode" if aliased else "token_mix_prompt",
        compiler_params=_params("arbitrary"))(*args)


def _ffn_kernel(x_ref, wg_ref, wu_ref, wd_ref, g_ref, b_ref, o_ref, *, alpha):
    x = x_ref[...]
    ffn = _swiglu_split(x.astype(BF16), wg_ref, wu_ref, wd_ref)
    o_ref[...] = _layer_norm(alpha * x + ffn, g_ref[...], b_ref[...])


def _layer_spec(a, layer, **kw):
    return pl.BlockSpec((None,) + a.shape[1:], lambda *_, n=a.ndim - 1: (layer,) + (0,) * n, **kw)


def _dense_ffn(x_all, layer, wg, wu, wd, ln_g, ln_b, alpha):
    n, d_model = x_all.shape
    resident = lambda a: _layer_spec(a, layer, pipeline_mode=pl.Buffered(1))
    return pl.pallas_call(
        functools.partial(_ffn_kernel, alpha=alpha),
        grid=(n // ROW_TILE,),
        in_specs=[pl.BlockSpec((ROW_TILE, d_model), lambda i: (i, 0)),
                  resident(wg), resident(wu), resident(wd),
                  pl.BlockSpec((1, d_model), lambda i: (0, 0)), pl.BlockSpec((1, d_model), lambda i: (0, 0))],
        out_specs=pl.BlockSpec((ROW_TILE, d_model), lambda i: (i, 0)),
        out_shape=jax.ShapeDtypeStruct((n, d_model), F32),
        name="dense_swiglu_ln",
        compiler_params=_params("parallel"))(x_all, wg, wu, wd, ln_g, ln_b)


def _split_bf16(a):
    hi = a.astype(BF16)
    lo = (a - hi.astype(F32)).astype(BF16)
    return hi, lo


DEAD_RANK = -(1 << 20)


def _router_kernel(x_ref, w_ref, b_ref, idx_ref, gate_ref, cnt_ref, *, n_tokens):
    x_hi, x_lo = _split_bf16(x_ref[...])
    w_hi, w_lo = _split_bf16(w_ref[...])
    logits = _dot(x_hi, w_hi) + (_dot(x_lo, w_hi) + _dot(x_hi, w_lo)) + b_ref[...]
    lane = lax.broadcasted_iota(jnp.int32, logits.shape, 1)
    logits = jnp.where(lane < N_EXPERTS, logits, -jnp.inf)
    m1 = jnp.max(logits, axis=-1, keepdims=True)
    i1 = jnp.min(jnp.where(logits == m1, lane, LANES), axis=-1, keepdims=True)
    rest = jnp.where(lane == i1, -jnp.inf, logits)
    m2 = jnp.max(rest, axis=-1, keepdims=True)
    i2 = jnp.min(jnp.where(rest == m2, lane, LANES), axis=-1, keepdims=True)
    e2 = jnp.exp(m2 - m1)
    den = 1.0 + e2
    gate_ref[...] = jnp.where(lane == 0, 1.0 / den, e2 / den)

    row = lax.broadcasted_iota(jnp.int32, (ROW_TILE, 1), 0) + pl.program_id(0) * ROW_TILE
    live = row < n_tokens
    pick1 = jnp.where(jnp.logical_and(lane == i1, live), 1.0, 0.0)
    pick2 = jnp.where(jnp.logical_and(lane == i2, live), 1.0, 0.0)
    picks = pick1 + pick2
    r = lax.broadcasted_iota(jnp.int32, (ROW_TILE, ROW_TILE), 0)
    c = lax.broadcasted_iota(jnp.int32, (ROW_TILE, ROW_TILE), 1)
    earlier = _dot(jnp.where(c < r, 1.0, 0.0).astype(BF16), picks.astype(BF16))
    rank1 = jnp.sum(earlier * pick1, axis=-1, keepdims=True).astype(jnp.int32)
    rank2 = jnp.sum(earlier * pick2, axis=-1, keepdims=True).astype(jnp.int32)
    rank1 = jnp.where(live, rank1, DEAD_RANK)
    rank2 = jnp.where(live, rank2, DEAD_RANK)
    idx_ref[...] = jnp.where(lane == 0, i1, jnp.where(lane == 1, i2, jnp.where(lane == 2, rank1, rank2)))
    counts = jnp.sum(picks, axis=0, keepdims=True).astype(jnp.int32)
    cnt_ref[...] = jnp.broadcast_to(counts, cnt_ref.shape)


def _router(x_all, n_tokens, w_router, b_router):
    n, d_model = x_all.shape
    w = jnp.pad(w_router, ((0, 0), (0, LANES - N_EXPERTS)))
    b = jnp.pad(b_router, (0, LANES - N_EXPERTS))[None, :]
    n_tiles = n // ROW_TILE
    return pl.pallas_call(
        functools.partial(_router_kernel, n_tokens=n_tokens), grid=(n_tiles,),
        in_specs=[pl.BlockSpec((ROW_TILE, d_model), lambda i: (i, 0)),
                  pl.BlockSpec((d_model, LANES), lambda i: (0, 0)),
                  pl.BlockSpec((1, LANES), lambda i: (0, 0))],
        out_specs=(pl.BlockSpec((ROW_TILE, LANES), lambda i: (i, 0)),
                   pl.BlockSpec((ROW_TILE, LANES), lambda i: (i, 0)),
                   pl.BlockSpec((8, LANES), lambda i: (i, 0))),
        out_shape=(jax.ShapeDtypeStruct((n, LANES), jnp.int32), jax.ShapeDtypeStruct((n, LANES), F32),
                   jax.ShapeDtypeStruct((n_tiles * 8, LANES), jnp.int32)),
        name="moe_router",
        compiler_params=_params("parallel"))(x_all, w, b)


def _group_copy(src_ref, src_row, dst_ref, dst_row, sem):
    return pltpu.make_async_copy(src_ref.at[pl.ds(pl.multiple_of(src_row, GROUP), GROUP)],
                                 dst_ref.at[pl.ds(pl.multiple_of(dst_row, GROUP), GROUP)], sem)


def _wait_groups(count, src_ref, dst_ref, sem):
    def wait(_, carry):
        _group_copy(src_ref, 0, dst_ref, 0, sem).wait()
        return carry
    lax.fori_loop(0, count, wait, 0)


def _stage_rows(expert, rank, first_ref, step):
    pos = rank
    for e in range(N_EXPERTS):
        pos = pos + jnp.where(expert == e, first_ref[step * N_EXPERTS + e], 0)
    return pos


def _regroup_kernel(w0_ref, off_ref, ngrp_ref, part_ref, pend_ref, first_ref, x_ref, idx_ref, xs_hbm,
                    stage_ref, carry_ref, zero_ref, sem):
    step = pl.program_id(0)

    @pl.when(step == 0)
    def _():
        carry_ref[...] = jnp.zeros(carry_ref.shape, BF16)
        zero_ref[...] = jnp.zeros(zero_ref.shape, BF16)

    xb = x_ref[...].astype(BF16)
    stage_row = lax.broadcasted_iota(jnp.int32, (STAGE_ROWS, ROW_TILE), 0)
    idx = idx_ref[...]
    hit = None
    for choice in range(TOP_K):
        pos = _stage_rows(idx[choice:choice + 1, :], idx[TOP_K + choice:TOP_K + choice + 1, :], first_ref, step)
        hit = stage_row == pos if hit is None else jnp.logical_or(hit, stage_row == pos)
    stage_ref[...] = _dot(jnp.where(hit, 1.0, 0.0).astype(BF16), xb).astype(BF16)

    started = 0
    for e in range(N_EXPERTS):
        k = step * N_EXPERTS + e
        off, w0, ngrp = off_ref[k], w0_ref[k], ngrp_ref[k]
        head = pl.ds(pl.multiple_of(off, GROUP), GROUP)
        stage_ref[head, :] = stage_ref[head, :] + carry_ref[e]

        def send(q, carry, off=off, w0=w0):
            _group_copy(stage_ref, off + q * GROUP, xs_hbm, w0 + q * GROUP, sem).start()
            return carry

        lax.fori_loop(0, ngrp, send, 0)
        tail = stage_ref[pl.ds(pl.multiple_of(off + ngrp * GROUP, GROUP), GROUP), :]
        carry_ref[e] = jnp.where(part_ref[k] != 0, tail, jnp.zeros_like(tail))
        started = started + ngrp
    _wait_groups(started, stage_ref, xs_hbm, sem)

    @pl.when(step == pl.num_programs(0) - 1)
    def _():
        flushed = 0
        for e in range(N_EXPERTS):
            k = step * N_EXPERTS + e
            row = w0_ref[k] + ngrp_ref[k] * GROUP
            is_open = part_ref[k]

            @pl.when(is_open != 0)
            def _(row=row, e=e):
                pltpu.make_async_copy(carry_ref.at[e], xs_hbm.at[pl.ds(pl.multiple_of(row, GROUP), GROUP)],
                                      sem).start()

            row = row + is_open * GROUP
            n_zero = (pend_ref[e] - row) // GROUP

            def fill(q, carry, row=row):
                _group_copy(zero_ref, 0, xs_hbm, row + q * GROUP, sem).start()
                return carry

            lax.fori_loop(0, n_zero, fill, 0)
            flushed = flushed + is_open + n_zero
        _wait_groups(flushed, zero_ref, xs_hbm, sem)


def _regroup(x_all, idx_rows, tables, n_slots):
    n, d_model = x_all.shape
    return pl.pallas_call(
        _regroup_kernel,
        grid_spec=pltpu.PrefetchScalarGridSpec(
            num_scalar_prefetch=6, grid=(n // ROW_TILE,),
            in_specs=[pl.BlockSpec((ROW_TILE, d_model), lambda i, *_: (i, 0)),
                      pl.BlockSpec((2 * TOP_K, ROW_TILE), lambda i, *_: (0, i))],
            out_specs=pl.BlockSpec(memory_space=pl.ANY),
            scratch_shapes=[pltpu.VMEM((STAGE_ROWS, d_model), BF16),
                            pltpu.VMEM((N_EXPERTS, GROUP, d_model), BF16),
                            pltpu.VMEM((GROUP, d_model), BF16),
                            pltpu.SemaphoreType.DMA(())]),
        out_shape=jax.ShapeDtypeStruct((n_slots, d_model), BF16),
        name="moe_regroup",
        compiler_params=_params("arbitrary"))(
            tables["w0"], tables["off"], tables["ngrp"], tables["part"], tables["pend"], tables["first"],
            x_all, idx_rows)


def _moe_kernel(item_e_ref, item_b0_ref, item_nb_ref, item_half_ref, nvalid_ref, *refs):
    xs_refs = refs[:ITEM_BLOCKS]
    wg_ref, wu_ref, wd_ref, ys_hbm = refs[ITEM_BLOCKS:ITEM_BLOCKS + 4]
    *wgub_refs, wdb_ref, acc_ref, stage_ref, sem = refs[ITEM_BLOCKS + 4:]
    item, c = pl.program_id(0), pl.program_id(1)
    last_item, last_c = pl.num_programs(0) - 1, pl.num_programs(1) - 1
    n_blocks = ys_hbm.shape[0] // MOE_ROWS
    nb, b0 = item_nb_ref[item], item_b0_ref[item]

    def block_out(src_ref, block):
        rows = pl.ds(pl.multiple_of(block * MOE_ROWS, MOE_ROWS), MOE_ROWS)
        return pltpu.make_async_copy(src_ref, ys_hbm.at[rows], sem)

    def start_and_wait(count, copy_of):
        def start(q, carry):
            copy_of(q).start()
            return carry

        def wait(q, carry):
            copy_of(q).wait()
            return carry

        lax.fori_loop(0, count, start, 0)
        lax.fori_loop(0, count, wait, 0)

    @pl.when(jnp.logical_and(item == 0, c == 0))
    def _():
        acc_ref[...] = jnp.zeros(acc_ref.shape, F32)

    def cast_then_load(p):
        cols = slice(p * LANES, (p + 1) * LANES)
        wgub_refs[p][:, :LANES] = wg_ref[:, cols].astype(BF16)
        wgub_refs[p][:, LANES:] = wu_ref[:, cols].astype(BF16)
        return wgub_refs[p][...]

    def run_block(j, n_rows):
        if j == 0:
            wdb_ref[...] = wd_ref[...].astype(BF16)
        piece = cast_then_load if j == 0 else (lambda p: wgub_refs[p][...])
        part = _swiglu_interleaved(xs_refs[j][0:n_rows, :], piece, wdb_ref)
        total = jnp.where(c == 0, part, acc_ref[j, 0:n_rows, :] + part)
        acc_ref[j, 0:n_rows, :] = total
        stage_ref[j, 0:n_rows, :] = total.astype(BF16)
        if n_rows < MOE_ROWS:
            stage_ref[j, n_rows:, :] = jnp.zeros((MOE_ROWS - n_rows, stage_ref.shape[2]), BF16)

    for j in range(ITEM_BLOCKS):
        half = jnp.logical_and(j == nb - 1, item_half_ref[item] != 0)
        pl.when(jnp.logical_and(j < nb, jnp.logical_not(half)))(functools.partial(run_block, j, MOE_ROWS))
        pl.when(half)(functools.partial(run_block, j, MOE_ROWS // 2))

    @pl.when(c == last_c)
    def _():
        start_and_wait(nb, lambda q: block_out(stage_ref.at[q], b0 + q))

    @pl.when(jnp.logical_and(item == last_item, c == last_c))
    def _():
        stage_ref[0] = jnp.zeros(stage_ref.shape[1:], BF16)
        first_unused = nvalid_ref[0]
        start_and_wait(n_blocks - first_unused, lambda q: block_out(stage_ref.at[0], first_unused + q))


def _grouped_swiglu(xs, items, nvalid, layer, wg, wu, wd):
    n_slots, d_model = xs.shape
    d_ff = wg.shape[3]
    fc = MOE_FF_CHUNK
    n_chunks = d_ff // fc
    n_blocks = n_slots // MOE_ROWS
    item_e, item_b0, item_nb, item_half = items
    assert d_ff % fc == 0 and n_slots % MOE_ROWS == 0
    chunk = lambda c, nb, it: jnp.where(nb[it] > 0, c, n_chunks - 1)
    w_in_map = lambda it, c, e, b0, nb, *_: (layer, e[it], 0, chunk(c, nb, it))
    w_out_map = lambda it, c, e, b0, nb, *_: (layer, e[it], chunk(c, nb, it), 0)
    xs_spec = lambda k: pl.BlockSpec(
        (MOE_ROWS, d_model), lambda it, c, e, b0, *_: (jnp.minimum(b0[it] + k, n_blocks - 1), 0))
    return pl.pallas_call(
        _moe_kernel,
        grid_spec=pltpu.PrefetchScalarGridSpec(
            num_scalar_prefetch=5, grid=(item_e.shape[0], n_chunks),
            in_specs=[xs_spec(k) for k in range(ITEM_BLOCKS)] + [
                pl.BlockSpec((None, None, d_model, fc), w_in_map),
                pl.BlockSpec((None, None, d_model, fc), w_in_map),
                pl.BlockSpec((None, None, fc, d_model), w_out_map)],
            out_specs=pl.BlockSpec(memory_space=pl.ANY),
            scratch_shapes=[pltpu.VMEM((d_model, 2 * LANES), BF16) for _ in range(fc // LANES)] + [
                            pltpu.VMEM((fc, d_model), BF16),
                            pltpu.VMEM((ITEM_BLOCKS, MOE_ROWS, d_model), F32),
                            pltpu.VMEM((ITEM_BLOCKS, MOE_ROWS, d_model), BF16),
                            pltpu.SemaphoreType.DMA(())]),
        out_shape=jax.ShapeDtypeStruct((n_slots, d_model), BF16),
        name="moe_grouped_swiglu",
        compiler_params=_params("arbitrary", "arbitrary"))(
            item_e, item_b0, item_nb, item_half, nvalid, *([xs] * ITEM_BLOCKS), wg, wu, wd)


def _combine_kernel(w0_ref, off_ref, nwin_ref, first_ref, x_ref, idx_ref, gate_ref, g_ref, b_ref, ys_hbm,
                    *rest, alpha, batch_major):
    if batch_major:
        prompt_ref, tail_ref, stage_ref, sem = rest
    else:
        o_ref, stage_ref, sem = rest
    step = pl.program_id(0)

    def fetch_tile(tile, buf):
        stage_ref[buf] = jnp.zeros(stage_ref.shape[1:], BF16)
        for e in range(N_EXPERTS):
            k = tile * N_EXPERTS + e
            off, w0 = off_ref[k], w0_ref[k]

            def fetch(q, carry, off=off, w0=w0):
                _group_copy(ys_hbm, w0 + q * GROUP, stage_ref.at[buf], off + q * GROUP, sem.at[buf]).start()
                return carry

            lax.fori_loop(0, nwin_ref[k], fetch, 0)

    @pl.when(step == 0)
    def _():
        fetch_tile(step, 0)

    buf = lax.rem(step, 2)

    @pl.when(step + 1 < pl.num_programs(0))
    def _():
        fetch_tile(step + 1, 1 - buf)

    n_groups = 0
    for e in range(N_EXPERTS):
        n_groups = n_groups + nwin_ref[step * N_EXPERTS + e]
    _wait_groups(n_groups, ys_hbm, stage_ref.at[buf], sem.at[buf])

    stage_col = lax.broadcasted_iota(jnp.int32, (ROW_TILE, STAGE_ROWS), 1)
    idx = idx_ref[...]
    gates = gate_ref[...]
    ys = stage_ref[buf]
    f = None
    for choice in range(TOP_K):
        pos = _stage_rows(idx[:, choice:choice + 1], idx[:, TOP_K + choice:TOP_K + choice + 1], first_ref, step)
        pick = jnp.where(stage_col == pos, 1.0, 0.0).astype(BF16)
        term = gates[:, choice:choice + 1] * _dot(pick, ys)
        f = term if f is None else f + term
    out = _layer_norm(alpha * x_ref[...] + f, g_ref[...], b_ref[...])
    if not batch_major:
        o_ref[...] = out
        return
    nb, tt = prompt_ref.shape[0], prompt_ref.shape[1]
    last = pl.num_programs(0) - 1

    @pl.when(step < last)
    def _():
        for t in range(tt):
            prompt_ref[:, t, :] = out[t * nb:(t + 1) * nb, :]

    @pl.when(step == last)
    def _():
        tail_ref[...] = out


def _combine(x_all, idx, gates, tables, ys, ln_g, ln_b, alpha, prompt_shape=None):
    n, d_model = x_all.shape
    n_tiles = n // ROW_TILE
    if prompt_shape is None:
        out_specs = pl.BlockSpec((ROW_TILE, d_model), lambda i, *_: (i, 0))
        out_shape = jax.ShapeDtypeStruct((n, d_model), F32)
    else:
        batch, seq = prompt_shape
        tt = ROW_TILE // batch
        assert batch * seq == (n_tiles - 1) * ROW_TILE
        out_specs = (pl.BlockSpec((batch, tt, d_model), lambda i, *_: (0, jnp.minimum(i, n_tiles - 2), 0)),
                     pl.BlockSpec((ROW_TILE, d_model), lambda i, *_: (0, 0)))
        out_shape = (jax.ShapeDtypeStruct((batch, seq, d_model), F32),
                     jax.ShapeDtypeStruct((ROW_TILE, d_model), F32))
    return pl.pallas_call(
        functools.partial(_combine_kernel, alpha=alpha, batch_major=prompt_shape is not None),
        grid_spec=pltpu.PrefetchScalarGridSpec(
            num_scalar_prefetch=4, grid=(n_tiles,),
            in_specs=[pl.BlockSpec((ROW_TILE, d_model), lambda i, *_: (i, 0)),
                      pl.BlockSpec((ROW_TILE, LANES), lambda i, *_: (i, 0)),
                      pl.BlockSpec((ROW_TILE, LANES), lambda i, *_: (i, 0)),
                      pl.BlockSpec((1, d_model), lambda i, *_: (0, 0)),
                      pl.BlockSpec((1, d_model), lambda i, *_: (0, 0)),
                      pl.BlockSpec(memory_space=pl.ANY)],
            out_specs=out_specs,
            scratch_shapes=[pltpu.VMEM((2, STAGE_ROWS, d_model), BF16), pltpu.SemaphoreType.DMA((2,))]),
        out_shape=out_shape,
        name="moe_combine_ln",
        compiler_params=_params("arbitrary"))(
            tables["w0"], tables["off"], tables["nwin"], tables["first"], x_all, idx, gates, ln_g, ln_b, ys)


def _routing_tables(counts, n_slots):
    count = jnp.sum(counts, axis=0)
    blocks = (count + MOE_ROWS - 1) // MOE_ROWS
    blk_end = jnp.cumsum(blocks)
    base = (blk_end - blocks) * MOE_ROWS
    n_items = n_slots // MOE_ROWS // ITEM_BLOCKS + N_EXPERTS
    items_e = (blocks + ITEM_BLOCKS - 1) // ITEM_BLOCKS
    item_end = jnp.cumsum(items_e)
    n_used = item_end[-1]
    item = jnp.arange(n_items, dtype=jnp.int32)
    used_item = jnp.minimum(item, n_used - 1)
    item_e = jnp.minimum(jnp.sum((used_item[:, None] >= item_end[None, :]).astype(jnp.int32), axis=1),
                         N_EXPERTS - 1)
    of_item = lambda per_expert: jnp.sum(
        jnp.where(item_e[:, None] == jnp.arange(N_EXPERTS)[None, :], per_expert[None, :], 0), axis=1)
    local = used_item - of_item(item_end - items_e)
    item_b0 = of_item(blk_end - blocks) + ITEM_BLOCKS * local
    item_nb = jnp.where(item < n_used, jnp.clip(of_item(blocks) - ITEM_BLOCKS * local, 0, ITEM_BLOCKS), 0)
    tail_rows = count - (blocks - 1) * MOE_ROWS
    item_half = jnp.logical_and(jnp.logical_and(item < n_used, local == of_item(items_e) - 1),
                                of_item(tail_rows) <= MOE_ROWS // 2)
    items = tuple(a.astype(jnp.int32) for a in (item_e, item_b0, item_nb, item_half))
    start = base[None, :] + jnp.cumsum(counts, axis=0) - counts
    end = start + counts
    w0 = start // GROUP * GROUP
    ngrp = end // GROUP - start // GROUP
    part = (end % GROUP != 0).astype(jnp.int32)
    nwin = ngrp + part
    off = (jnp.cumsum(nwin, axis=1) - nwin) * GROUP
    nvalid = blk_end[-1]
    pend = (blk_end * MOE_ROWS).at[-1].set(n_slots)
    flat = lambda a: a.reshape(-1).astype(jnp.int32)
    tables = dict(w0=flat(w0), off=flat(off), first=flat(start - w0 + off), ngrp=flat(ngrp), part=flat(part),
                  nwin=flat(nwin), pend=flat(pend))
    return tables, items, nvalid.reshape(1).astype(jnp.int32)


def _moe_ffn(x_all, n_tokens, layer, w_router, b_router, wg, wu, wd, ln_g, ln_b, alpha, prompt_shape=None):
    idx, gates, cnt = _router(x_all, n_tokens, w_router, b_router)
    n_slots = TOP_K * n_tokens + N_EXPERTS * MOE_ROWS
    n_slots = -(-n_slots // (MOE_ROWS * ITEM_BLOCKS)) * (MOE_ROWS * ITEM_BLOCKS)
    tables, items, nvalid = _routing_tables(cnt[::8, :N_EXPERTS], n_slots)
    xs = _regroup(x_all, idx[:, :2 * TOP_K].T, tables, n_slots)
    ys = _grouped_swiglu(xs, items, nvalid, layer, wg, wu, wd)
    return _combine(x_all, idx, gates, tables, ys, ln_g, ln_b, alpha, prompt_shape)


def kernel(x_prompt, x_sample, state_pool, state_ssm_re, state_ssm_im, ln1_g, ln1_b, w_in, b_in, w_pool, pool_scale, lambda_re, lambda_im, log_dt, b_re, b_im, c_re, c_im, d_skip, w_glu, b_glu, w_proj_a, w_proj_b, w_out, ln2_g, ln2_b, w_ffn_gate, w_ffn_up, w_ffn_down, w_router, b_router, w_moe_gate, w_moe_up, w_moe_down):
    batch, seq, d_model = x_prompt.shape
    dec_batch = x_sample.shape[0]
    depth = w_in.shape[0]
    n_groups, n_state = lambda_re.shape[1], lambda_re.shape[2]
    chan = b_re.shape[-1]
    nst = n_groups * n_state
    pw = state_pool.shape[-1]
    alpha = (2.0 * depth) ** 0.25
    n_prompt = batch * seq
    n_tokens = n_prompt + dec_batch
    total_rows = -(-n_tokens // ROW_TILE) * ROW_TILE
    assert x_sample.shape[1] == 1 and n_prompt % ROW_TILE == 0 and seq % MIX_STEPS == 0
    assert batch * MIX_STEPS == ROW_TILE and dec_batch <= ROW_TILE

    rows_of = lambda a: a[:, None, :]
    a_re, a_im, bb_re, bb_im = _ssm_params(lambda_re, lambda_im, log_dt, b_re, b_im)
    lam = jnp.concatenate([a_re[::chan].reshape(depth, 1, nst), a_im[::chan].reshape(depth, 1, nst)], axis=2)
    wts = (w_in.astype(BF16), rows_of(b_in), w_pool.astype(BF16), rows_of(pool_scale), lam,
           _block_diag_in(bb_re, depth, n_groups, chan, n_state).astype(BF16),
           _block_diag_in(bb_im, depth, n_groups, chan, n_state).astype(BF16),
           _block_diag_out(c_re, depth, n_groups, chan, n_state).astype(BF16),
           _block_diag_out(c_im, depth, n_groups, chan, n_state).astype(BF16),
           d_skip.reshape(depth, 1, -1), w_glu.astype(BF16), rows_of(b_glu),
           w_proj_a.astype(BF16), w_proj_b.astype(BF16), w_out.astype(BF16),
           rows_of(ln1_g), rows_of(ln1_b))
    ffn_wts = (w_ffn_gate.astype(BF16), w_ffn_up.astype(BF16), w_ffn_down.astype(BF16))
    prev_s = jnp.transpose(state_pool, (0, 2, 1, 3)).reshape(depth, POOL_STATE * dec_batch, pw)
    h0_s = jnp.concatenate([state_ssm_re.reshape(depth, dec_batch, nst),
                            state_ssm_im.reshape(depth, dec_batch, nst)], axis=2)
    zero_prev = jnp.zeros((1, POOL_STATE * batch, pw), F32)
    zero_h = jnp.zeros((1, batch, 2 * nst), F32)

    x_p = x_prompt
    x_s = x_sample.reshape(dec_batch, d_model)
    s_row0 = 0
    outs = [[] for _ in range(6)]
    for l in range(depth):
        y_mix, pool_p, h_p = _token_mix(x_p, 0, 0, seq // MIX_STEPS, batch, MIX_STEPS, 0, alpha, l,
                                        zero_prev, zero_h, wts, total_rows, state_layer=0)
        y_mix, pool_s, h_s = _token_mix(x_s, s_row0, n_prompt, 1, dec_batch, 1, PAST_LEN, alpha, l,
                                        prev_s, h0_s, wts, total_rows, y_buf=y_mix)
        outs[0].append(jnp.transpose(pool_p.reshape(POOL_STATE, batch, pw), (1, 0, 2)))
        outs[1].append(h_p[:, :nst].reshape(batch, n_groups, n_state))
        outs[2].append(h_p[:, nst:].reshape(batch, n_groups, n_state))
        outs[3].append(jnp.transpose(pool_s.reshape(POOL_STATE, dec_batch, pw), (1, 0, 2)))
        outs[4].append(h_s[:, :nst].reshape(dec_batch, n_groups, n_state))
        outs[5].append(h_s[:, nst:].reshape(dec_batch, n_groups, n_state))
        j = l // 2
        if l % 2 == 0:
            y_all = _dense_ffn(y_mix, j, *ffn_wts, ln2_g[l][None, :], ln2_b[l][None, :], alpha)
        else:
            prompt_shape = (batch, seq) if l == depth - 1 else None
            y_all = _moe_ffn(y_mix, n_tokens, j, w_router[j], b_router[j], w_moe_gate, w_moe_up,
                             w_moe_down, ln2_g[l][None, :], ln2_b[l][None, :], alpha, prompt_shape)
        x_p = x_s = y_all
        s_row0 = n_prompt
    if isinstance(y_all, (tuple, list)):
        y_p, tail = y_all
        y_s = tail[:dec_batch].reshape(dec_batch, 1, d_model)
    else:
        y_p = jnp.transpose(y_all[:n_prompt].reshape(seq, batch, d_model), (1, 0, 2))
        y_s = y_all[n_prompt:n_tokens].reshape(dec_batch, 1, d_model)
    return (y_p, y_s, *[jnp.stack(o) for o in outs])
```

```python
import functools
import math

import jax
import jax.numpy as jnp
from jax import lax
from jax.experimental import pallas as pl
from jax.experimental.pallas import tpu as pltpu

F32 = jnp.float32
BF16 = jnp.bfloat16

POOL_WINDOWS = (2, 4, 8, 16)
POOL_STATE = max(POOL_WINDOWS) - 1
POOL_GROUP = 128
SSM_GROUP = 16
SSM_STATE = 64
N_EXPERTS = 8
TOP_K = 2
LN_EPS = 1e-5
PAST_LEN = 16384

LANES = 128
MXU_COLS = 256
ROW_TILE = 512
MIX_STEPS = 64
MIX_TILES_PER_STEP = 1
SCAN_COLS = 512
MOE_ROWS = 512
ITEM_BLOCKS = 5
MOE_FF_CHUNK = 896
GROUP = 16
STAGE_ROWS = TOP_K * ROW_TILE + N_EXPERTS * 2 * GROUP
VMEM_LIMIT = 56 * 1024 * 1024


def _dot(a, b):
    return jnp.dot(a, b, preferred_element_type=F32)


def _layer_norm(r, g, b):
    mu = jnp.mean(r, axis=-1, keepdims=True)
    d = r - mu
    var = jnp.mean(d * d, axis=-1, keepdims=True)
    return d * lax.rsqrt(var + LN_EPS) * g + b


def _swiglu(xb, gate_up_piece, n_pieces, wd_ref):
    pieces = []
    for p in range(n_pieces):
        gate, up = gate_up_piece(p)
        pieces.append((jax.nn.silu(gate) * up).astype(BF16))
    return _dot(jnp.concatenate(pieces, axis=-1), wd_ref[...])


def _swiglu_split(xb, wg_ref, wu_ref, wd_ref):
    def piece(p):
        cols = slice(p * MXU_COLS, (p + 1) * MXU_COLS)
        return _dot(xb, wg_ref[:, cols]), _dot(xb, wu_ref[:, cols])
    return _swiglu(xb, piece, wd_ref.shape[0] // MXU_COLS, wd_ref)


def _hidden_pieces(xb, wgu_piece, n_pieces):
    pieces = []
    for p in range(n_pieces):
        gate_up = _dot(xb, wgu_piece(p))
        pieces.append((jax.nn.silu(gate_up[:, :LANES]) * gate_up[:, LANES:]).astype(BF16))
    return pieces


def _params(*sem):
    return pltpu.CompilerParams(dimension_semantics=sem, vmem_limit_bytes=VMEM_LIMIT)


def _ssm_param_kernel(lre_ref, lim_ref, ldt_ref, bre_ref, bim_ref, are_ref, aim_ref, bbre_ref, bbim_ref):
    lre, lim = lre_ref[...], lim_ref[...]
    dt = jnp.exp(ldt_ref[...])
    mag = jnp.exp(lre * dt)
    a_re = mag * jnp.cos(lim * dt)
    a_im = mag * jnp.sin(lim * dt)
    den = lre * lre + lim * lim
    n_re, n_im = a_re - 1.0, a_im
    k_re = (n_re * lre + n_im * lim) / den
    k_im = (n_im * lre - n_re * lim) / den
    b_re, b_im = bre_ref[...], bim_ref[...]
    are_ref[...] = a_re
    aim_ref[...] = a_im
    bbre_ref[...] = k_re * b_re - k_im * b_im
    bbim_ref[...] = k_re * b_im + k_im * b_re


def _ssm_params(lambda_re, lambda_im, log_dt, b_re, b_im):
    depth, g, p = lambda_re.shape
    c = b_re.shape[-1]
    rep = lambda a: jnp.repeat(a.reshape(depth * g, p), c, axis=0)
    bt = lambda a: jnp.transpose(a, (0, 1, 3, 2)).reshape(depth * g * c, p)
    shape = jax.ShapeDtypeStruct((depth * g * c, p), F32)
    return pl.pallas_call(_ssm_param_kernel, out_shape=(shape,) * 4, name="s5_discretise")(
        rep(lambda_re), rep(lambda_im), rep(jnp.broadcast_to(log_dt[:, :, None], (depth, g, p))),
        bt(b_re), bt(b_im))


def _block_diag_in(bt, depth, g, c, p):
    m = g // 8
    a = bt.reshape(depth, m, 8, c, p)
    eye = jnp.eye(8, dtype=bt.dtype)
    return jnp.einsum('lmgcp,gh->lmgchp', a, eye).reshape(depth, m, 8 * c, 8 * p)


def _block_diag_out(cm, depth, g, c, p):
    m = g // 8
    a = jnp.transpose(cm, (0, 1, 3, 2)).reshape(depth, m, 8, p, c)
    eye = jnp.eye(8, dtype=cm.dtype)
    return jnp.einsum('lmgpc,gh->lmgphc', a, eye).reshape(depth, m, 8 * p, 8 * c)


def _mix_kernel(*refs, n_steps, zero_tail, n_sub, **kw):
    if not zero_tail:
        _mix_body(*refs, n_sub=n_sub, **kw)
        return
    step = pl.program_id(0)
    y_ref = refs[-(5 + n_sub)]

    @pl.when(step < n_steps)
    def _():
        _mix_body(*refs, n_sub=n_sub, **kw)

    @pl.when(step >= n_steps)
    def _():
        y_ref[...] = jnp.zeros(y_ref.shape, F32)


def _mix_body(x_ref, prev_ref, h0_ref, *rest, nb, tt, start_pos, alpha, aliased, n_sub):
    wts, rest = rest[:N_MIX_WEIGHTS], rest[N_MIX_WEIGHTS:]
    if aliased:
        rest = rest[1:]
    y_ref, pool_ref, hlast_ref, ext_ref, hs_ref, *bu_refs = rest
    rows = nb * tt
    halo = POOL_STATE * nb
    step = pl.program_id(0)

    @pl.when(step == 0)
    def _():
        ext_ref[0:halo, :] = prev_ref[...]
        hs_ref[...] = h0_ref[...]

    for sub in range(n_sub):
        if x_ref.ndim == 3:
            x = jnp.concatenate([x_ref[:, sub * tt + t, :] for t in range(tt)], axis=0)
        else:
            x = x_ref[sub * rows:(sub + 1) * rows, :]
        out = _mix_tile(x, step * n_sub + sub, *wts, pool_ref, hlast_ref, ext_ref, bu_refs[sub], hs_ref,
                        nb=nb, tt=tt, start_pos=start_pos, alpha=alpha)
        y_ref[sub * rows:(sub + 1) * rows, :] = out
    if y_ref.shape[0] != n_sub * rows:
        y_ref[n_sub * rows:, :] = jnp.zeros((y_ref.shape[0] - n_sub * rows, y_ref.shape[1]), F32)


N_MIX_WEIGHTS = 17


def _mix_tile(x, step, win_ref, bin_ref, wpool_ref, pscale_ref, lam_ref,
              bre_ref, bim_ref, cre_ref, cim_ref, dskip_ref, wglu_ref, bglu_ref,
              wpa_ref, wpb_ref, wout_ref, lng_ref, lnb_ref, pool_ref, hlast_ref, ext_ref, bu_ref, hs_ref,
              *, nb, tt, start_pos, alpha):
    rows = nb * tt
    halo = POOL_STATE * nb
    d_model = x.shape[-1]
    pw = pscale_ref.shape[1]
    nst = lam_ref.shape[1] // 2
    xb = x.astype(BF16)

    u_a = _dot(xb, win_ref[:, 0:pw]) + bin_ref[:, 0:pw]
    ext_ref[halo:halo + rows, :] = u_a
    t_idx = lax.shift_right_logical(lax.broadcasted_iota(jnp.int32, (rows, 1), 0), int(math.log2(nb)))
    pos1 = t_idx + (step * tt + start_pos + 1)
    mixed = []
    for gi, w in enumerate(POOL_WINDOWS):
        sl = slice(gi * POOL_GROUP, (gi + 1) * POOL_GROUP)
        cur = u_a[:, sl]
        acc = cur
        for k in range(1, w):
            acc = acc + ext_ref[halo - k * nb:halo - k * nb + rows, sl]
        count = jnp.minimum(pos1, w).astype(F32)
        pooled = acc / count - cur
        mixed.append(_dot(pooled.astype(BF16), wpool_ref[gi]))
    out_a = jnp.concatenate(mixed, axis=-1) * pscale_ref[...]
    new_halo = ext_ref[rows:rows + halo, :]
    ext_ref[0:halo, :] = new_halo
    pool_ref[...] = new_halo
    proj_a = _dot(out_a.astype(BF16), wpa_ref[...])

    u_b = _dot(xb, win_ref[:, pw:2 * pw]) + bin_ref[:, pw:2 * pw]
    ub16 = u_b.astype(BF16)
    n_m = bre_ref.shape[0]
    kin, kst = bre_ref.shape[1], bre_ref.shape[2]
    for m in range(n_m):
        um = ub16[:, m * kin:(m + 1) * kin]
        bu_ref[:, m * kst:(m + 1) * kst] = _dot(um, bre_ref[m])
        bu_ref[:, nst + m * kst:nst + (m + 1) * kst] = _dot(um, bim_ref[m])
    if tt == 1:
        a_re, a_im = lam_ref[:, 0:nst], lam_ref[:, nst:]
        h_re, h_im = hs_ref[:, 0:nst], hs_ref[:, nst:]
        n_re = a_re * h_re - a_im * h_im + bu_ref[:, 0:nst]
        n_im = a_re * h_im + a_im * h_re + bu_ref[:, nst:]
        bu_ref[:, 0:nst] = n_re
        bu_ref[:, nst:] = n_im
        hs_ref[:, 0:nst] = n_re
        hs_ref[:, nst:] = n_im
    else:
        for q in range(nst // SCAN_COLS):
            c_re = slice(q * SCAN_COLS, (q + 1) * SCAN_COLS)
            c_im = slice(nst + q * SCAN_COLS, nst + (q + 1) * SCAN_COLS)
            a_re = jnp.broadcast_to(lam_ref[:, c_re], (nb, SCAN_COLS))
            a_im = jnp.broadcast_to(lam_ref[:, c_im], (nb, SCAN_COLS))

            def body(t, carry, c_re=c_re, c_im=c_im, a_re=a_re, a_im=a_im):
                h_re, h_im = carry
                r = pl.ds(pl.multiple_of(t * nb, nb), nb)
                n_re = a_re * h_re - a_im * h_im + bu_ref[r, c_re]
                n_im = a_re * h_im + a_im * h_re + bu_ref[r, c_im]
                bu_ref[r, c_re] = n_re
                bu_ref[r, c_im] = n_im
                return n_re, n_im

            h_re, h_im = lax.fori_loop(0, tt, body, (hs_ref[:, c_re], hs_ref[:, c_im]), unroll=True)
            hs_ref[:, c_re] = h_re
            hs_ref[:, c_im] = h_im
    hlast_ref[...] = hs_ref[...]

    ys = []
    for m in range(n_m):
        h_re = bu_ref[:, m * kst:(m + 1) * kst].astype(BF16)
        h_im = bu_ref[:, nst + m * kst:nst + (m + 1) * kst].astype(BF16)
        ys.append(_dot(h_re, cre_ref[m]) - _dot(h_im, cim_ref[m]))
    y = jnp.concatenate(ys, axis=-1) + dskip_ref[...] * u_b
    y = jax.nn.gelu(y)
    out_b = y * jax.nn.sigmoid(_dot(y.astype(BF16), wglu_ref[...]) + bglu_ref[...])
    proj_b = _dot(out_b.astype(BF16), wpb_ref[...])

    z_ga = _dot(xb, win_ref[:, 2 * pw:2 * pw + d_model]) + bin_ref[:, 2 * pw:2 * pw + d_model]
    merged = jax.nn.sigmoid(z_ga) * proj_a
    z_gb = _dot(xb, win_ref[:, 2 * pw + d_model:]) + bin_ref[:, 2 * pw + d_model:]
    merged = merged + jax.nn.sigmoid(z_gb) * proj_b
    mix = _dot(merged.astype(BF16), wout_ref[...])
    return _layer_norm(alpha * x + mix, lng_ref[...], lnb_ref[...])


def _token_mix(x_src, x_row0, row0, n_steps, nb, tt, start_pos, alpha, layer, prev, h0, wts, total_rows,
               y_buf=None, state_layer=None):
    state_layer = layer if state_layer is None else state_layer
    rows = nb * tt
    d_model = x_src.shape[-1]
    nst2 = h0.shape[-1]
    pw = prev.shape[-1]
    halo = POOL_STATE * nb
    aliased = y_buf is not None
    n_sub = MIX_TILES_PER_STEP if n_steps % MIX_TILES_PER_STEP == 0 else 1
    n_steps, step_rows = n_steps // n_sub, rows * n_sub
    assert len(wts) == N_MIX_WEIGHTS
    const = lambda a: _layer_spec(a, layer)
    tile = lambda i: jnp.minimum(i, n_steps - 1)
    if x_src.ndim == 3:
        x_spec = pl.BlockSpec((nb, tt * n_sub, d_model), lambda i: (0, tile(i), 0))
    else:
        x_spec = pl.BlockSpec((step_rows, d_model), lambda i: (x_row0 // step_rows + tile(i), 0))
    in_specs = [x_spec, _layer_spec(prev, state_layer), _layer_spec(h0, state_layer)]
    in_specs += [const(w) for w in wts]
    args = [x_src, prev, h0, *wts]
    if aliased:
        in_specs.append(pl.BlockSpec(memory_space=pl.ANY))
        args.append(y_buf)
        y_spec = pl.BlockSpec((ROW_TILE, d_model), lambda i: (row0 // ROW_TILE, 0))
    else:
        y_spec = pl.BlockSpec((step_rows, d_model), lambda i: (row0 // step_rows + i, 0))
    out_shape = (jax.ShapeDtypeStruct((total_rows, d_model), F32),
                 jax.ShapeDtypeStruct((halo, pw), F32),
                 jax.ShapeDtypeStruct((nb, nst2), F32))
    out_specs = (y_spec,
                 pl.BlockSpec((halo, pw), lambda i: (0, 0)),
                 pl.BlockSpec((nb, nst2), lambda i: (0, 0)))
    zero_tail = not aliased
    assert aliased or (row0 == 0 and n_steps * step_rows < total_rows <= (n_steps + 1) * step_rows)
    kern = functools.partial(_mix_kernel, n_steps=n_steps, zero_tail=zero_tail, n_sub=n_sub,
                             nb=nb, tt=tt, start_pos=start_pos, alpha=alpha, aliased=aliased)
    return pl.pallas_call(
        kern, grid=(n_steps + int(zero_tail),), in_specs=in_specs, out_specs=out_specs, out_shape=out_shape,
        scratch_shapes=[pltpu.VMEM((halo + rows, pw), F32), pltpu.VMEM((nb, nst2), F32)]
        + [pltpu.VMEM((rows, nst2), F32) for _ in range(n_sub)],
        input_output_aliases={len(args) - 1: 0} if aliased else {},
        name="token_mix_decode" if aliased else "token_mix_prompt",
        compiler_params=_params("arbitrary"))(*args)


def _ffn_kernel(x_ref, wg_ref, wu_ref, wd_ref, g_ref, b_ref, o_ref, *, alpha):
    x = x_ref[...]
    ffn = _swiglu_split(x.astype(BF16), wg_ref, wu_ref, wd_ref)
    o_ref[...] = _layer_norm(alpha * x + ffn, g_ref[...], b_ref[...])


def _layer_spec(a, layer, **kw):
    return pl.BlockSpec((None,) + a.shape[1:], lambda *_, n=a.ndim - 1: (layer,) + (0,) * n, **kw)


def _dense_ffn(x_all, layer, wg, wu, wd, ln_g, ln_b, alpha):
    n, d_model = x_all.shape
    resident = lambda a: _layer_spec(a, layer, pipeline_mode=pl.Buffered(1))
    return pl.pallas_call(
        functools.partial(_ffn_kernel, alpha=alpha),
        grid=(n // ROW_TILE,),
        in_specs=[pl.BlockSpec((ROW_TILE, d_model), lambda i: (i, 0)),
                  resident(wg), resident(wu), resident(wd),
                  pl.BlockSpec((1, d_model), lambda i: (0, 0)), pl.BlockSpec((1, d_model), lambda i: (0, 0))],
        out_specs=pl.BlockSpec((ROW_TILE, d_model), lambda i: (i, 0)),
        out_shape=jax.ShapeDtypeStruct((n, d_model), F32),
        name="dense_swiglu_ln",
        compiler_params=_params("parallel"))(x_all, wg, wu, wd, ln_g, ln_b)


def _split_bf16(a):
    hi = a.astype(BF16)
    lo = (a - hi.astype(F32)).astype(BF16)
    return hi, lo


DEAD_RANK = -(1 << 20)


def _router_kernel(x_ref, w_ref, b_ref, idx_ref, gate_ref, cnt_ref, *, n_tokens):
    x_hi, x_lo = _split_bf16(x_ref[...])
    w_hi, w_lo = _split_bf16(w_ref[...])
    logits = _dot(x_hi, w_hi) + (_dot(x_lo, w_hi) + _dot(x_hi, w_lo)) + b_ref[...]
    lane = lax.broadcasted_iota(jnp.int32, logits.shape, 1)
    logits = jnp.where(lane < N_EXPERTS, logits, -jnp.inf)
    m1 = jnp.max(logits, axis=-1, keepdims=True)
    i1 = jnp.min(jnp.where(logits == m1, lane, LANES), axis=-1, keepdims=True)
    rest = jnp.where(lane == i1, -jnp.inf, logits)
    m2 = jnp.max(rest, axis=-1, keepdims=True)
    i2 = jnp.min(jnp.where(rest == m2, lane, LANES), axis=-1, keepdims=True)
    e2 = jnp.exp(m2 - m1)
    den = 1.0 + e2
    gate_ref[...] = jnp.where(lane == 0, 1.0 / den, e2 / den)

    row = lax.broadcasted_iota(jnp.int32, (ROW_TILE, 1), 0) + pl.program_id(0) * ROW_TILE
    live = row < n_tokens
    pick1 = jnp.where(jnp.logical_and(lane == i1, live), 1.0, 0.0)
    pick2 = jnp.where(jnp.logical_and(lane == i2, live), 1.0, 0.0)
    picks = pick1 + pick2
    r = lax.broadcasted_iota(jnp.int32, (ROW_TILE, ROW_TILE), 0)
    c = lax.broadcasted_iota(jnp.int32, (ROW_TILE, ROW_TILE), 1)
    earlier = _dot(jnp.where(c < r, 1.0, 0.0).astype(BF16), picks.astype(BF16))
    rank1 = jnp.sum(earlier * pick1, axis=-1, keepdims=True).astype(jnp.int32)
    rank2 = jnp.sum(earlier * pick2, axis=-1, keepdims=True).astype(jnp.int32)
    rank1 = jnp.where(live, rank1, DEAD_RANK)
    rank2 = jnp.where(live, rank2, DEAD_RANK)
    idx_ref[...] = jnp.where(lane == 0, i1, jnp.where(lane == 1, i2, jnp.where(lane == 2, rank1, rank2)))
    counts = jnp.sum(picks, axis=0, keepdims=True).astype(jnp.int32)
    cnt_ref[...] = jnp.broadcast_to(counts, cnt_ref.shape)


def _router(x_all, n_tokens, w_router, b_router):
    n, d_model = x_all.shape
    w = jnp.pad(w_router, ((0, 0), (0, LANES - N_EXPERTS)))
    b = jnp.pad(b_router, (0, LANES - N_EXPERTS))[None, :]
    n_tiles = n // ROW_TILE
    return pl.pallas_call(
        functools.partial(_router_kernel, n_tokens=n_tokens), grid=(n_tiles,),
        in_specs=[pl.BlockSpec((ROW_TILE, d_model), lambda i: (i, 0)),
                  pl.BlockSpec((d_model, LANES), lambda i: (0, 0)),
                  pl.BlockSpec((1, LANES), lambda i: (0, 0))],
        out_specs=(pl.BlockSpec((ROW_TILE, LANES), lambda i: (i, 0)),
                   pl.BlockSpec((ROW_TILE, LANES), lambda i: (i, 0)),
                   pl.BlockSpec((8, LANES), lambda i: (i, 0))),
        out_shape=(jax.ShapeDtypeStruct((n, LANES), jnp.int32), jax.ShapeDtypeStruct((n, LANES), F32),
                   jax.ShapeDtypeStruct((n_tiles * 8, LANES), jnp.int32)),
        name="moe_router",
        compiler_params=_params("parallel"))(x_all, w, b)


def _group_copy(src_ref, src_row, dst_ref, dst_row, sem):
    return pltpu.make_async_copy(src_ref.at[pl.ds(pl.multiple_of(src_row, GROUP), GROUP)],
                                 dst_ref.at[pl.ds(pl.multiple_of(dst_row, GROUP), GROUP)], sem)


def _wait_groups(count, src_ref, dst_ref, sem):
    def wait(_, carry):
        _group_copy(src_ref, 0, dst_ref, 0, sem).wait()
        return carry
    lax.fori_loop(0, count, wait, 0)


def _stage_rows(expert, rank, first_ref, step):
    pos = rank
    for e in range(N_EXPERTS):
        pos = pos + jnp.where(expert == e, first_ref[step * N_EXPERTS + e], 0)
    return pos


def _regroup_kernel(w0_ref, off_ref, ngrp_ref, part_ref, pend_ref, first_ref, x_ref, idx_ref, xs_hbm,
                    stage_ref, carry_ref, zero_ref, sem):
    step = pl.program_id(0)

    @pl.when(step == 0)
    def _():
        carry_ref[...] = jnp.zeros(carry_ref.shape, BF16)
        zero_ref[...] = jnp.zeros(zero_ref.shape, BF16)

    xb = x_ref[...].astype(BF16)
    stage_row = lax.broadcasted_iota(jnp.int32, (STAGE_ROWS, ROW_TILE), 0)
    idx = idx_ref[...]
    hit = None
    for choice in range(TOP_K):
        pos = _stage_rows(idx[choice:choice + 1, :], idx[TOP_K + choice:TOP_K + choice + 1, :], first_ref, step)
        hit = stage_row == pos if hit is None else jnp.logical_or(hit, stage_row == pos)
    stage_ref[...] = _dot(jnp.where(hit, 1.0, 0.0).astype(BF16), xb).astype(BF16)

    started = 0
    for e in range(N_EXPERTS):
        k = step * N_EXPERTS + e
        off, w0, ngrp = off_ref[k], w0_ref[k], ngrp_ref[k]
        head = pl.ds(pl.multiple_of(off, GROUP), GROUP)
        stage_ref[head, :] = stage_ref[head, :] + carry_ref[e]

        def send(q, carry, off=off, w0=w0):
            _group_copy(stage_ref, off + q * GROUP, xs_hbm, w0 + q * GROUP, sem).start()
            return carry

        lax.fori_loop(0, ngrp, send, 0)
        tail = stage_ref[pl.ds(pl.multiple_of(off + ngrp * GROUP, GROUP), GROUP), :]
        carry_ref[e] = jnp.where(part_ref[k] != 0, tail, jnp.zeros_like(tail))
        started = started + ngrp
    _wait_groups(started, stage_ref, xs_hbm, sem)

    @pl.when(step == pl.num_programs(0) - 1)
    def _():
        flushed = 0
        for e in range(N_EXPERTS):
            k = step * N_EXPERTS + e
            row = w0_ref[k] + ngrp_ref[k] * GROUP
            is_open = part_ref[k]

            @pl.when(is_open != 0)
            def _(row=row, e=e):
                pltpu.make_async_copy(carry_ref.at[e], xs_hbm.at[pl.ds(pl.multiple_of(row, GROUP), GROUP)],
                                      sem).start()

            row = row + is_open * GROUP
            n_zero = (pend_ref[e] - row) // GROUP

            def fill(q, carry, row=row):
                _group_copy(zero_ref, 0, xs_hbm, row + q * GROUP, sem).start()
                return carry

            lax.fori_loop(0, n_zero, fill, 0)
            flushed = flushed + is_open + n_zero
        _wait_groups(flushed, zero_ref, xs_hbm, sem)


def _regroup(x_all, idx_rows, tables, n_slots):
    n, d_model = x_all.shape
    return pl.pallas_call(
        _regroup_kernel,
        grid_spec=pltpu.PrefetchScalarGridSpec(
            num_scalar_prefetch=6, grid=(n // ROW_TILE,),
            in_specs=[pl.BlockSpec((ROW_TILE, d_model), lambda i, *_: (i, 0)),
                      pl.BlockSpec((2 * TOP_K, ROW_TILE), lambda i, *_: (0, i))],
            out_specs=pl.BlockSpec(memory_space=pl.ANY),
            scratch_shapes=[pltpu.VMEM((STAGE_ROWS, d_model), BF16),
                            pltpu.VMEM((N_EXPERTS, GROUP, d_model), BF16),
                            pltpu.VMEM((GROUP, d_model), BF16),
                            pltpu.SemaphoreType.DMA(())]),
        out_shape=jax.ShapeDtypeStruct((n_slots, d_model), BF16),
        name="moe_regroup",
        compiler_params=_params("arbitrary"))(
            tables["w0"], tables["off"], tables["ngrp"], tables["part"], tables["pend"], tables["first"],
            x_all, idx_rows)


def _moe_kernel(item_e_ref, item_b0_ref, item_nb_ref, item_half_ref, nvalid_ref, *refs):
    xs_refs = refs[:ITEM_BLOCKS]
    wg_ref, wu_ref, wd_ref, ys_hbm = refs[ITEM_BLOCKS:ITEM_BLOCKS + 4]
    *wgub_refs, wdcat_ref, htail_ref, acc_ref, stage_ref, sem = refs[ITEM_BLOCKS + 4:]
    fc = wd_ref.shape[0]
    item, c = pl.program_id(0), pl.program_id(1)
    last_item, last_c = pl.num_programs(0) - 1, pl.num_programs(1) - 1
    n_blocks = ys_hbm.shape[0] // MOE_ROWS
    nb, b0 = item_nb_ref[item], item_b0_ref[item]

    def block_out(src_ref, block):
        rows = pl.ds(pl.multiple_of(block * MOE_ROWS, MOE_ROWS), MOE_ROWS)
        return pltpu.make_async_copy(src_ref, ys_hbm.at[rows], sem)

    def start_and_wait(count, copy_of):
        def start(q, carry):
            copy_of(q).start()
            return carry

        def wait(q, carry):
            copy_of(q).wait()
            return carry

        lax.fori_loop(0, count, start, 0)
        lax.fori_loop(0, count, wait, 0)

    @pl.when(jnp.logical_and(item == 0, c == 0))
    def _():
        acc_ref[...] = jnp.zeros(acc_ref.shape, F32)

    def cast_then_load(p):
        cols = slice(p * LANES, (p + 1) * LANES)
        wgub_refs[p][:, :LANES] = wg_ref[:, cols].astype(BF16)
        wgub_refs[p][:, LANES:] = wu_ref[:, cols].astype(BF16)
        return wgub_refs[p][...]

    def run_block(j, n_rows, odd):
        if j == 0:
            wdcat_ref[LANES:, :] = wd_ref[...].astype(BF16)
            if not odd:
                wdcat_ref[0:LANES, :] = wd_ref[fc - LANES:, :].astype(BF16)
        piece = cast_then_load if j == 0 else (lambda p: wgub_refs[p][...])
        pieces = _hidden_pieces(xs_refs[j][0:n_rows, :], piece, fc // LANES)
        if odd:
            hidden = jnp.concatenate([htail_ref[j, 0:n_rows, :]] + pieces, axis=-1)
            part = _dot(hidden, wdcat_ref[...])
        else:
            htail_ref[j, 0:n_rows, :] = pieces[-1]
            part = _dot(jnp.concatenate(pieces[:-1], axis=-1), wdcat_ref[LANES:fc, :])
        total = jnp.where(c == 0, part, acc_ref[j, 0:n_rows, :] + part)
        acc_ref[j, 0:n_rows, :] = total
        stage_ref[j, 0:n_rows, :] = total.astype(BF16)
        if n_rows < MOE_ROWS:
            stage_ref[j, n_rows:, :] = jnp.zeros((MOE_ROWS - n_rows, stage_ref.shape[2]), BF16)

    for j in range(ITEM_BLOCKS):
        half = jnp.logical_and(j == nb - 1, item_half_ref[item] != 0)
        full = jnp.logical_and(j < nb, jnp.logical_not(half))
        for odd in (False, True):
            parity = lax.rem(c, 2) == int(odd)
            pl.when(jnp.logical_and(full, parity))(functools.partial(run_block, j, MOE_ROWS, odd))
            pl.when(jnp.logical_and(half, parity))(functools.partial(run_block, j, MOE_ROWS // 2, odd))

    @pl.when(c == last_c)
    def _():
        start_and_wait(nb, lambda q: block_out(stage_ref.at[q], b0 + q))

    @pl.when(jnp.logical_and(item == last_item, c == last_c))
    def _():
        stage_ref[0] = jnp.zeros(stage_ref.shape[1:], BF16)
        first_unused = nvalid_ref[0]
        start_and_wait(n_blocks - first_unused, lambda q: block_out(stage_ref.at[0], first_unused + q))


def _grouped_swiglu(xs, items, nvalid, layer, wg, wu, wd):
    n_slots, d_model = xs.shape
    d_ff = wg.shape[3]
    fc = MOE_FF_CHUNK
    n_chunks = d_ff // fc
    n_blocks = n_slots // MOE_ROWS
    item_e, item_b0, item_nb, item_half = items
    assert d_ff % fc == 0 and n_chunks % 2 == 0 and (fc - LANES) % MXU_COLS == 0 and n_slots % MOE_ROWS == 0
    chunk = lambda c, nb, it: jnp.where(nb[it] > 0, c, n_chunks - 1)
    w_in_map = lambda it, c, e, b0, nb, *_: (layer, e[it], 0, chunk(c, nb, it))
    w_out_map = lambda it, c, e, b0, nb, *_: (layer, e[it], chunk(c, nb, it), 0)
    xs_spec = lambda k: pl.BlockSpec(
        (MOE_ROWS, d_model), lambda it, c, e, b0, *_: (jnp.minimum(b0[it] + k, n_blocks - 1), 0))
    return pl.pallas_call(
        _moe_kernel,
        grid_spec=pltpu.PrefetchScalarGridSpec(
            num_scalar_prefetch=5, grid=(item_e.shape[0], n_chunks),
            in_specs=[xs_spec(k) for k in range(ITEM_BLOCKS)] + [
                pl.BlockSpec((None, None, d_model, fc), w_in_map),
                pl.BlockSpec((None, None, d_model, fc), w_in_map),
                pl.BlockSpec((None, None, fc, d_model), w_out_map)],
            out_specs=pl.BlockSpec(memory_space=pl.ANY),
            scratch_shapes=[pltpu.VMEM((d_model, 2 * LANES), BF16) for _ in range(fc // LANES)] + [
                            pltpu.VMEM((LANES + fc, d_model), BF16),
                            pltpu.VMEM((ITEM_BLOCKS, MOE_ROWS, LANES), BF16),
                            pltpu.VMEM((ITEM_BLOCKS, MOE_ROWS, d_model), F32),
                            pltpu.VMEM((ITEM_BLOCKS, MOE_ROWS, d_model), BF16),
                            pltpu.SemaphoreType.DMA(())]),
        out_shape=jax.ShapeDtypeStruct((n_slots, d_model), BF16),
        name="moe_grouped_swiglu",
        compiler_params=_params("arbitrary", "arbitrary"))(
            item_e, item_b0, item_nb, item_half, nvalid, *([xs] * ITEM_BLOCKS), wg, wu, wd)


def _combine_kernel(w0_ref, off_ref, nwin_ref, first_ref, x_ref, idx_ref, gate_ref, g_ref, b_ref, ys_hbm,
                    *rest, alpha, batch_major):
    if batch_major:
        prompt_ref, tail_ref, stage_ref, sem = rest
    else:
        o_ref, stage_ref, sem = rest
    step = pl.program_id(0)

    def fetch_tile(tile, buf):
        stage_ref[buf] = jnp.zeros(stage_ref.shape[1:], BF16)
        for e in range(N_EXPERTS):
            k = tile * N_EXPERTS + e
            off, w0 = off_ref[k], w0_ref[k]

            def fetch(q, carry, off=off, w0=w0):
                _group_copy(ys_hbm, w0 + q * GROUP, stage_ref.at[buf], off + q * GROUP, sem.at[buf]).start()
                return carry

            lax.fori_loop(0, nwin_ref[k], fetch, 0)

    @pl.when(step == 0)
    def _():
        fetch_tile(step, 0)

    buf = lax.rem(step, 2)

    @pl.when(step + 1 < pl.num_programs(0))
    def _():
        fetch_tile(step + 1, 1 - buf)

    n_groups = 0
    for e in range(N_EXPERTS):
        n_groups = n_groups + nwin_ref[step * N_EXPERTS + e]
    _wait_groups(n_groups, ys_hbm, stage_ref.at[buf], sem.at[buf])

    stage_col = lax.broadcasted_iota(jnp.int32, (ROW_TILE, STAGE_ROWS), 1)
    idx = idx_ref[...]
    gates = gate_ref[...]
    ys = stage_ref[buf]
    f = None
    for choice in range(TOP_K):
        pos = _stage_rows(idx[:, choice:choice + 1], idx[:, TOP_K + choice:TOP_K + choice + 1], first_ref, step)
        pick = jnp.where(stage_col == pos, 1.0, 0.0).astype(BF16)
        term = gates[:, choice:choice + 1] * _dot(pick, ys)
        f = term if f is None else f + term
    out = _layer_norm(alpha * x_ref[...] + f, g_ref[...], b_ref[...])
    if not batch_major:
        o_ref[...] = out
        return
    nb, tt = prompt_ref.shape[0], prompt_ref.shape[1]
    last = pl.num_programs(0) - 1

    @pl.when(step < last)
    def _():
        for t in range(tt):
            prompt_ref[:, t, :] = out[t * nb:(t + 1) * nb, :]

    @pl.when(step == last)
    def _():
        tail_ref[...] = out


def _combine(x_all, idx, gates, tables, ys, ln_g, ln_b, alpha, prompt_shape=None):
    n, d_model = x_all.shape
    n_tiles = n // ROW_TILE
    if prompt_shape is None:
        out_specs = pl.BlockSpec((ROW_TILE, d_model), lambda i, *_: (i, 0))
        out_shape = jax.ShapeDtypeStruct((n, d_model), F32)
    else:
        batch, seq = prompt_shape
        tt = ROW_TILE // batch
        assert batch * seq == (n_tiles - 1) * ROW_TILE
        out_specs = (pl.BlockSpec((batch, tt, d_model), lambda i, *_: (0, jnp.minimum(i, n_tiles - 2), 0)),
                     pl.BlockSpec((ROW_TILE, d_model), lambda i, *_: (0, 0)))
        out_shape = (jax.ShapeDtypeStruct((batch, seq, d_model), F32),
                     jax.ShapeDtypeStruct((ROW_TILE, d_model), F32))
    return pl.pallas_call(
        functools.partial(_combine_kernel, alpha=alpha, batch_major=prompt_shape is not None),
        grid_spec=pltpu.PrefetchScalarGridSpec(
            num_scalar_prefetch=4, grid=(n_tiles,),
            in_specs=[pl.BlockSpec((ROW_TILE, d_model), lambda i, *_: (i, 0)),
                      pl.BlockSpec((ROW_TILE, LANES), lambda i, *_: (i, 0)),
                      pl.BlockSpec((ROW_TILE, LANES), lambda i, *_: (i, 0)),
                      pl.BlockSpec((1, d_model), lambda i, *_: (0, 0)),
                      pl.BlockSpec((1, d_model), lambda i, *_: (0, 0)),
                      pl.BlockSpec(memory_space=pl.ANY)],
            out_specs=out_specs,
            scratch_shapes=[pltpu.VMEM((2, STAGE_ROWS, d_model), BF16), pltpu.SemaphoreType.DMA((2,))]),
        out_shape=out_shape,
        name="moe_combine_ln",
        compiler_params=_params("arbitrary"))(
            tables["w0"], tables["off"], tables["nwin"], tables["first"], x_all, idx, gates, ln_g, ln_b, ys)


def _routing_tables(counts, n_slots):
    count = jnp.sum(counts, axis=0)
    blocks = (count + MOE_ROWS - 1) // MOE_ROWS
    blk_end = jnp.cumsum(blocks)
    base = (blk_end - blocks) * MOE_ROWS
    n_items = n_slots // MOE_ROWS // ITEM_BLOCKS + N_EXPERTS
    items_e = (blocks + ITEM_BLOCKS - 1) // ITEM_BLOCKS
    item_end = jnp.cumsum(items_e)
    n_used = item_end[-1]
    item = jnp.arange(n_items, dtype=jnp.int32)
    used_item = jnp.minimum(item, n_used - 1)
    item_e = jnp.minimum(jnp.sum((used_item[:, None] >= item_end[None, :]).astype(jnp.int32), axis=1),
                         N_EXPERTS - 1)
    of_item = lambda per_expert: jnp.sum(
        jnp.where(item_e[:, None] == jnp.arange(N_EXPERTS)[None, :], per_expert[None, :], 0), axis=1)
    local = used_item - of_item(item_end - items_e)
    item_b0 = of_item(blk_end - blocks) + ITEM_BLOCKS * local
    item_nb = jnp.where(item < n_used, jnp.clip(of_item(blocks) - ITEM_BLOCKS * local, 0, ITEM_BLOCKS), 0)
    tail_rows = count - (blocks - 1) * MOE_ROWS
    item_half = jnp.logical_and(jnp.logical_and(item < n_used, local == of_item(items_e) - 1),
                                of_item(tail_rows) <= MOE_ROWS // 2)
    items = tuple(a.astype(jnp.int32) for a in (item_e, item_b0, item_nb, item_half))
    start = base[None, :] + jnp.cumsum(counts, axis=0) - counts
    end = start + counts
    w0 = start // GROUP * GROUP
    ngrp = end // GROUP - start // GROUP
    part = (end % GROUP != 0).astype(jnp.int32)
    nwin = ngrp + part
    off = (jnp.cumsum(nwin, axis=1) - nwin) * GROUP
    nvalid = blk_end[-1]
    pend = (blk_end * MOE_ROWS).at[-1].set(n_slots)
    flat = lambda a: a.reshape(-1).astype(jnp.int32)
    tables = dict(w0=flat(w0), off=flat(off), first=flat(start - w0 + off), ngrp=flat(ngrp), part=flat(part),
                  nwin=flat(nwin), pend=flat(pend))
    return tables, items, nvalid.reshape(1).astype(jnp.int32)


def _moe_ffn(x_all, n_tokens, layer, w_router, b_router, wg, wu, wd, ln_g, ln_b, alpha, prompt_shape=None):
    idx, gates, cnt = _router(x_all, n_tokens, w_router, b_router)
    n_slots = TOP_K * n_tokens + N_EXPERTS * MOE_ROWS
    n_slots = -(-n_slots // (MOE_ROWS * ITEM_BLOCKS)) * (MOE_ROWS * ITEM_BLOCKS)
    tables, items, nvalid = _routing_tables(cnt[::8, :N_EXPERTS], n_slots)
    xs = _regroup(x_all, idx[:, :2 * TOP_K].T, tables, n_slots)
    ys = _grouped_swiglu(xs, items, nvalid, layer, wg, wu, wd)
    return _combine(x_all, idx, gates, tables, ys, ln_g, ln_b, alpha, prompt_shape)


def kernel(x_prompt, x_sample, state_pool, state_ssm_re, state_ssm_im, ln1_g, ln1_b, w_in, b_in, w_pool, pool_scale, lambda_re, lambda_im, log_dt, b_re, b_im, c_re, c_im, d_skip, w_glu, b_glu, w_proj_a, w_proj_b, w_out, ln2_g, ln2_b, w_ffn_gate, w_ffn_up, w_ffn_down, w_router, b_router, w_moe_gate, w_moe_up, w_moe_down):
    batch, seq, d_model = x_prompt.shape
    dec_batch = x_sample.shape[0]
    depth = w_in.shape[0]
    n_groups, n_state = lambda_re.shape[1], lambda_re.shape[2]
    chan = b_re.shape[-1]
    nst = n_groups * n_state
    pw = state_pool.shape[-1]
    alpha = (2.0 * depth) ** 0.25
    n_prompt = batch * seq
    n_tokens = n_prompt + dec_batch
    total_rows = -(-n_tokens // ROW_TILE) * ROW_TILE
    assert x_sample.shape[1] == 1 and n_prompt % ROW_TILE == 0 and seq % MIX_STEPS == 0
    assert batch * MIX_STEPS == ROW_TILE and dec_batch <= ROW_TILE

    rows_of = lambda a: a[:, None, :]
    a_re, a_im, bb_re, bb_im = _ssm_params(lambda_re, lambda_im, log_dt, b_re, b_im)
    lam = jnp.concatenate([a_re[::chan].reshape(depth, 1, nst), a_im[::chan].reshape(depth, 1, nst)], axis=2)
    wts = (w_in.astype(BF16), rows_of(b_in), w_pool.astype(BF16), rows_of(pool_scale), lam,
           _block_diag_in(bb_re, depth, n_groups, chan, n_state).astype(BF16),
           _block_diag_in(bb_im, depth, n_groups, chan, n_state).astype(BF16),
           _block_diag_out(c_re, depth, n_groups, chan, n_state).astype(BF16),
           _block_diag_out(c_im, depth, n_groups, chan, n_state).astype(BF16),
           d_skip.reshape(depth, 1, -1), w_glu.astype(BF16), rows_of(b_glu),
           w_proj_a.astype(BF16), w_proj_b.astype(BF16), w_out.astype(BF16),
           rows_of(ln1_g), rows_of(ln1_b))
    ffn_wts = (w_ffn_gate.astype(BF16), w_ffn_up.astype(BF16), w_ffn_down.astype(BF16))
    prev_s = jnp.transpose(state_pool, (0, 2, 1, 3)).reshape(depth, POOL_STATE * dec_batch, pw)
    h0_s = jnp.concatenate([state_ssm_re.reshape(depth, dec_batch, nst),
                            state_ssm_im.reshape(depth, dec_batch, nst)], axis=2)
    zero_prev = jnp.zeros((1, POOL_STATE * batch, pw), F32)
    zero_h = jnp.zeros((1, batch, 2 * nst), F32)

    x_p = x_prompt
    x_s = x_sample.reshape(dec_batch, d_model)
    s_row0 = 0
    outs = [[] for _ in range(6)]
    for l in range(depth):
        y_mix, pool_p, h_p = _token_mix(x_p, 0, 0, seq // MIX_STEPS, batch, MIX_STEPS, 0, alpha, l,
                                        zero_prev, zero_h, wts, total_rows, state_layer=0)
        y_mix, pool_s, h_s = _token_mix(x_s, s_row0, n_prompt, 1, dec_batch, 1, PAST_LEN, alpha, l,
                                        prev_s, h0_s, wts, total_rows, y_buf=y_mix)
        outs[0].append(jnp.transpose(pool_p.reshape(POOL_STATE, batch, pw), (1, 0, 2)))
        outs[1].append(h_p[:, :nst].reshape(batch, n_groups, n_state))
        outs[2].append(h_p[:, nst:].reshape(batch, n_groups, n_state))
        outs[3].append(jnp.transpose(pool_s.reshape(POOL_STATE, dec_batch, pw), (1, 0, 2)))
        outs[4].append(h_s[:, :nst].reshape(dec_batch, n_groups, n_state))
        outs[5].append(h_s[:, nst:].reshape(dec_batch, n_groups, n_state))
        j = l // 2
        if l % 2 == 0:
            y_all = _dense_ffn(y_mix, j, *ffn_wts, ln2_g[l][None, :], ln2_b[l][None, :], alpha)
        else:
            prompt_shape = (batch, seq) if l == depth - 1 else None
            y_all = _moe_ffn(y_mix, n_tokens, j, w_router[j], b_router[j], w_moe_gate, w_moe_up,
                             w_moe_down, ln2_g[l][None, :], ln2_b[l][None, :], alpha, prompt_shape)
        x_p = x_s = y_all
        s_row0 = n_prompt
    if isinstance(y_all, (tuple, list)):
        y_p, tail = y_all
        y_s = tail[:dec_batch].reshape(dec_batch, 1, d_model)
    else:
        y_p = jnp.transpose(y_all[:n_prompt].reshape(seq, batch, d_model), (1, 0, 2))
        y_s = y_all[n_prompt:n_tokens].reshape(dec_batch, 1, d_model)
    return (y_p, y_s, *[jnp.stack(o) for o in outs])
```

```python
import functools
import math

import jax
import jax.numpy as jnp
from jax import lax
from jax.experimental import pallas as pl
from jax.experimental.pallas import tpu as pltpu

F32 = jnp.float32
BF16 = jnp.bfloat16

POOL_WINDOWS = (2, 4, 8, 16)
POOL_STATE = max(POOL_WINDOWS) - 1
POOL_GROUP = 128
SSM_GROUP = 16
SSM_STATE = 64
N_EXPERTS = 8
TOP_K = 2
LN_EPS = 1e-5
PAST_LEN = 16384

LANES = 128
MXU_COLS = 256
ROW_TILE = 512
MIX_STEPS = 64
SCAN_COLS = 512
MOE_ROWS = 512
ITEM_BLOCKS = 3
MOE_FF_CHUNK = 896
GROUP = 16
STAGE_ROWS = TOP_K * ROW_TILE + N_EXPERTS * 2 * GROUP
VMEM_LIMIT = 56 * 1024 * 1024


def _dot(a, b):
    return jnp.dot(a, b, preferred_element_type=F32)


def _layer_norm(r, g, b):
    mu = jnp.mean(r, axis=-1, keepdims=True)
    d = r - mu
    var = jnp.mean(d * d, axis=-1, keepdims=True)
    return d * lax.rsqrt(var + LN_EPS) * g + b


def _swiglu(xb, gate_up_piece, n_pieces, wd_ref):
    pieces = []
    for p in range(n_pieces):
        gate, up = gate_up_piece(p)
        pieces.append((jax.nn.silu(gate) * up).astype(BF16))
    return _dot(jnp.concatenate(pieces, axis=-1), wd_ref[...])


def _swiglu_split(xb, wg_ref, wu_ref, wd_ref):
    def piece(p):
        cols = slice(p * MXU_COLS, (p + 1) * MXU_COLS)
        return _dot(xb, wg_ref[:, cols]), _dot(xb, wu_ref[:, cols])
    return _swiglu(xb, piece, wd_ref.shape[0] // MXU_COLS, wd_ref)


def _hidden_pieces(xb, wgu_piece, n_pieces):
    pieces = []
    for p in range(n_pieces):
        gate_up = _dot(xb, wgu_piece(p))
        pieces.append((jax.nn.silu(gate_up[:, :LANES]) * gate_up[:, LANES:]).astype(BF16))
    return pieces


def _params(*sem):
    return pltpu.CompilerParams(dimension_semantics=sem, vmem_limit_bytes=VMEM_LIMIT)


def _ssm_param_kernel(lre_ref, lim_ref, ldt_ref, bre_ref, bim_ref, are_ref, aim_ref, bbre_ref, bbim_ref):
    lre, lim = lre_ref[...], lim_ref[...]
    dt = jnp.exp(ldt_ref[...])
    mag = jnp.exp(lre * dt)
    a_re = mag * jnp.cos(lim * dt)
    a_im = mag * jnp.sin(lim * dt)
    den = lre * lre + lim * lim
    n_re, n_im = a_re - 1.0, a_im
    k_re = (n_re * lre + n_im * lim) / den
    k_im = (n_im * lre - n_re * lim) / den
    b_re, b_im = bre_ref[...], bim_ref[...]
    are_ref[...] = a_re
    aim_ref[...] = a_im
    bbre_ref[...] = k_re * b_re - k_im * b_im
    bbim_ref[...] = k_re * b_im + k_im * b_re


def _ssm_params(lambda_re, lambda_im, log_dt, b_re, b_im):
    depth, g, p = lambda_re.shape
    c = b_re.shape[-1]
    rep = lambda a: jnp.repeat(a.reshape(depth * g, p), c, axis=0)
    bt = lambda a: jnp.transpose(a, (0, 1, 3, 2)).reshape(depth * g * c, p)
    shape = jax.ShapeDtypeStruct((depth * g * c, p), F32)
    return pl.pallas_call(_ssm_param_kernel, out_shape=(shape,) * 4, name="s5_discretise")(
        rep(lambda_re), rep(lambda_im), rep(jnp.broadcast_to(log_dt[:, :, None], (depth, g, p))),
        bt(b_re), bt(b_im))


def _block_diag_in(bt, depth, g, c, p):
    m = g // 8
    a = bt.reshape(depth, m, 8, c, p)
    eye = jnp.eye(8, dtype=bt.dtype)
    return jnp.einsum('lmgcp,gh->lmgchp', a, eye).reshape(depth, m, 8 * c, 8 * p)


def _block_diag_out(cm, depth, g, c, p):
    m = g // 8
    a = jnp.transpose(cm, (0, 1, 3, 2)).reshape(depth, m, 8, p, c)
    eye = jnp.eye(8, dtype=cm.dtype)
    return jnp.einsum('lmgpc,gh->lmgphc', a, eye).reshape(depth, m, 8 * p, 8 * c)


def _mix_kernel(*refs, n_steps, zero_tail, **kw):
    if not zero_tail:
        _mix_body(*refs, **kw)
        return
    step = pl.program_id(0)
    y_ref = refs[-6]

    @pl.when(step < n_steps)
    def _():
        _mix_body(*refs, **kw)

    @pl.when(step >= n_steps)
    def _():
        y_ref[...] = jnp.zeros(y_ref.shape, F32)


def _mix_body(x_ref, prev_ref, h0_ref, win_ref, bin_ref, wpool_ref, pscale_ref, lam_ref,
              bre_ref, bim_ref, cre_ref, cim_ref, dskip_ref, wglu_ref, bglu_ref,
              wpa_ref, wpb_ref, wout_ref, lng_ref, lnb_ref, *rest,
              nb, tt, start_pos, alpha, aliased):
    if aliased:
        rest = rest[1:]
    y_ref, pool_ref, hlast_ref, ext_ref, bu_ref, hs_ref = rest
    rows = nb * tt
    halo = POOL_STATE * nb
    d_model = x_ref.shape[-1]
    pw = pscale_ref.shape[1]
    nst = lam_ref.shape[1] // 2
    step = pl.program_id(0)

    @pl.when(step == 0)
    def _():
        ext_ref[0:halo, :] = prev_ref[...]
        hs_ref[...] = h0_ref[...]

    if x_ref.ndim == 3:
        x = jnp.concatenate([x_ref[:, t, :] for t in range(tt)], axis=0)
    else:
        x = x_ref[...]
    xb = x.astype(BF16)

    u_a = _dot(xb, win_ref[:, 0:pw]) + bin_ref[:, 0:pw]
    ext_ref[halo:halo + rows, :] = u_a
    t_idx = lax.shift_right_logical(lax.broadcasted_iota(jnp.int32, (rows, 1), 0), int(math.log2(nb)))
    pos1 = t_idx + (step * tt + start_pos + 1)
    mixed = []
    for gi, w in enumerate(POOL_WINDOWS):
        sl = slice(gi * POOL_GROUP, (gi + 1) * POOL_GROUP)
        cur = u_a[:, sl]
        acc = cur
        for k in range(1, w):
            acc = acc + ext_ref[halo - k * nb:halo - k * nb + rows, sl]
        count = jnp.minimum(pos1, w).astype(F32)
        pooled = acc / count - cur
        mixed.append(_dot(pooled.astype(BF16), wpool_ref[gi]))
    out_a = jnp.concatenate(mixed, axis=-1) * pscale_ref[...]
    new_halo = ext_ref[rows:rows + halo, :]
    ext_ref[0:halo, :] = new_halo
    pool_ref[...] = new_halo
    proj_a = _dot(out_a.astype(BF16), wpa_ref[...])

    u_b = _dot(xb, win_ref[:, pw:2 * pw]) + bin_ref[:, pw:2 * pw]
    ub16 = u_b.astype(BF16)
    n_m = bre_ref.shape[0]
    kin, kst = bre_ref.shape[1], bre_ref.shape[2]
    for m in range(n_m):
        um = ub16[:, m * kin:(m + 1) * kin]
        bu_ref[:, m * kst:(m + 1) * kst] = _dot(um, bre_ref[m])
        bu_ref[:, nst + m * kst:nst + (m + 1) * kst] = _dot(um, bim_ref[m])
    if tt == 1:
        a_re, a_im = lam_ref[:, 0:nst], lam_ref[:, nst:]
        h_re, h_im = hs_ref[:, 0:nst], hs_ref[:, nst:]
        n_re = a_re * h_re - a_im * h_im + bu_ref[:, 0:nst]
        n_im = a_re * h_im + a_im * h_re + bu_ref[:, nst:]
        bu_ref[:, 0:nst] = n_re
        bu_ref[:, nst:] = n_im
        hs_ref[:, 0:nst] = n_re
        hs_ref[:, nst:] = n_im
    else:
        for q in range(nst // SCAN_COLS):
            c_re = slice(q * SCAN_COLS, (q + 1) * SCAN_COLS)
            c_im = slice(nst + q * SCAN_COLS, nst + (q + 1) * SCAN_COLS)
            a_re = jnp.broadcast_to(lam_ref[:, c_re], (nb, SCAN_COLS))
            a_im = jnp.broadcast_to(lam_ref[:, c_im], (nb, SCAN_COLS))

            def body(t, carry, c_re=c_re, c_im=c_im, a_re=a_re, a_im=a_im):
                h_re, h_im = carry
                r = pl.ds(pl.multiple_of(t * nb, nb), nb)
                n_re = a_re * h_re - a_im * h_im + bu_ref[r, c_re]
                n_im = a_re * h_im + a_im * h_re + bu_ref[r, c_im]
                bu_ref[r, c_re] = n_re
                bu_ref[r, c_im] = n_im
                return n_re, n_im

            h_re, h_im = lax.fori_loop(0, tt, body, (hs_ref[:, c_re], hs_ref[:, c_im]), unroll=True)
            hs_ref[:, c_re] = h_re
            hs_ref[:, c_im] = h_im
    hlast_ref[...] = hs_ref[...]

    ys = []
    for m in range(n_m):
        h_re = bu_ref[:, m * kst:(m + 1) * kst].astype(BF16)
        h_im = bu_ref[:, nst + m * kst:nst + (m + 1) * kst].astype(BF16)
        ys.append(_dot(h_re, cre_ref[m]) - _dot(h_im, cim_ref[m]))
    y = jnp.concatenate(ys, axis=-1) + dskip_ref[...] * u_b
    y = jax.nn.gelu(y)
    out_b = y * jax.nn.sigmoid(_dot(y.astype(BF16), wglu_ref[...]) + bglu_ref[...])
    proj_b = _dot(out_b.astype(BF16), wpb_ref[...])

    z_ga = _dot(xb, win_ref[:, 2 * pw:2 * pw + d_model]) + bin_ref[:, 2 * pw:2 * pw + d_model]
    merged = jax.nn.sigmoid(z_ga) * proj_a
    z_gb = _dot(xb, win_ref[:, 2 * pw + d_model:]) + bin_ref[:, 2 * pw + d_model:]
    merged = merged + jax.nn.sigmoid(z_gb) * proj_b
    mix = _dot(merged.astype(BF16), wout_ref[...])
    out = _layer_norm(alpha * x + mix, lng_ref[...], lnb_ref[...])
    if y_ref.shape[0] == rows:
        y_ref[...] = out
    else:
        y_ref[0:rows, :] = out
        y_ref[rows:, :] = jnp.zeros((y_ref.shape[0] - rows, d_model), F32)


def _token_mix(x_src, x_row0, row0, n_steps, nb, tt, start_pos, alpha, layer, prev, h0, wts, total_rows,
               y_buf=None, state_layer=None):
    state_layer = layer if state_layer is None else state_layer
    rows = nb * tt
    d_model = x_src.shape[-1]
    nst2 = h0.shape[-1]
    pw = prev.shape[-1]
    halo = POOL_STATE * nb
    aliased = y_buf is not None
    const = lambda a: _layer_spec(a, layer)
    tile = lambda i: jnp.minimum(i, n_steps - 1)
    if x_src.ndim == 3:
        x_spec = pl.BlockSpec((nb, tt, d_model), lambda i: (0, tile(i), 0))
    else:
        x_spec = pl.BlockSpec((rows, d_model), lambda i: (x_row0 // rows + tile(i), 0))
    in_specs = [x_spec, _layer_spec(prev, state_layer), _layer_spec(h0, state_layer)]
    in_specs += [const(w) for w in wts]
    args = [x_src, prev, h0, *wts]
    if aliased:
        in_specs.append(pl.BlockSpec(memory_space=pl.ANY))
        args.append(y_buf)
        y_spec = pl.BlockSpec((ROW_TILE, d_model), lambda i: (row0 // ROW_TILE, 0))
    else:
        y_spec = pl.BlockSpec((rows, d_model), lambda i: (row0 // rows + i, 0))
    out_shape = (jax.ShapeDtypeStruct((total_rows, d_model), F32),
                 jax.ShapeDtypeStruct((halo, pw), F32),
                 jax.ShapeDtypeStruct((nb, nst2), F32))
    out_specs = (y_spec,
                 pl.BlockSpec((halo, pw), lambda i: (0, 0)),
                 pl.BlockSpec((nb, nst2), lambda i: (0, 0)))
    zero_tail = not aliased
    assert aliased or row0 + (n_steps + 1) * rows == total_rows
    kern = functools.partial(_mix_kernel, n_steps=n_steps, zero_tail=zero_tail,
                             nb=nb, tt=tt, start_pos=start_pos, alpha=alpha, aliased=aliased)
    return pl.pallas_call(
        kern, grid=(n_steps + int(zero_tail),), in_specs=in_specs, out_specs=out_specs, out_shape=out_shape,
        scratch_shapes=[pltpu.VMEM((halo + rows, pw), F32), pltpu.VMEM((rows, nst2), F32),
                        pltpu.VMEM((nb, nst2), F32)],
        input_output_aliases={len(args) - 1: 0} if aliased else {},
        name="token_mix_decode" if aliased else "token_mix_prompt",
        compiler_params=_params("arbitrary"))(*args)


def _ffn_kernel(x_ref, wg_ref, wu_ref, wd_ref, g_ref, b_ref, o_ref, *, alpha):
    x = x_ref[...]
    ffn = _swiglu_split(x.astype(BF16), wg_ref, wu_ref, wd_ref)
    o_ref[...] = _layer_norm(alpha * x + ffn, g_ref[...], b_ref[...])


def _layer_spec(a, layer, **kw):
    return pl.BlockSpec((None,) + a.shape[1:], lambda *_, n=a.ndim - 1: (layer,) + (0,) * n, **kw)


def _dense_ffn(x_all, layer, wg, wu, wd, ln_g, ln_b, alpha):
    n, d_model = x_all.shape
    resident = lambda a: _layer_spec(a, layer, pipeline_mode=pl.Buffered(1))
    return pl.pallas_call(
        functools.partial(_ffn_kernel, alpha=alpha),
        grid=(n // ROW_TILE,),
        in_specs=[pl.BlockSpec((ROW_TILE, d_model), lambda i: (i, 0)),
                  resident(wg), resident(wu), resident(wd),
                  pl.BlockSpec((1, d_model), lambda i: (0, 0)), pl.BlockSpec((1, d_model), lambda i: (0, 0))],
        out_specs=pl.BlockSpec((ROW_TILE, d_model), lambda i: (i, 0)),
        out_shape=jax.ShapeDtypeStruct((n, d_model), F32),
        name="dense_swiglu_ln",
        compiler_params=_params("parallel"))(x_all, wg, wu, wd, ln_g, ln_b)


def _split_bf16(a):
    hi = a.astype(BF16)
    lo = (a - hi.astype(F32)).astype(BF16)
    return hi, lo


DEAD_RANK = -(1 << 20)


def _router_kernel(x_ref, w_ref, b_ref, idx_ref, gate_ref, cnt_ref, *, n_tokens):
    x_hi, x_lo = _split_bf16(x_ref[...])
    w_hi, w_lo = _split_bf16(w_ref[...])
    logits = _dot(x_hi, w_hi) + (_dot(x_lo, w_hi) + _dot(x_hi, w_lo)) + b_ref[...]
    lane = lax.broadcasted_iota(jnp.int32, logits.shape, 1)
    logits = jnp.where(lane < N_EXPERTS, logits, -jnp.inf)
    m1 = jnp.max(logits, axis=-1, keepdims=True)
    i1 = jnp.min(jnp.where(logits == m1, lane, LANES), axis=-1, keepdims=True)
    rest = jnp.where(lane == i1, -jnp.inf, logits)
    m2 = jnp.max(rest, axis=-1, keepdims=True)
    i2 = jnp.min(jnp.where(rest == m2, lane, LANES), axis=-1, keepdims=True)
    e2 = jnp.exp(m2 - m1)
    den = 1.0 + e2
    gate_ref[...] = jnp.where(lane == 0, 1.0 / den, e2 / den)

    row = lax.broadcasted_iota(jnp.int32, (ROW_TILE, 1), 0) + pl.program_id(0) * ROW_TILE
    live = row < n_tokens
    pick1 = jnp.where(jnp.logical_and(lane == i1, live), 1.0, 0.0)
    pick2 = jnp.where(jnp.logical_and(lane == i2, live), 1.0, 0.0)
    picks = pick1 + pick2
    r = lax.broadcasted_iota(jnp.int32, (ROW_TILE, ROW_TILE), 0)
    c = lax.broadcasted_iota(jnp.int32, (ROW_TILE, ROW_TILE), 1)
    earlier = _dot(jnp.where(c < r, 1.0, 0.0).astype(BF16), picks.astype(BF16))
    rank1 = jnp.sum(earlier * pick1, axis=-1, keepdims=True).astype(jnp.int32)
    rank2 = jnp.sum(earlier * pick2, axis=-1, keepdims=True).astype(jnp.int32)
    rank1 = jnp.where(live, rank1, DEAD_RANK)
    rank2 = jnp.where(live, rank2, DEAD_RANK)
    idx_ref[...] = jnp.where(lane == 0, i1, jnp.where(lane == 1, i2, jnp.where(lane == 2, rank1, rank2)))
    counts = jnp.sum(picks, axis=0, keepdims=True).astype(jnp.int32)
    cnt_ref[...] = jnp.broadcast_to(counts, cnt_ref.shape)


def _router(x_all, n_tokens, w_router, b_router):
    n, d_model = x_all.shape
    w = jnp.pad(w_router, ((0, 0), (0, LANES - N_EXPERTS)))
    b = jnp.pad(b_router, (0, LANES - N_EXPERTS))[None, :]
    n_tiles = n // ROW_TILE
    return pl.pallas_call(
        functools.partial(_router_kernel, n_tokens=n_tokens), grid=(n_tiles,),
        in_specs=[pl.BlockSpec((ROW_TILE, d_model), lambda i: (i, 0)),
                  pl.BlockSpec((d_model, LANES), lambda i: (0, 0)),
                  pl.BlockSpec((1, LANES), lambda i: (0, 0))],
        out_specs=(pl.BlockSpec((ROW_TILE, LANES), lambda i: (i, 0)),
                   pl.BlockSpec((ROW_TILE, LANES), lambda i: (i, 0)),
                   pl.BlockSpec((8, LANES), lambda i: (i, 0))),
        out_shape=(jax.ShapeDtypeStruct((n, LANES), jnp.int32), jax.ShapeDtypeStruct((n, LANES), F32),
                   jax.ShapeDtypeStruct((n_tiles * 8, LANES), jnp.int32)),
        name="moe_router",
        compiler_params=_params("parallel"))(x_all, w, b)


def _group_copy(src_ref, src_row, dst_ref, dst_row, sem):
    return pltpu.make_async_copy(src_ref.at[pl.ds(pl.multiple_of(src_row, GROUP), GROUP)],
                                 dst_ref.at[pl.ds(pl.multiple_of(dst_row, GROUP), GROUP)], sem)


def _wait_groups(count, src_ref, dst_ref, sem):
    def wait(_, carry):
        _group_copy(src_ref, 0, dst_ref, 0, sem).wait()
        return carry
    lax.fori_loop(0, count, wait, 0)


def _stage_rows(expert, rank, first_ref, step):
    pos = rank
    for e in range(N_EXPERTS):
        pos = pos + jnp.where(expert == e, first_ref[step * N_EXPERTS + e], 0)
    return pos


def _regroup_kernel(w0_ref, off_ref, ngrp_ref, part_ref, pend_ref, first_ref, x_ref, idx_ref, xs_hbm,
                    stage_ref, carry_ref, zero_ref, sem):
    step = pl.program_id(0)

    @pl.when(step == 0)
    def _():
        carry_ref[...] = jnp.zeros(carry_ref.shape, BF16)
        zero_ref[...] = jnp.zeros(zero_ref.shape, BF16)

    xb = x_ref[...].astype(BF16)
    stage_row = lax.broadcasted_iota(jnp.int32, (STAGE_ROWS, ROW_TILE), 0)
    idx = idx_ref[...]
    hit = None
    for choice in range(TOP_K):
        pos = _stage_rows(idx[choice:choice + 1, :], idx[TOP_K + choice:TOP_K + choice + 1, :], first_ref, step)
        hit = stage_row == pos if hit is None else jnp.logical_or(hit, stage_row == pos)
    stage_ref[...] = _dot(jnp.where(hit, 1.0, 0.0).astype(BF16), xb).astype(BF16)

    started = 0
    for e in range(N_EXPERTS):
        k = step * N_EXPERTS + e
        off, w0, ngrp = off_ref[k], w0_ref[k], ngrp_ref[k]
        head = pl.ds(pl.multiple_of(off, GROUP), GROUP)
        stage_ref[head, :] = stage_ref[head, :] + carry_ref[e]

        def send(q, carry, off=off, w0=w0):
            _group_copy(stage_ref, off + q * GROUP, xs_hbm, w0 + q * GROUP, sem).start()
            return carry

        lax.fori_loop(0, ngrp, send, 0)
        tail = stage_ref[pl.ds(pl.multiple_of(off + ngrp * GROUP, GROUP), GROUP), :]
        carry_ref[e] = jnp.where(part_ref[k] != 0, tail, jnp.zeros_like(tail))
        started = started + ngrp
    _wait_groups(started, stage_ref, xs_hbm, sem)

    @pl.when(step == pl.num_programs(0) - 1)
    def _():
        flushed = 0
        for e in range(N_EXPERTS):
            k = step * N_EXPERTS + e
            row = w0_ref[k] + ngrp_ref[k] * GROUP
            is_open = part_ref[k]

            @pl.when(is_open != 0)
            def _(row=row, e=e):
                pltpu.make_async_copy(carry_ref.at[e], xs_hbm.at[pl.ds(pl.multiple_of(row, GROUP), GROUP)],
                                      sem).start()

            row = row + is_open * GROUP
            n_zero = (pend_ref[e] - row) // GROUP

            def fill(q, carry, row=row):
                _group_copy(zero_ref, 0, xs_hbm, row + q * GROUP, sem).start()
                return carry

            lax.fori_loop(0, n_zero, fill, 0)
            flushed = flushed + is_open + n_zero
        _wait_groups(flushed, zero_ref, xs_hbm, sem)


def _regroup(x_all, idx_rows, tables, n_slots):
    n, d_model = x_all.shape
    return pl.pallas_call(
        _regroup_kernel,
        grid_spec=pltpu.PrefetchScalarGridSpec(
            num_scalar_prefetch=6, grid=(n // ROW_TILE,),
            in_specs=[pl.BlockSpec((ROW_TILE, d_model), lambda i, *_: (i, 0)),
                      pl.BlockSpec((2 * TOP_K, ROW_TILE), lambda i, *_: (0, i))],
            out_specs=pl.BlockSpec(memory_space=pl.ANY),
            scratch_shapes=[pltpu.VMEM((STAGE_ROWS, d_model), BF16),
                            pltpu.VMEM((N_EXPERTS, GROUP, d_model), BF16),
                            pltpu.VMEM((GROUP, d_model), BF16),
                            pltpu.SemaphoreType.DMA(())]),
        out_shape=jax.ShapeDtypeStruct((n_slots, d_model), BF16),
        name="moe_regroup",
        compiler_params=_params("arbitrary"))(
            tables["w0"], tables["off"], tables["ngrp"], tables["part"], tables["pend"], tables["first"],
            x_all, idx_rows)


def _moe_kernel(item_e_ref, item_b0_ref, item_nb_ref, item_half_ref, nvalid_ref, *refs):
    xs_refs = refs[:ITEM_BLOCKS]
    wg_ref, wu_ref, wd_ref, ys_hbm = refs[ITEM_BLOCKS:ITEM_BLOCKS + 4]
    *wgub_refs, wdcat_ref, htail_ref, acc_ref, stage_ref, sem = refs[ITEM_BLOCKS + 4:]
    fc = wd_ref.shape[0]
    item, c = pl.program_id(0), pl.program_id(1)
    last_item, last_c = pl.num_programs(0) - 1, pl.num_programs(1) - 1
    n_blocks = ys_hbm.shape[0] // MOE_ROWS
    nb, b0 = item_nb_ref[item], item_b0_ref[item]

    def block_out(src_ref, block):
        rows = pl.ds(pl.multiple_of(block * MOE_ROWS, MOE_ROWS), MOE_ROWS)
        return pltpu.make_async_copy(src_ref, ys_hbm.at[rows], sem)

    def start_and_wait(count, copy_of):
        def start(q, carry):
            copy_of(q).start()
            return carry

        def wait(q, carry):
            copy_of(q).wait()
            return carry

        lax.fori_loop(0, count, start, 0)
        lax.fori_loop(0, count, wait, 0)

    @pl.when(jnp.logical_and(item == 0, c == 0))
    def _():
        acc_ref[...] = jnp.zeros(acc_ref.shape, F32)

    def cast_then_load(p):
        cols = slice(p * LANES, (p + 1) * LANES)
        wgub_refs[p][:, :LANES] = wg_ref[:, cols].astype(BF16)
        wgub_refs[p][:, LANES:] = wu_ref[:, cols].astype(BF16)
        return wgub_refs[p][...]

    def run_block(j, n_rows, odd):
        if j == 0:
            wdcat_ref[LANES:, :] = wd_ref[...].astype(BF16)
            if not odd:
                wdcat_ref[0:LANES, :] = wd_ref[fc - LANES:, :].astype(BF16)
        piece = cast_then_load if j == 0 else (lambda p: wgub_refs[p][...])
        pieces = _hidden_pieces(xs_refs[j][0:n_rows, :], piece, fc // LANES)
        if odd:
            hidden = jnp.concatenate([htail_ref[j, 0:n_rows, :]] + pieces, axis=-1)
            part = _dot(hidden, wdcat_ref[...])
        else:
            htail_ref[j, 0:n_rows, :] = pieces[-1]
            part = _dot(jnp.concatenate(pieces[:-1], axis=-1), wdcat_ref[LANES:fc, :])
        total = jnp.where(c == 0, part, acc_ref[j, 0:n_rows, :] + part)
        acc_ref[j, 0:n_rows, :] = total
        stage_ref[j, 0:n_rows, :] = total.astype(BF16)
        if n_rows < MOE_ROWS:
            stage_ref[j, n_rows:, :] = jnp.zeros((MOE_ROWS - n_rows, stage_ref.shape[2]), BF16)

    for j in range(ITEM_BLOCKS):
        half = jnp.logical_and(j == nb - 1, item_half_ref[item] != 0)
        full = jnp.logical_and(j < nb, jnp.logical_not(half))
        for odd in (False, True):
            parity = lax.rem(c, 2) == int(odd)
            pl.when(jnp.logical_and(full, parity))(functools.partial(run_block, j, MOE_ROWS, odd))
            pl.when(jnp.logical_and(half, parity))(functools.partial(run_block, j, MOE_ROWS // 2, odd))

    @pl.when(c == last_c)
    def _():
        start_and_wait(nb, lambda q: block_out(stage_ref.at[q], b0 + q))

    @pl.when(jnp.logical_and(item == last_item, c == last_c))
    def _():
        stage_ref[0] = jnp.zeros(stage_ref.shape[1:], BF16)
        first_unused = nvalid_ref[0]
        start_and_wait(n_blocks - first_unused, lambda q: block_out(stage_ref.at[0], first_unused + q))


def _grouped_swiglu(xs, items, nvalid, layer, wg, wu, wd):
    n_slots, d_model = xs.shape
    d_ff = wg.shape[3]
    fc = MOE_FF_CHUNK
    n_chunks = d_ff // fc
    n_blocks = n_slots // MOE_ROWS
    item_e, item_b0, item_nb, item_half = items
    assert d_ff % fc == 0 and n_chunks % 2 == 0 and (fc - LANES) % MXU_COLS == 0 and n_slots % MOE_ROWS == 0
    chunk = lambda c, nb, it: jnp.where(nb[it] > 0, c, n_chunks - 1)
    w_in_map = lambda it, c, e, b0, nb, *_: (layer, e[it], 0, chunk(c, nb, it))
    w_out_map = lambda it, c, e, b0, nb, *_: (layer, e[it], chunk(c, nb, it), 0)
    xs_spec = lambda k: pl.BlockSpec(
        (MOE_ROWS, d_model), lambda it, c, e, b0, *_: (jnp.minimum(b0[it] + k, n_blocks - 1), 0))
    return pl.pallas_call(
        _moe_kernel,
        grid_spec=pltpu.PrefetchScalarGridSpec(
            num_scalar_prefetch=5, grid=(item_e.shape[0], n_chunks),
            in_specs=[xs_spec(k) for k in range(ITEM_BLOCKS)] + [
                pl.BlockSpec((None, None, d_model, fc), w_in_map),
                pl.BlockSpec((None, None, d_model, fc), w_in_map),
                pl.BlockSpec((None, None, fc, d_model), w_out_map)],
            out_specs=pl.BlockSpec(memory_space=pl.ANY),
            scratch_shapes=[pltpu.VMEM((d_model, 2 * LANES), BF16) for _ in range(fc // LANES)] + [
                            pltpu.VMEM((LANES + fc, d_model), BF16),
                            pltpu.VMEM((ITEM_BLOCKS, MOE_ROWS, LANES), BF16),
                            pltpu.VMEM((ITEM_BLOCKS, MOE_ROWS, d_model), F32),
                            pltpu.VMEM((ITEM_BLOCKS, MOE_ROWS, d_model), BF16),
                            pltpu.SemaphoreType.DMA(())]),
        out_shape=jax.ShapeDtypeStruct((n_slots, d_model), BF16),
        name="moe_grouped_swiglu",
        compiler_params=_params("arbitrary", "arbitrary"))(
            item_e, item_b0, item_nb, item_half, nvalid, *([xs] * ITEM_BLOCKS), wg, wu, wd)


def _combine_kernel(w0_ref, off_ref, nwin_ref, first_ref, x_ref, idx_ref, gate_ref, g_ref, b_ref, ys_hbm,
                    *rest, alpha, batch_major):
    if batch_major:
        prompt_ref, tail_ref, stage_ref, sem = rest
    else:
        o_ref, stage_ref, sem = rest
    step = pl.program_id(0)

    def fetch_tile(tile, buf):
        stage_ref[buf] = jnp.zeros(stage_ref.shape[1:], BF16)
        for e in range(N_EXPERTS):
            k = tile * N_EXPERTS + e
            off, w0 = off_ref[k], w0_ref[k]

            def fetch(q, carry, off=off, w0=w0):
                _group_copy(ys_hbm, w0 + q * GROUP, stage_ref.at[buf], off + q * GROUP, sem.at[buf]).start()
                return carry

            lax.fori_loop(0, nwin_ref[k], fetch, 0)

    @pl.when(step == 0)
    def _():
        fetch_tile(step, 0)

    buf = lax.rem(step, 2)

    @pl.when(step + 1 < pl.num_programs(0))
    def _():
        fetch_tile(step + 1, 1 - buf)

    n_groups = 0
    for e in range(N_EXPERTS):
        n_groups = n_groups + nwin_ref[step * N_EXPERTS + e]
    _wait_groups(n_groups, ys_hbm, stage_ref.at[buf], sem.at[buf])

    stage_col = lax.broadcasted_iota(jnp.int32, (ROW_TILE, STAGE_ROWS), 1)
    idx = idx_ref[...]
    gates = gate_ref[...]
    ys = stage_ref[buf]
    f = None
    for choice in range(TOP_K):
        pos = _stage_rows(idx[:, choice:choice + 1], idx[:, TOP_K + choice:TOP_K + choice + 1], first_ref, step)
        pick = jnp.where(stage_col == pos, 1.0, 0.0).astype(BF16)
        term = gates[:, choice:choice + 1] * _dot(pick, ys)
        f = term if f is None else f + term
    out = _layer_norm(alpha * x_ref[...] + f, g_ref[...], b_ref[...])
    if not batch_major:
        o_ref[...] = out
        return
    nb, tt = prompt_ref.shape[0], prompt_ref.shape[1]
    last = pl.num_programs(0) - 1

    @pl.when(step < last)
    def _():
        for t in range(tt):
            prompt_ref[:, t, :] = out[t * nb:(t + 1) * nb, :]

    @pl.when(step == last)
    def _():
        tail_ref[...] = out


def _combine(x_all, idx, gates, tables, ys, ln_g, ln_b, alpha, prompt_shape=None):
    n, d_model = x_all.shape
    n_tiles = n // ROW_TILE
    if prompt_shape is None:
        out_specs = pl.BlockSpec((ROW_TILE, d_model), lambda i, *_: (i, 0))
        out_shape = jax.ShapeDtypeStruct((n, d_model), F32)
    else:
        batch, seq = prompt_shape
        tt = ROW_TILE // batch
        assert batch * seq == (n_tiles - 1) * ROW_TILE
        out_specs = (pl.BlockSpec((batch, tt, d_model), lambda i, *_: (0, jnp.minimum(i, n_tiles - 2), 0)),
                     pl.BlockSpec((ROW_TILE, d_model), lambda i, *_: (0, 0)))
        out_shape = (jax.ShapeDtypeStruct((batch, seq, d_model), F32),
                     jax.ShapeDtypeStruct((ROW_TILE, d_model), F32))
    return pl.pallas_call(
        functools.partial(_combine_kernel, alpha=alpha, batch_major=prompt_shape is not None),
        grid_spec=pltpu.PrefetchScalarGridSpec(
            num_scalar_prefetch=4, grid=(n_tiles,),
            in_specs=[pl.BlockSpec((ROW_TILE, d_model), lambda i, *_: (i, 0)),
                      pl.BlockSpec((ROW_TILE, LANES), lambda i, *_: (i, 0)),
                      pl.BlockSpec((ROW_TILE, LANES), lambda i, *_: (i, 0)),
                      pl.BlockSpec((1, d_model), lambda i, *_: (0, 0)),
                      pl.BlockSpec((1, d_model), lambda i, *_: (0, 0)),
                      pl.BlockSpec(memory_space=pl.ANY)],
            out_specs=out_specs,
            scratch_shapes=[pltpu.VMEM((2, STAGE_ROWS, d_model), BF16), pltpu.SemaphoreType.DMA((2,))]),
        out_shape=out_shape,
        name="moe_combine_ln",
        compiler_params=_params("arbitrary"))(
            tables["w0"], tables["off"], tables["nwin"], tables["first"], x_all, idx, gates, ln_g, ln_b, ys)


def _routing_tables(counts, n_slots):
    count = jnp.sum(counts, axis=0)
    blocks = (count + MOE_ROWS - 1) // MOE_ROWS
    blk_end = jnp.cumsum(blocks)
    base = (blk_end - blocks) * MOE_ROWS
    n_items = n_slots // MOE_ROWS // ITEM_BLOCKS + N_EXPERTS
    items_e = (blocks + ITEM_BLOCKS - 1) // ITEM_BLOCKS
    item_end = jnp.cumsum(items_e)
    n_used = item_end[-1]
    item = jnp.arange(n_items, dtype=jnp.int32)
    used_item = jnp.minimum(item, n_used - 1)
    item_e = jnp.minimum(jnp.sum((used_item[:, None] >= item_end[None, :]).astype(jnp.int32), axis=1),
                         N_EXPERTS - 1)
    of_item = lambda per_expert: jnp.sum(
        jnp.where(item_e[:, None] == jnp.arange(N_EXPERTS)[None, :], per_expert[None, :], 0), axis=1)
    local = used_item - of_item(item_end - items_e)
    item_b0 = of_item(blk_end - blocks) + ITEM_BLOCKS * local
    item_nb = jnp.where(item < n_used, jnp.clip(of_item(blocks) - ITEM_BLOCKS * local, 0, ITEM_BLOCKS), 0)
    tail_rows = count - (blocks - 1) * MOE_ROWS
    item_half = jnp.logical_and(jnp.logical_and(item < n_used, local == of_item(items_e) - 1),
                                of_item(tail_rows) <= MOE_ROWS // 2)
    items = tuple(a.astype(jnp.int32) for a in (item_e, item_b0, item_nb, item_half))
    start = base[None, :] + jnp.cumsum(counts, axis=0) - counts
    end = start + counts
    w0 = start // GROUP * GROUP
    ngrp = end // GROUP - start // GROUP
    part = (end % GROUP != 0).astype(jnp.int32)
    nwin = ngrp + part
    off = (jnp.cumsum(nwin, axis=1) - nwin) * GROUP
    nvalid = blk_end[-1]
    pend = (blk_end * MOE_ROWS).at[-1].set(n_slots)
    flat = lambda a: a.reshape(-1).astype(jnp.int32)
    tables = dict(w0=flat(w0), off=flat(off), first=flat(start - w0 + off), ngrp=flat(ngrp), part=flat(part),
                  nwin=flat(nwin), pend=flat(pend))
    return tables, items, nvalid.reshape(1).astype(jnp.int32)


def _moe_ffn(x_all, n_tokens, layer, w_router, b_router, wg, wu, wd, ln_g, ln_b, alpha, prompt_shape=None):
    idx, gates, cnt = _router(x_all, n_tokens, w_router, b_router)
    n_slots = TOP_K * n_tokens + N_EXPERTS * MOE_ROWS
    n_slots = -(-n_slots // (MOE_ROWS * ITEM_BLOCKS)) * (MOE_ROWS * ITEM_BLOCKS)
    tables, items, nvalid = _routing_tables(cnt[::8, :N_EXPERTS], n_slots)
    xs = _regroup(x_all, idx[:, :2 * TOP_K].T, tables, n_slots)
    ys = _grouped_swiglu(xs, items, nvalid, layer, wg, wu, wd)
    return _combine(x_all, idx, gates, tables, ys, ln_g, ln_b, alpha, prompt_shape)


def kernel(x_prompt, x_sample, state_pool, state_ssm_re, state_ssm_im, ln1_g, ln1_b, w_in, b_in, w_pool, pool_scale, lambda_re, lambda_im, log_dt, b_re, b_im, c_re, c_im, d_skip, w_glu, b_glu, w_proj_a, w_proj_b, w_out, ln2_g, ln2_b, w_ffn_gate, w_ffn_up, w_ffn_down, w_router, b_router, w_moe_gate, w_moe_up, w_moe_down):
    batch, seq, d_model = x_prompt.shape
    dec_batch = x_sample.shape[0]
    depth = w_in.shape[0]
    n_groups, n_state = lambda_re.shape[1], lambda_re.shape[2]
    chan = b_re.shape[-1]
    nst = n_groups * n_state
    pw = state_pool.shape[-1]
    alpha = (2.0 * depth) ** 0.25
    n_prompt = batch * seq
    n_tokens = n_prompt + dec_batch
    total_rows = -(-n_tokens // ROW_TILE) * ROW_TILE
    assert x_sample.shape[1] == 1 and n_prompt % ROW_TILE == 0 and seq % MIX_STEPS == 0
    assert batch * MIX_STEPS == ROW_TILE and dec_batch <= ROW_TILE

    rows_of = lambda a: a[:, None, :]
    a_re, a_im, bb_re, bb_im = _ssm_params(lambda_re, lambda_im, log_dt, b_re, b_im)
    lam = jnp.concatenate([a_re[::chan].reshape(depth, 1, nst), a_im[::chan].reshape(depth, 1, nst)], axis=2)
    wts = (w_in.astype(BF16), rows_of(b_in), w_pool.astype(BF16), rows_of(pool_scale), lam,
           _block_diag_in(bb_re, depth, n_groups, chan, n_state).astype(BF16),
           _block_diag_in(bb_im, depth, n_groups, chan, n_state).astype(BF16),
           _block_diag_out(c_re, depth, n_groups, chan, n_state).astype(BF16),
           _block_diag_out(c_im, depth, n_groups, chan, n_state).astype(BF16),
           d_skip.reshape(depth, 1, -1), w_glu.astype(BF16), rows_of(b_glu),
           w_proj_a.astype(BF16), w_proj_b.astype(BF16), w_out.astype(BF16),
           rows_of(ln1_g), rows_of(ln1_b))
    ffn_wts = (w_ffn_gate.astype(BF16), w_ffn_up.astype(BF16), w_ffn_down.astype(BF16))
    prev_s = jnp.transpose(state_pool, (0, 2, 1, 3)).reshape(depth, POOL_STATE * dec_batch, pw)
    h0_s = jnp.concatenate([state_ssm_re.reshape(depth, dec_batch, nst),
                            state_ssm_im.reshape(depth, dec_batch, nst)], axis=2)
    zero_prev = jnp.zeros((1, POOL_STATE * batch, pw), F32)
    zero_h = jnp.zeros((1, batch, 2 * nst), F32)

    x_p = x_prompt
    x_s = x_sample.reshape(dec_batch, d_model)
    s_row0 = 0
    outs = [[] for _ in range(6)]
    for l in range(depth):
        y_mix, pool_p, h_p = _token_mix(x_p, 0, 0, seq // MIX_STEPS, batch, MIX_STEPS, 0, alpha, l,
                                        zero_prev, zero_h, wts, total_rows, state_layer=0)
        y_mix, pool_s, h_s = _token_mix(x_s, s_row0, n_prompt, 1, dec_batch, 1, PAST_LEN, alpha, l,
                                        prev_s, h0_s, wts, total_rows, y_buf=y_mix)
        outs[0].append(jnp.transpose(pool_p.reshape(POOL_STATE, batch, pw), (1, 0, 2)))
        outs[1].append(h_p[:, :nst].reshape(batch, n_groups, n_state))
        outs[2].append(h_p[:, nst:].reshape(batch, n_groups, n_state))
        outs[3].append(jnp.transpose(pool_s.reshape(POOL_STATE, dec_batch, pw), (1, 0, 2)))
        outs[4].append(h_s[:, :nst].reshape(dec_batch, n_groups, n_state))
        outs[5].append(h_s[:, nst:].reshape(dec_batch, n_groups, n_state))
        j = l // 2
        if l % 2 == 0:
            y_all = _dense_ffn(y_mix, j, *ffn_wts, ln2_g[l][None, :], ln2_b[l][None, :], alpha)
        else:
            prompt_shape = (batch, seq) if l == depth - 1 else None
            y_all = _moe_ffn(y_mix, n_tokens, j, w_router[j], b_router[j], w_moe_gate, w_moe_up,
                             w_moe_down, ln2_g[l][None, :], ln2_b[l][None, :], alpha, prompt_shape)
        x_p = x_s = y_all
        s_row0 = n_prompt
    if isinstance(y_all, (tuple, list)):
        y_p, tail = y_all
        y_s = tail[:dec_batch].reshape(dec_batch, 1, d_model)
    else:
        y_p = jnp.transpose(y_all[:n_prompt].reshape(seq, batch, d_model), (1, 0, 2))
        y_s = y_all[n_prompt:n_tokens].reshape(dec_batch, 1, d_model)
    return (y_p, y_s, *[jnp.stack(o) for o in outs])
```

```python
import functools
import math

import jax
import jax.numpy as jnp
from jax import lax
from jax.experimental import pallas as pl
from jax.experimental.pallas import tpu as pltpu

F32 = jnp.float32
BF16 = jnp.bfloat16

POOL_WINDOWS = (2, 4, 8, 16)
POOL_STATE = max(POOL_WINDOWS) - 1
POOL_GROUP = 128
SSM_GROUP = 16
SSM_STATE = 64
N_EXPERTS = 8
TOP_K = 2
LN_EPS = 1e-5
PAST_LEN = 16384

LANES = 128
MXU_COLS = 256
ROW_TILE = 512
MIX_STEPS = 64
SCAN_COLS = 512
MOE_ROWS = 512
ITEM_BLOCKS = 5
MOE_FF_CHUNK = 896
GROUP = 32
STAGE_ROWS = TOP_K * ROW_TILE + N_EXPERTS * 2 * GROUP
VMEM_LIMIT = 56 * 1024 * 1024


def _dot(a, b):
    return jnp.dot(a, b, preferred_element_type=F32)


def _layer_norm(r, g, b):
    mu = jnp.mean(r, axis=-1, keepdims=True)
    d = r - mu
    var = jnp.mean(d * d, axis=-1, keepdims=True)
    return d * lax.rsqrt(var + LN_EPS) * g + b


def _swiglu(xb, gate_up_piece, n_pieces, wd_ref):
    pieces = []
    for p in range(n_pieces):
        gate, up = gate_up_piece(p)
        pieces.append((jax.nn.silu(gate) * up).astype(BF16))
    return _dot(jnp.concatenate(pieces, axis=-1), wd_ref[...])


def _swiglu_split(xb, wg_ref, wu_ref, wd_ref):
    def piece(p):
        cols = slice(p * MXU_COLS, (p + 1) * MXU_COLS)
        return _dot(xb, wg_ref[:, cols]), _dot(xb, wu_ref[:, cols])
    return _swiglu(xb, piece, wd_ref.shape[0] // MXU_COLS, wd_ref)


def _swiglu_interleaved(xb, wgu_piece, wd_ref):
    def piece(p):
        gate_up = _dot(xb, wgu_piece(p))
        return gate_up[:, :LANES], gate_up[:, LANES:]
    return _swiglu(xb, piece, wd_ref.shape[0] // LANES, wd_ref)


def _params(*sem):
    return pltpu.CompilerParams(dimension_semantics=sem, vmem_limit_bytes=VMEM_LIMIT)


def _ssm_param_kernel(lre_ref, lim_ref, ldt_ref, bre_ref, bim_ref, are_ref, aim_ref, bbre_ref, bbim_ref):
    lre, lim = lre_ref[...], lim_ref[...]
    dt = jnp.exp(ldt_ref[...])
    mag = jnp.exp(lre * dt)
    a_re = mag * jnp.cos(lim * dt)
    a_im = mag * jnp.sin(lim * dt)
    den = lre * lre + lim * lim
    n_re, n_im = a_re - 1.0, a_im
    k_re = (n_re * lre + n_im * lim) / den
    k_im = (n_im * lre - n_re * lim) / den
    b_re, b_im = bre_ref[...], bim_ref[...]
    are_ref[...] = a_re
    aim_ref[...] = a_im
    bbre_ref[...] = k_re * b_re - k_im * b_im
    bbim_ref[...] = k_re * b_im + k_im * b_re


def _ssm_params(lambda_re, lambda_im, log_dt, b_re, b_im):
    depth, g, p = lambda_re.shape
    c = b_re.shape[-1]
    rep = lambda a: jnp.repeat(a.reshape(depth * g, p), c, axis=0)
    bt = lambda a: jnp.transpose(a, (0, 1, 3, 2)).reshape(depth * g * c, p)
    shape = jax.ShapeDtypeStruct((depth * g * c, p), F32)
    return pl.pallas_call(_ssm_param_kernel, out_shape=(shape,) * 4, name="s5_discretise")(
        rep(lambda_re), rep(lambda_im), rep(jnp.broadcast_to(log_dt[:, :, None], (depth, g, p))),
        bt(b_re), bt(b_im))


def _block_diag_in(bt, depth, g, c, p):
    m = g // 8
    a = bt.reshape(depth, m, 8, c, p)
    eye = jnp.eye(8, dtype=bt.dtype)
    return jnp.einsum('lmgcp,gh->lmgchp', a, eye).reshape(depth, m, 8 * c, 8 * p)


def _block_diag_out(cm, depth, g, c, p):
    m = g // 8
    a = jnp.transpose(cm, (0, 1, 3, 2)).reshape(depth, m, 8, p, c)
    eye = jnp.eye(8, dtype=cm.dtype)
    return jnp.einsum('lmgpc,gh->lmgphc', a, eye).reshape(depth, m, 8 * p, 8 * c)


def _mix_kernel(*refs, n_steps, zero_tail, **kw):
    if not zero_tail:
        _mix_body(*refs, **kw)
        return
    step = pl.program_id(0)
    y_ref = refs[-6]

    @pl.when(step < n_steps)
    def _():
        _mix_body(*refs, **kw)

    @pl.when(step >= n_steps)
    def _():
        y_ref[...] = jnp.zeros(y_ref.shape, F32)


def _mix_body(x_ref, prev_ref, h0_ref, win_ref, bin_ref, wpool_ref, pscale_ref, lam_ref,
              bre_ref, bim_ref, cre_ref, cim_ref, dskip_ref, wglu_ref, bglu_ref,
              wpa_ref, wpb_ref, wout_ref, lng_ref, lnb_ref, *rest,
              nb, tt, start_pos, alpha, aliased):
    if aliased:
        rest = rest[1:]
    y_ref, pool_ref, hlast_ref, ext_ref, bu_ref, hs_ref = rest
    rows = nb * tt
    halo = POOL_STATE * nb
    d_model = x_ref.shape[-1]
    pw = pscale_ref.shape[1]
    nst = lam_ref.shape[1] // 2
    step = pl.program_id(0)

    @pl.when(step == 0)
    def _():
        ext_ref[0:halo, :] = prev_ref[...]
        hs_ref[...] = h0_ref[...]

    if x_ref.ndim == 3:
        x = jnp.concatenate([x_ref[:, t, :] for t in range(tt)], axis=0)
    else:
        x = x_ref[...]
    xb = x.astype(BF16)

    u_a = _dot(xb, win_ref[:, 0:pw]) + bin_ref[:, 0:pw]
    ext_ref[halo:halo + rows, :] = u_a
    t_idx = lax.shift_right_logical(lax.broadcasted_iota(jnp.int32, (rows, 1), 0), int(math.log2(nb)))
    pos1 = t_idx + (step * tt + start_pos + 1)
    mixed = []
    for gi, w in enumerate(POOL_WINDOWS):
        sl = slice(gi * POOL_GROUP, (gi + 1) * POOL_GROUP)
        cur = u_a[:, sl]
        acc = cur
        for k in range(1, w):
            acc = acc + ext_ref[halo - k * nb:halo - k * nb + rows, sl]
        count = jnp.minimum(pos1, w).astype(F32)
        pooled = acc / count - cur
        mixed.append(_dot(pooled.astype(BF16), wpool_ref[gi]))
    out_a = jnp.concatenate(mixed, axis=-1) * pscale_ref[...]
    new_halo = ext_ref[rows:rows + halo, :]
    ext_ref[0:halo, :] = new_halo
    pool_ref[...] = new_halo
    proj_a = _dot(out_a.astype(BF16), wpa_ref[...])

    u_b = _dot(xb, win_ref[:, pw:2 * pw]) + bin_ref[:, pw:2 * pw]
    ub16 = u_b.astype(BF16)
    n_m = bre_ref.shape[0]
    kin, kst = bre_ref.shape[1], bre_ref.shape[2]
    for m in range(n_m):
        um = ub16[:, m * kin:(m + 1) * kin]
        bu_ref[:, m * kst:(m + 1) * kst] = _dot(um, bre_ref[m])
        bu_ref[:, nst + m * kst:nst + (m + 1) * kst] = _dot(um, bim_ref[m])
    if tt == 1:
        a_re, a_im = lam_ref[:, 0:nst], lam_ref[:, nst:]
        h_re, h_im = hs_ref[:, 0:nst], hs_ref[:, nst:]
        n_re = a_re * h_re - a_im * h_im + bu_ref[:, 0:nst]
        n_im = a_re * h_im + a_im * h_re + bu_ref[:, nst:]
        bu_ref[:, 0:nst] = n_re
        bu_ref[:, nst:] = n_im
        hs_ref[:, 0:nst] = n_re
        hs_ref[:, nst:] = n_im
    else:
        for q in range(nst // SCAN_COLS):
            c_re = slice(q * SCAN_COLS, (q + 1) * SCAN_COLS)
            c_im = slice(nst + q * SCAN_COLS, nst + (q + 1) * SCAN_COLS)
            a_re = jnp.broadcast_to(lam_ref[:, c_re], (nb, SCAN_COLS))
            a_im = jnp.broadcast_to(lam_ref[:, c_im], (nb, SCAN_COLS))

            def body(t, carry, c_re=c_re, c_im=c_im, a_re=a_re, a_im=a_im):
                h_re, h_im = carry
                r = pl.ds(pl.multiple_of(t * nb, nb), nb)
                n_re = a_re * h_re - a_im * h_im + bu_ref[r, c_re]
                n_im = a_re * h_im + a_im * h_re + bu_ref[r, c_im]
                bu_ref[r, c_re] = n_re
                bu_ref[r, c_im] = n_im
                return n_re, n_im

            h_re, h_im = lax.fori_loop(0, tt, body, (hs_ref[:, c_re], hs_ref[:, c_im]), unroll=True)
            hs_ref[:, c_re] = h_re
            hs_ref[:, c_im] = h_im
    hlast_ref[...] = hs_ref[...]

    ys = []
    for m in range(n_m):
        h_re = bu_ref[:, m * kst:(m + 1) * kst].astype(BF16)
        h_im = bu_ref[:, nst + m * kst:nst + (m + 1) * kst].astype(BF16)
        ys.append(_dot(h_re, cre_ref[m]) - _dot(h_im, cim_ref[m]))
    y = jnp.concatenate(ys, axis=-1) + dskip_ref[...] * u_b
    y = jax.nn.gelu(y)
    out_b = y * jax.nn.sigmoid(_dot(y.astype(BF16), wglu_ref[...]) + bglu_ref[...])
    proj_b = _dot(out_b.astype(BF16), wpb_ref[...])

    z_ga = _dot(xb, win_ref[:, 2 * pw:2 * pw + d_model]) + bin_ref[:, 2 * pw:2 * pw + d_model]
    merged = jax.nn.sigmoid(z_ga) * proj_a
    z_gb = _dot(xb, win_ref[:, 2 * pw + d_model:]) + bin_ref[:, 2 * pw + d_model:]
    merged = merged + jax.nn.sigmoid(z_gb) * proj_b
    mix = _dot(merged.astype(BF16), wout_ref[...])
    out = _layer_norm(alpha * x + mix, lng_ref[...], lnb_ref[...])
    if y_ref.shape[0] == rows:
        y_ref[...] = out
    else:
        y_ref[0:rows, :] = out
        y_ref[rows:, :] = jnp.zeros((y_ref.shape[0] - rows, d_model), F32)


def _token_mix(x_src, x_row0, row0, n_steps, nb, tt, start_pos, alpha, layer, prev, h0, wts, total_rows,
               y_buf=None, state_layer=None):
    state_layer = layer if state_layer is None else state_layer
    rows = nb * tt
    d_model = x_src.shape[-1]
    nst2 = h0.shape[-1]
    pw = prev.shape[-1]
    halo = POOL_STATE * nb
    aliased = y_buf is not None
    const = lambda a: _layer_spec(a, layer)
    tile = lambda i: jnp.minimum(i, n_steps - 1)
    if x_src.ndim == 3:
        x_spec = pl.BlockSpec((nb, tt, d_model), lambda i: (0, tile(i), 0))
    else:
        x_spec = pl.BlockSpec((rows, d_model), lambda i: (x_row0 // rows + tile(i), 0))
    in_specs = [x_spec, _layer_spec(prev, state_layer), _layer_spec(h0, state_layer)]
    in_specs += [const(w) for w in wts]
    args = [x_src, prev, h0, *wts]
    if aliased:
        in_specs.append(pl.BlockSpec(memory_space=pl.ANY))
        args.append(y_buf)
        y_spec = pl.BlockSpec((ROW_TILE, d_model), lambda i: (row0 // ROW_TILE, 0))
    else:
        y_spec = pl.BlockSpec((rows, d_model), lambda i: (row0 // rows + i, 0))
    out_shape = (jax.ShapeDtypeStruct((total_rows, d_model), F32),
                 jax.ShapeDtypeStruct((halo, pw), F32),
                 jax.ShapeDtypeStruct((nb, nst2), F32))
    out_specs = (y_spec,
                 pl.BlockSpec((halo, pw), lambda i: (0, 0)),
                 pl.BlockSpec((nb, nst2), lambda i: (0, 0)))
    zero_tail = not aliased
    assert aliased or row0 + (n_steps + 1) * rows == total_rows
    kern = functools.partial(_mix_kernel, n_steps=n_steps, zero_tail=zero_tail,
                             nb=nb, tt=tt, start_pos=start_pos, alpha=alpha, aliased=aliased)
    return pl.pallas_call(
        kern, grid=(n_steps + int(zero_tail),), in_specs=in_specs, out_specs=out_specs, out_shape=out_shape,
        scratch_shapes=[pltpu.VMEM((halo + rows, pw), F32), pltpu.VMEM((rows, nst2), F32),
                        pltpu.VMEM((nb, nst2), F32)],
        input_output_aliases={len(args) - 1: 0} if aliased else {},
        name="token_mix_decode" if aliased else "token_mix_prompt",
        compiler_params=_params("arbitrary"))(*args)


def _ffn_kernel(x_ref, wg_ref, wu_ref, wd_ref, g_ref, b_ref, o_ref, *, alpha):
    x = x_ref[...]
    ffn = _swiglu_split(x.astype(BF16), wg_ref, wu_ref, wd_ref)
    o_ref[...] = _layer_norm(alpha * x + ffn, g_ref[...], b_ref[...])


def _layer_spec(a, layer, **kw):
    return pl.BlockSpec((None,) + a.shape[1:], lambda *_, n=a.ndim - 1: (layer,) + (0,) * n, **kw)


def _dense_ffn(x_all, layer, wg, wu, wd, ln_g, ln_b, alpha):
    n, d_model = x_all.shape
    resident = lambda a: _layer_spec(a, layer, pipeline_mode=pl.Buffered(1))
    return pl.pallas_call(
        functools.partial(_ffn_kernel, alpha=alpha),
        grid=(n // ROW_TILE,),
        in_specs=[pl.BlockSpec((ROW_TILE, d_model), lambda i: (i, 0)),
                  resident(wg), resident(wu), resident(wd),
                  pl.BlockSpec((1, d_model), lambda i: (0, 0)), pl.BlockSpec((1, d_model), lambda i: (0, 0))],
        out_specs=pl.BlockSpec((ROW_TILE, d_model), lambda i: (i, 0)),
        out_shape=jax.ShapeDtypeStruct((n, d_model), F32),
        name="dense_swiglu_ln",
        compiler_params=_params("parallel"))(x_all, wg, wu, wd, ln_g, ln_b)


def _split_bf16(a):
    hi = a.astype(BF16)
    lo = (a - hi.astype(F32)).astype(BF16)
    return hi, lo


DEAD_RANK = -(1 << 20)


def _router_kernel(x_ref, w_ref, b_ref, idx_ref, gate_ref, cnt_ref, *, n_tokens):
    x_hi, x_lo = _split_bf16(x_ref[...])
    w_hi, w_lo = _split_bf16(w_ref[...])
    logits = _dot(x_hi, w_hi) + (_dot(x_lo, w_hi) + _dot(x_hi, w_lo)) + b_ref[...]
    lane = lax.broadcasted_iota(jnp.int32, logits.shape, 1)
    logits = jnp.where(lane < N_EXPERTS, logits, -jnp.inf)
    m1 = jnp.max(logits, axis=-1, keepdims=True)
    i1 = jnp.min(jnp.where(logits == m1, lane, LANES), axis=-1, keepdims=True)
    rest = jnp.where(lane == i1, -jnp.inf, logits)
    m2 = jnp.max(rest, axis=-1, keepdims=True)
    i2 = jnp.min(jnp.where(rest == m2, lane, LANES), axis=-1, keepdims=True)
    e2 = jnp.exp(m2 - m1)
    den = 1.0 + e2
    gate_ref[...] = jnp.where(lane == 0, 1.0 / den, e2 / den)

    row = lax.broadcasted_iota(jnp.int32, (ROW_TILE, 1), 0) + pl.program_id(0) * ROW_TILE
    live = row < n_tokens
    pick1 = jnp.where(jnp.logical_and(lane == i1, live), 1.0, 0.0)
    pick2 = jnp.where(jnp.logical_and(lane == i2, live), 1.0, 0.0)
    picks = pick1 + pick2
    r = lax.broadcasted_iota(jnp.int32, (ROW_TILE, ROW_TILE), 0)
    c = lax.broadcasted_iota(jnp.int32, (ROW_TILE, ROW_TILE), 1)
    earlier = _dot(jnp.where(c < r, 1.0, 0.0).astype(BF16), picks.astype(BF16))
    rank1 = jnp.sum(earlier * pick1, axis=-1, keepdims=True).astype(jnp.int32)
    rank2 = jnp.sum(earlier * pick2, axis=-1, keepdims=True).astype(jnp.int32)
    rank1 = jnp.where(live, rank1, DEAD_RANK)
    rank2 = jnp.where(live, rank2, DEAD_RANK)
    idx_ref[...] = jnp.where(lane == 0, i1, jnp.where(lane == 1, i2, jnp.where(lane == 2, rank1, rank2)))
    counts = jnp.sum(picks, axis=0, keepdims=True).astype(jnp.int32)
    cnt_ref[...] = jnp.broadcast_to(counts, cnt_ref.shape)


def _router(x_all, n_tokens, w_router, b_router):
    n, d_model = x_all.shape
    w = jnp.pad(w_router, ((0, 0), (0, LANES - N_EXPERTS)))
    b = jnp.pad(b_router, (0, LANES - N_EXPERTS))[None, :]
    n_tiles = n // ROW_TILE
    return pl.pallas_call(
        functools.partial(_router_kernel, n_tokens=n_tokens), grid=(n_tiles,),
        in_specs=[pl.BlockSpec((ROW_TILE, d_model), lambda i: (i, 0)),
                  pl.BlockSpec((d_model, LANES), lambda i: (0, 0)),
                  pl.BlockSpec((1, LANES), lambda i: (0, 0))],
        out_specs=(pl.BlockSpec((ROW_TILE, LANES), lambda i: (i, 0)),
                   pl.BlockSpec((ROW_TILE, LANES), lambda i: (i, 0)),
                   pl.BlockSpec((8, LANES), lambda i: (i, 0))),
        out_shape=(jax.ShapeDtypeStruct((n, LANES), jnp.int32), jax.ShapeDtypeStruct((n, LANES), F32),
                   jax.ShapeDtypeStruct((n_tiles * 8, LANES), jnp.int32)),
        name="moe_router",
        compiler_params=_params("parallel"))(x_all, w, b)


def _group_copy(src_ref, src_row, dst_ref, dst_row, sem):
    return pltpu.make_async_copy(src_ref.at[pl.ds(pl.multiple_of(src_row, GROUP), GROUP)],
                                 dst_ref.at[pl.ds(pl.multiple_of(dst_row, GROUP), GROUP)], sem)


def _wait_groups(count, src_ref, dst_ref, sem):
    def wait(_, carry):
        _group_copy(src_ref, 0, dst_ref, 0, sem).wait()
        return carry
    lax.fori_loop(0, count, wait, 0)


def _stage_rows(expert, rank, first_ref, step):
    pos = rank
    for e in range(N_EXPERTS):
        pos = pos + jnp.where(expert == e, first_ref[step * N_EXPERTS + e], 0)
    return pos


def _regroup_kernel(w0_ref, off_ref, ngrp_ref, part_ref, pend_ref, first_ref, x_ref, idx_ref, xs_hbm,
                    stage_ref, carry_ref, zero_ref, sem):
    step = pl.program_id(0)

    @pl.when(step == 0)
    def _():
        carry_ref[...] = jnp.zeros(carry_ref.shape, BF16)
        zero_ref[...] = jnp.zeros(zero_ref.shape, BF16)

    xb = x_ref[...].astype(BF16)
    stage_row = lax.broadcasted_iota(jnp.int32, (STAGE_ROWS, ROW_TILE), 0)
    idx = idx_ref[...]
    hit = None
    for choice in range(TOP_K):
        pos = _stage_rows(idx[choice:choice + 1, :], idx[TOP_K + choice:TOP_K + choice + 1, :], first_ref, step)
        hit = stage_row == pos if hit is None else jnp.logical_or(hit, stage_row == pos)
    stage_ref[...] = _dot(jnp.where(hit, 1.0, 0.0).astype(BF16), xb).astype(BF16)

    started = 0
    for e in range(N_EXPERTS):
        k = step * N_EXPERTS + e
        off, w0, ngrp = off_ref[k], w0_ref[k], ngrp_ref[k]
        head = pl.ds(pl.multiple_of(off, GROUP), GROUP)
        stage_ref[head, :] = stage_ref[head, :] + carry_ref[e]

        def send(q, carry, off=off, w0=w0):
            _group_copy(stage_ref, off + q * GROUP, xs_hbm, w0 + q * GROUP, sem).start()
            return carry

        lax.fori_loop(0, ngrp, send, 0)
        tail = stage_ref[pl.ds(pl.multiple_of(off + ngrp * GROUP, GROUP), GROUP), :]
        carry_ref[e] = jnp.where(part_ref[k] != 0, tail, jnp.zeros_like(tail))
        started = started + ngrp
    _wait_groups(started, stage_ref, xs_hbm, sem)

    @pl.when(step == pl.num_programs(0) - 1)
    def _():
        flushed = 0
        for e in range(N_EXPERTS):
            k = step * N_EXPERTS + e
            row = w0_ref[k] + ngrp_ref[k] * GROUP
            is_open = part_ref[k]

            @pl.when(is_open != 0)
            def _(row=row, e=e):
                pltpu.make_async_copy(carry_ref.at[e], xs_hbm.at[pl.ds(pl.multiple_of(row, GROUP), GROUP)],
                                      sem).start()

            row = row + is_open * GROUP
            n_zero = (pend_ref[e] - row) // GROUP

            def fill(q, carry, row=row):
                _group_copy(zero_ref, 0, xs_hbm, row + q * GROUP, sem).start()
                return carry

            lax.fori_loop(0, n_zero, fill, 0)
            flushed = flushed + is_open + n_zero
        _wait_groups(flushed, zero_ref, xs_hbm, sem)


def _regroup(x_all, idx_rows, tables, n_slots):
    n, d_model = x_all.shape
    return pl.pallas_call(
        _regroup_kernel,
        grid_spec=pltpu.PrefetchScalarGridSpec(
            num_scalar_prefetch=6, grid=(n // ROW_TILE,),
            in_specs=[pl.BlockSpec((ROW_TILE, d_model), lambda i, *_: (i, 0)),
                      pl.BlockSpec((2 * TOP_K, ROW_TILE), lambda i, *_: (0, i))],
            out_specs=pl.BlockSpec(memory_space=pl.ANY),
            scratch_shapes=[pltpu.VMEM((STAGE_ROWS, d_model), BF16),
                            pltpu.VMEM((N_EXPERTS, GROUP, d_model), BF16),
                            pltpu.VMEM((GROUP, d_model), BF16),
                            pltpu.SemaphoreType.DMA(())]),
        out_shape=jax.ShapeDtypeStruct((n_slots, d_model), BF16),
        name="moe_regroup",
        compiler_params=_params("arbitrary"))(
            tables["w0"], tables["off"], tables["ngrp"], tables["part"], tables["pend"], tables["first"],
            x_all, idx_rows)


def _moe_kernel(item_e_ref, item_b0_ref, item_nb_ref, item_half_ref, nvalid_ref, *refs):
    xs_refs = refs[:ITEM_BLOCKS]
    wg_ref, wu_ref, wd_ref, ys_hbm = refs[ITEM_BLOCKS:ITEM_BLOCKS + 4]
    *wgub_refs, wdb_ref, acc_ref, stage_ref, sem = refs[ITEM_BLOCKS + 4:]
    item, c = pl.program_id(0), pl.program_id(1)
    last_item, last_c = pl.num_programs(0) - 1, pl.num_programs(1) - 1
    n_blocks = ys_hbm.shape[0] // MOE_ROWS
    nb, b0 = item_nb_ref[item], item_b0_ref[item]

    def block_out(src_ref, block):
        rows = pl.ds(pl.multiple_of(block * MOE_ROWS, MOE_ROWS), MOE_ROWS)
        return pltpu.make_async_copy(src_ref, ys_hbm.at[rows], sem)

    def start_and_wait(count, copy_of):
        def start(q, carry):
            copy_of(q).start()
            return carry

        def wait(q, carry):
            copy_of(q).wait()
            return carry

        lax.fori_loop(0, count, start, 0)
        lax.fori_loop(0, count, wait, 0)

    @pl.when(jnp.logical_and(item == 0, c == 0))
    def _():
        acc_ref[...] = jnp.zeros(acc_ref.shape, F32)

    def cast_then_load(p):
        cols = slice(p * LANES, (p + 1) * LANES)
        wgub_refs[p][:, :LANES] = wg_ref[:, cols].astype(BF16)
        wgub_refs[p][:, LANES:] = wu_ref[:, cols].astype(BF16)
        return wgub_refs[p][...]

    def run_block(j, n_rows):
        if j == 0:
            wdb_ref[...] = wd_ref[...].astype(BF16)
        piece = cast_then_load if j == 0 else (lambda p: wgub_refs[p][...])
        part = _swiglu_interleaved(xs_refs[j][0:n_rows, :], piece, wdb_ref)
        total = jnp.where(c == 0, part, acc_ref[j, 0:n_rows, :] + part)
        acc_ref[j, 0:n_rows, :] = total
        stage_ref[j, 0:n_rows, :] = total.astype(BF16)
        if n_rows < MOE_ROWS:
            stage_ref[j, n_rows:, :] = jnp.zeros((MOE_ROWS - n_rows, stage_ref.shape[2]), BF16)

    for j in range(ITEM_BLOCKS):
        half = jnp.logical_and(j == nb - 1, item_half_ref[item] != 0)
        pl.when(jnp.logical_and(j < nb, jnp.logical_not(half)))(functools.partial(run_block, j, MOE_ROWS))
        pl.when(half)(functools.partial(run_block, j, MOE_ROWS // 2))

    @pl.when(c == last_c)
    def _():
        start_and_wait(nb, lambda q: block_out(stage_ref.at[q], b0 + q))

    @pl.when(jnp.logical_and(item == last_item, c == last_c))
    def _():
        stage_ref[0] = jnp.zeros(stage_ref.shape[1:], BF16)
        first_unused = nvalid_ref[0]
        start_and_wait(n_blocks - first_unused, lambda q: block_out(stage_ref.at[0], first_unused + q))


def _grouped_swiglu(xs, items, nvalid, layer, wg, wu, wd):
    n_slots, d_model = xs.shape
    d_ff = wg.shape[3]
    fc = MOE_FF_CHUNK
    n_chunks = d_ff // fc
    n_blocks = n_slots // MOE_ROWS
    item_e, item_b0, item_nb, item_half = items
    assert d_ff % fc == 0 and n_slots % MOE_ROWS == 0
    chunk = lambda c, nb, it: jnp.where(nb[it] > 0, c, n_chunks - 1)
    w_in_map = lambda it, c, e, b0, nb, *_: (layer, e[it], 0, chunk(c, nb, it))
    w_out_map = lambda it, c, e, b0, nb, *_: (layer, e[it], chunk(c, nb, it), 0)
    xs_spec = lambda k: pl.BlockSpec(
        (MOE_ROWS, d_model), lambda it, c, e, b0, *_: (jnp.minimum(b0[it] + k, n_blocks - 1), 0))
    return pl.pallas_call(
        _moe_kernel,
        grid_spec=pltpu.PrefetchScalarGridSpec(
            num_scalar_prefetch=5, grid=(item_e.shape[0], n_chunks),
            in_specs=[xs_spec(k) for k in range(ITEM_BLOCKS)] + [
                pl.BlockSpec((None, None, d_model, fc), w_in_map),
                pl.BlockSpec((None, None, d_model, fc), w_in_map),
                pl.BlockSpec((None, None, fc, d_model), w_out_map)],
            out_specs=pl.BlockSpec(memory_space=pl.ANY),
            scratch_shapes=[pltpu.VMEM((d_model, 2 * LANES), BF16) for _ in range(fc // LANES)] + [
                            pltpu.VMEM((fc, d_model), BF16),
                            pltpu.VMEM((ITEM_BLOCKS, MOE_ROWS, d_model), F32),
                            pltpu.VMEM((ITEM_BLOCKS, MOE_ROWS, d_model), BF16),
                            pltpu.SemaphoreType.DMA(())]),
        out_shape=jax.ShapeDtypeStruct((n_slots, d_model), BF16),
        name="moe_grouped_swiglu",
        compiler_params=_params("arbitrary", "arbitrary"))(
            item_e, item_b0, item_nb, item_half, nvalid, *([xs] * ITEM_BLOCKS), wg, wu, wd)


def _combine_kernel(w0_ref, off_ref, nwin_ref, first_ref, x_ref, idx_ref, gate_ref, g_ref, b_ref, ys_hbm,
                    *rest, alpha, batch_major):
    if batch_major:
        prompt_ref, tail_ref, stage_ref, sem = rest
    else:
        o_ref, stage_ref, sem = rest
    step = pl.program_id(0)

    def fetch_tile(tile, buf):
        stage_ref[buf] = jnp.zeros(stage_ref.shape[1:], BF16)
        for e in range(N_EXPERTS):
            k = tile * N_EXPERTS + e
            off, w0 = off_ref[k], w0_ref[k]

            def fetch(q, carry, off=off, w0=w0):
                _group_copy(ys_hbm, w0 + q * GROUP, stage_ref.at[buf], off + q * GROUP, sem.at[buf]).start()
                return carry

            lax.fori_loop(0, nwin_ref[k], fetch, 0)

    @pl.when(step == 0)
    def _():
        fetch_tile(step, 0)

    buf = lax.rem(step, 2)

    @pl.when(step + 1 < pl.num_programs(0))
    def _():
        fetch_tile(step + 1, 1 - buf)

    n_groups = 0
    for e in range(N_EXPERTS):
        n_groups = n_groups + nwin_ref[step * N_EXPERTS + e]
    _wait_groups(n_groups, ys_hbm, stage_ref.at[buf], sem.at[buf])

    stage_col = lax.broadcasted_iota(jnp.int32, (ROW_TILE, STAGE_ROWS), 1)
    idx = idx_ref[...]
    gates = gate_ref[...]
    ys = stage_ref[buf]
    f = None
    for choice in range(TOP_K):
        pos = _stage_rows(idx[:, choice:choice + 1], idx[:, TOP_K + choice:TOP_K + choice + 1], first_ref, step)
        pick = jnp.where(stage_col == pos, 1.0, 0.0).astype(BF16)
        term = gates[:, choice:choice + 1] * _dot(pick, ys)
        f = term if f is None else f + term
    out = _layer_norm(alpha * x_ref[...] + f, g_ref[...], b_ref[...])
    if not batch_major:
        o_ref[...] = out
        return
    nb, tt = prompt_ref.shape[0], prompt_ref.shape[1]
    last = pl.num_programs(0) - 1

    @pl.when(step < last)
    def _():
        for t in range(tt):
            prompt_ref[:, t, :] = out[t * nb:(t + 1) * nb, :]

    @pl.when(step == last)
    def _():
        tail_ref[...] = out


def _combine(x_all, idx, gates, tables, ys, ln_g, ln_b, alpha, prompt_shape=None):
    n, d_model = x_all.shape
    n_tiles = n // ROW_TILE
    if prompt_shape is None:
        out_specs = pl.BlockSpec((ROW_TILE, d_model), lambda i, *_: (i, 0))
        out_shape = jax.ShapeDtypeStruct((n, d_model), F32)
    else:
        batch, seq = prompt_shape
        tt = ROW_TILE // batch
        assert batch * seq == (n_tiles - 1) * ROW_TILE
        out_specs = (pl.BlockSpec((batch, tt, d_model), lambda i, *_: (0, jnp.minimum(i, n_tiles - 2), 0)),
                     pl.BlockSpec((ROW_TILE, d_model), lambda i, *_: (0, 0)))
        out_shape = (jax.ShapeDtypeStruct((batch, seq, d_model), F32),
                     jax.ShapeDtypeStruct((ROW_TILE, d_model), F32))
    return pl.pallas_call(
        functools.partial(_combine_kernel, alpha=alpha, batch_major=prompt_shape is not None),
        grid_spec=pltpu.PrefetchScalarGridSpec(
            num_scalar_prefetch=4, grid=(n_tiles,),
            in_specs=[pl.BlockSpec((ROW_TILE, d_model), lambda i, *_: (i, 0)),
                      pl.BlockSpec((ROW_TILE, LANES), lambda i, *_: (i, 0)),
                      pl.BlockSpec((ROW_TILE, LANES), lambda i, *_: (i, 0)),
                      pl.BlockSpec((1, d_model), lambda i, *_: (0, 0)),
                      pl.BlockSpec((1, d_model), lambda i, *_: (0, 0)),
                      pl.BlockSpec(memory_space=pl.ANY)],
            out_specs=out_specs,
            scratch_shapes=[pltpu.VMEM((2, STAGE_ROWS, d_model), BF16), pltpu.SemaphoreType.DMA((2,))]),
        out_shape=out_shape,
        name="moe_combine_ln",
        compiler_params=_params("arbitrary"))(
            tables["w0"], tables["off"], tables["nwin"], tables["first"], x_all, idx, gates, ln_g, ln_b, ys)


def _routing_tables(counts, n_slots):
    count = jnp.sum(counts, axis=0)
    blocks = (count + MOE_ROWS - 1) // MOE_ROWS
    blk_end = jnp.cumsum(blocks)
    base = (blk_end - blocks) * MOE_ROWS
    n_items = n_slots // MOE_ROWS // ITEM_BLOCKS + N_EXPERTS
    items_e = (blocks + ITEM_BLOCKS - 1) // ITEM_BLOCKS
    item_end = jnp.cumsum(items_e)
    n_used = item_end[-1]
    item = jnp.arange(n_items, dtype=jnp.int32)
    used_item = jnp.minimum(item, n_used - 1)
    item_e = jnp.minimum(jnp.sum((used_item[:, None] >= item_end[None, :]).astype(jnp.int32), axis=1),
                         N_EXPERTS - 1)
    of_item = lambda per_expert: jnp.sum(
        jnp.where(item_e[:, None] == jnp.arange(N_EXPERTS)[None, :], per_expert[None, :], 0), axis=1)
    local = used_item - of_item(item_end - items_e)
    item_b0 = of_item(blk_end - blocks) + ITEM_BLOCKS * local
    item_nb = jnp.where(item < n_used, jnp.clip(of_item(blocks) - ITEM_BLOCKS * local, 0, ITEM_BLOCKS), 0)
    tail_rows = count - (blocks - 1) * MOE_ROWS
    item_half = jnp.logical_and(jnp.logical_and(item < n_used, local == of_item(items_e) - 1),
                                of_item(tail_rows) <= MOE_ROWS // 2)
    items = tuple(a.astype(jnp.int32) for a in (item_e, item_b0, item_nb, item_half))
    start = base[None, :] + jnp.cumsum(counts, axis=0) - counts
    end = start + counts
    w0 = start // GROUP * GROUP
    ngrp = end // GROUP - start // GROUP
    part = (end % GROUP != 0).astype(jnp.int32)
    nwin = ngrp + part
    off = (jnp.cumsum(nwin, axis=1) - nwin) * GROUP
    nvalid = blk_end[-1]
    pend = (blk_end * MOE_ROWS).at[-1].set(n_slots)
    flat = lambda a: a.reshape(-1).astype(jnp.int32)
    tables = dict(w0=flat(w0), off=flat(off), first=flat(start - w0 + off), ngrp=flat(ngrp), part=flat(part),
                  nwin=flat(nwin), pend=flat(pend))
    return tables, items, nvalid.reshape(1).astype(jnp.int32)


def _moe_ffn(x_all, n_tokens, layer, w_router, b_router, wg, wu, wd, ln_g, ln_b, alpha, prompt_shape=None):
    idx, gates, cnt = _router(x_all, n_tokens, w_router, b_router)
    n_slots = TOP_K * n_tokens + N_EXPERTS * MOE_ROWS
    n_slots = -(-n_slots // (MOE_ROWS * ITEM_BLOCKS)) * (MOE_ROWS * ITEM_BLOCKS)
    tables, items, nvalid = _routing_tables(cnt[::8, :N_EXPERTS], n_slots)
    xs = _regroup(x_all, idx[:, :2 * TOP_K].T, tables, n_slots)
    ys = _grouped_swiglu(xs, items, nvalid, layer, wg, wu, wd)
    return _combine(x_all, idx, gates, tables, ys, ln_g, ln_b, alpha, prompt_shape)


def kernel(x_prompt, x_sample, state_pool, state_ssm_re, state_ssm_im, ln1_g, ln1_b, w_in, b_in, w_pool, pool_scale, lambda_re, lambda_im, log_dt, b_re, b_im, c_re, c_im, d_skip, w_glu, b_glu, w_proj_a, w_proj_b, w_out, ln2_g, ln2_b, w_ffn_gate, w_ffn_up, w_ffn_down, w_router, b_router, w_moe_gate, w_moe_up, w_moe_down):
    batch, seq, d_model = x_prompt.shape
    dec_batch = x_sample.shape[0]
    depth = w_in.shape[0]
    n_groups, n_state = lambda_re.shape[1], lambda_re.shape[2]
    chan = b_re.shape[-1]
    nst = n_groups * n_state
    pw = state_pool.shape[-1]
    alpha = (2.0 * depth) ** 0.25
    n_prompt = batch * seq
    n_tokens = n_prompt + dec_batch
    total_rows = -(-n_tokens // ROW_TILE) * ROW_TILE
    assert x_sample.shape[1] == 1 and n_prompt % ROW_TILE == 0 and seq % MIX_STEPS == 0
    assert batch * MIX_STEPS == ROW_TILE and dec_batch <= ROW_TILE

    rows_of = lambda a: a[:, None, :]
    a_re, a_im, bb_re, bb_im = _ssm_params(lambda_re, lambda_im, log_dt, b_re, b_im)
    lam = jnp.concatenate([a_re[::chan].reshape(depth, 1, nst), a_im[::chan].reshape(depth, 1, nst)], axis=2)
    wts = (w_in.astype(BF16), rows_of(b_in), w_pool.astype(BF16), rows_of(pool_scale), lam,
           _block_diag_in(bb_re, depth, n_groups, chan, n_state).astype(BF16),
           _block_diag_in(bb_im, depth, n_groups, chan, n_state).astype(BF16),
           _block_diag_out(c_re, depth, n_groups, chan, n_state).astype(BF16),
           _block_diag_out(c_im, depth, n_groups, chan, n_state).astype(BF16),
           d_skip.reshape(depth, 1, -1), w_glu.astype(BF16), rows_of(b_glu),
           w_proj_a.astype(BF16), w_proj_b.astype(BF16), w_out.astype(BF16),
           rows_of(ln1_g), rows_of(ln1_b))
    ffn_wts = (w_ffn_gate.astype(BF16), w_ffn_up.astype(BF16), w_ffn_down.astype(BF16))
    prev_s = jnp.transpose(state_pool, (0, 2, 1, 3)).reshape(depth, POOL_STATE * dec_batch, pw)
    h0_s = jnp.concatenate([state_ssm_re.reshape(depth, dec_batch, nst),
                            state_ssm_im.reshape(depth, dec_batch, nst)], axis=2)
    zero_prev = jnp.zeros((1, POOL_STATE * batch, pw), F32)
    zero_h = jnp.zeros((1, batch, 2 * nst), F32)

    x_p = x_prompt
    x_s = x_sample.reshape(dec_batch, d_model)
    s_row0 = 0
    outs = [[] for _ in range(6)]
    for l in range(depth):
        y_mix, pool_p, h_p = _token_mix(x_p, 0, 0, seq // MIX_STEPS, batch, MIX_STEPS, 0, alpha, l,
                                        zero_prev, zero_h, wts, total_rows, state_layer=0)
        y_mix, pool_s, h_s = _token_mix(x_s, s_row0, n_prompt, 1, dec_batch, 1, PAST_LEN, alpha, l,
                                        prev_s, h0_s, wts, total_rows, y_buf=y_mix)
        outs[0].append(jnp.transpose(pool_p.reshape(POOL_STATE, batch, pw), (1, 0, 2)))
        outs[1].append(h_p[:, :nst].reshape(batch, n_groups, n_state))
        outs[2].append(h_p[:, nst:].reshape(batch, n_groups, n_state))
        outs[3].append(jnp.transpose(pool_s.reshape(POOL_STATE, dec_batch, pw), (1, 0, 2)))
        outs[4].append(h_s[:, :nst].reshape(dec_batch, n_groups, n_state))
        outs[5].append(h_s[:, nst:].reshape(dec_batch, n_groups, n_state))
        j = l // 2
        if l % 2 == 0:
            y_all = _dense_ffn(y_mix, j, *ffn_wts, ln2_g[l][None, :], ln2_b[l][None, :], alpha)
        else:
            prompt_shape = (batch, seq) if l == depth - 1 else None
            y_all = _moe_ffn(y_mix, n_tokens, j, w_router[j], b_router[j], w_moe_gate, w_moe_up,
                             w_moe_down, ln2_g[l][None, :], ln2_b[l][None, :], alpha, prompt_shape)
        x_p = x_s = y_all
        s_row0 = n_prompt
    if isinstance(y_all, (tuple, list)):
        y_p, tail = y_all
        y_s = tail[:dec_batch].reshape(dec_batch, 1, d_model)
    else:
        y_p = jnp.transpose(y_all[:n_prompt].reshape(seq, batch, d_model), (1, 0, 2))
        y_s = y_all[n_prompt:n_tokens].reshape(dec_batch, 1, d_model)
    return (y_p, y_s, *[jnp.stack(o) for o in outs])
```
